```python
import jax
import jax.numpy as jnp
from jax import lax
import numpy as np

D_MODEL = 1024
BATCH = 8
SEQ = 2048
DEPTH = 1
DEC_BATCH = 128
DEC_SEQ = 4
PAST_LEN = 16384
PAGE_SIZE = 128

GLA_HEADS = 4
GLA_DK = D_MODEL // 8
GLA_DV = D_MODEL // 4
GLA_KEY_W = GLA_HEADS * GLA_DK
GLA_VAL_W = GLA_HEADS * GLA_DV
GLA_GATE_RANK = 16
GLA_GATE_NORM = 16.0
GLA_CHUNK = 64
POOL_WINDOWS = (2, 4, 8, 16)
POOL_GROUPS = 4
POOL_W = D_MODEL
POOL_G = POOL_W // POOL_GROUPS
POOL_BUF = max(POOL_WINDOWS) - 1
MEM_LEN = 256
X_HEADS = 4
X_HEAD_DIM = D_MODEL // X_HEADS
D_FF = 4 * D_MODEL
EPS = 1e-6
IN_SIZES = (GLA_KEY_W, GLA_KEY_W, GLA_VAL_W, GLA_GATE_RANK, GLA_VAL_W, POOL_W, D_MODEL, D_MODEL)
IN_WIDTH = sum(IN_SIZES)

kernel_name = 'gla_pool_hybrid_decoder'


def rmsnorm(x, g):
    xf = x.astype(jnp.float32)
    y = xf * lax.rsqrt(jnp.mean(xf * xf, axis=-1, keepdims=True) + EPS)
    return (y * g.astype(jnp.float32)).astype(x.dtype)


def split_points():
    pts, acc = [], 0
    for s in IN_SIZES[:-1]:
        acc += s
        pts.append(acc)
    return pts


def gla_chunked(q, k, v, log_a, s0):
    B, T, H, DK = q.shape
    DV = v.shape[-1]
    C = min(GLA_CHUNK, T)
    n = -(-T // C)
    pad = n * C - T

    def prep(a):
        a = jnp.pad(a.astype(jnp.float32), ((0, 0), (0, pad), (0, 0), (0, 0)))
        return a.reshape(B, n, C, H, a.shape[-1]).transpose(1, 0, 2, 3, 4)

    causal = jnp.tril(jnp.ones((C, C), dtype=bool))[None, :, :, None, None]

    def step(S, inp):
        qc, kc, vc, ac = inp
        b = jnp.cumsum(ac, axis=1)
        decay = jnp.exp(jnp.where(causal, b[:, :, None] - b[:, None, :], -jnp.inf))
        attn = jnp.einsum('bthk,btshk,bshk->bhts', qc, decay, kc)
        o = (jnp.einsum('bhts,bshv->bthv', attn, vc)
             + jnp.einsum('bthk,bhkv->bthv', qc * jnp.exp(b), S))
        b_end = b[:, -1]
        k_dec = kc * jnp.exp(b_end[:, None] - b)
        S = jnp.exp(b_end)[..., None] * S + jnp.einsum('bshk,bshv->bhkv', k_dec, vc)
        return S, o

    S, o = lax.scan(step, s0.astype(jnp.float32), (prep(q), prep(k), prep(v), prep(log_a)))
    o = o.transpose(1, 0, 2, 3, 4).reshape(B, n * C, H, DV)[:, :T]
    return o.astype(q.dtype), S.astype(s0.dtype)


def pool_mix(u, buf, pos0, w_mix, scale):
    B, T, P = u.shape
    L = POOL_BUF
    ext = jnp.concatenate([buf.astype(u.dtype), u], axis=1).astype(jnp.float32)
    cs = jnp.concatenate([jnp.zeros((B, 1, P), jnp.float32), jnp.cumsum(ext, axis=1)], axis=1)
    hi = cs[:, L + 1:L + 1 + T]
    pos = pos0 + jnp.arange(T)
    means = []
    for g, w in enumerate(POOL_WINDOWS):
        sl = slice(g * POOL_G, (g + 1) * POOL_G)
        win = hi[..., sl] - cs[:, L + 1 - w:L + 1 - w + T, sl]
        cnt = jnp.minimum(pos + 1, w).astype(jnp.float32)[None, :, None]
        means.append(win / cnt)
    d = (jnp.concatenate(means, axis=-1) - u.astype(jnp.float32)).reshape(B, T, POOL_GROUPS, POOL_G)
    y = jnp.einsum('btgc,gcd->btgd', d, w_mix.astype(jnp.float32)).reshape(B, T, P)
    y = y * scale.astype(jnp.float32)
    return y.astype(u.dtype), ext[:, -L:].astype(buf.dtype)


def token_mix(xn, s0, buf, pos0, p):
    B, T, _ = xn.shape
    proj = xn @ p['w_in']
    q, k, v, gz, og, u, ga, gb = jnp.split(proj, split_points(), axis=-1)
    q = q.reshape(B, T, GLA_HEADS, GLA_DK) * (GLA_DK ** -0.5)
    k = k.reshape(B, T, GLA_HEADS, GLA_DK)
    v = v.reshape(B, T, GLA_HEADS, GLA_DV)
    log_a = jax.nn.log_sigmoid((gz @ p['w_gk_up'] + p['b_gk']).astype(jnp.float32)) / GLA_GATE_NORM
    log_a = log_a.reshape(B, T, GLA_HEADS, GLA_DK)
    o, s_new = gla_chunked(q, k, v, log_a, s0)
    o = rmsnorm(o, p['gla_norm_g']).reshape(B, T, GLA_VAL_W) * jax.nn.silu(og)
    branch_a = o @ p['w_branch_a']
    pooled, buf_new = pool_mix(u, buf, pos0, p['w_pool_mix'], p['pool_scale'])
    branch_b = pooled @ p['w_branch_b']
    merged = jax.nn.sigmoid(ga) * branch_a + jax.nn.sigmoid(gb) * branch_b
    return merged @ p['w_out'], s_new, buf_new


def mem_kv(mem, g, wk, wv):
    B, M, _ = mem.shape
    m = rmsnorm(mem, g)
    return ((m @ wk).reshape(B, M, X_HEADS, X_HEAD_DIM),
            (m @ wv).reshape(B, M, X_HEADS, X_HEAD_DIM))


def cross_attn(hn, mk, mv, wq, wo):
    B, T, _ = hn.shape
    q = (hn @ wq).reshape(B, T, X_HEADS, X_HEAD_DIM).astype(jnp.float32)
    s = jnp.einsum('bthd,bmhd->bhtm', q, mk.astype(jnp.float32)) * (X_HEAD_DIM ** -0.5)
    pr = jax.nn.softmax(s, axis=-1)
    o = jnp.einsum('bhtm,bmhd->bthd', pr, mv.astype(jnp.float32)).astype(hn.dtype)
    return o.reshape(B, T, D_MODEL) @ wo


def sq_relu_mlp(x, w_up, w_down):
    h = jax.nn.relu(x @ w_up)
    return (h * h) @ w_down


def decoder_layer(x, mk, mv, s0, buf, pos0, p):
    mix, s_new, buf_new = token_mix(rmsnorm(x, p['norm_mix_g']), s0, buf, pos0, p)
    h = x + mix
    h = h + cross_attn(rmsnorm(h, p['norm_x_g']), mk, mv, p['w_xq'], p['w_xo'])
    h = h + sq_relu_mlp(rmsnorm(h, p['norm_mlp_g']), p['w_up'], p['w_down'])
    return h, s_new, buf_new


def setup_inputs(seed: int = 0) -> dict:
    key = jax.random.key(seed)
    ks = jax.random.split(key, 32)
    f32 = jnp.float32

    def nrm(k, shape, scale=1.0):
        return jax.random.normal(k, shape, f32) * scale

    L = DEPTH
    return {
        'x_prompt': nrm(ks[0], (BATCH, SEQ, D_MODEL)),
        'x_sample': nrm(ks[1], (DEC_BATCH, DEC_SEQ, D_MODEL)),
        'mem_prompt': nrm(ks[2], (BATCH, MEM_LEN, D_MODEL)),
        'state_gla': nrm(ks[3], (L, DEC_BATCH, GLA_HEADS, GLA_DK, GLA_DV)),
        'state_pool': nrm(ks[4], (L, DEC_BATCH, POOL_BUF, POOL_W)),
        'cache_mem_k': nrm(ks[5], (L, DEC_BATCH, MEM_LEN, X_HEADS, X_HEAD_DIM)),
        'cache_mem_v': nrm(ks[6], (L, DEC_BATCH, MEM_LEN, X_HEADS, X_HEAD_DIM)),
        'norm_mix_g': 1.0 + nrm(ks[7], (L, D_MODEL), 0.02),
        'w_in': nrm(ks[8], (L, D_MODEL, IN_WIDTH), D_MODEL ** -0.5),
        'w_gk_up': nrm(ks[9], (L, GLA_GATE_RANK, GLA_KEY_W), GLA_GATE_RANK ** -0.5),
        'b_gk': nrm(ks[10], (L, GLA_KEY_W), 0.02),
        'gla_norm_g': 1.0 + nrm(ks[11], (L, GLA_DV), 0.02),
        'w_pool_mix': nrm(ks[12], (L, POOL_GROUPS, POOL_G, POOL_G), POOL_G ** -0.5),
        'pool_scale': 1.0 + nrm(ks[13], (L, POOL_W), 0.02),
        'w_branch_a': nrm(ks[14], (L, GLA_VAL_W, D_MODEL), GLA_VAL_W ** -0.5),
        'w_branch_b': nrm(ks[15], (L, POOL_W, D_MODEL), POOL_W ** -0.5),
        'w_out': nrm(ks[16], (L, D_MODEL, D_MODEL), D_MODEL ** -0.5),
        'norm_x_g': 1.0 + nrm(ks[17], (L, D_MODEL), 0.02),
        'norm_mem_g': 1.0 + nrm(ks[18], (L, D_MODEL), 0.02),
        'w_xq': nrm(ks[19], (L, D_MODEL, D_MODEL), D_MODEL ** -0.5),
        'w_xk': nrm(ks[20], (L, D_MODEL, D_MODEL), D_MODEL ** -0.5),
        'w_xv': nrm(ks[21], (L, D_MODEL, D_MODEL), D_MODEL ** -0.5),
        'w_xo': nrm(ks[22], (L, D_MODEL, D_MODEL), D_MODEL ** -0.5),
        'norm_mlp_g': 1.0 + nrm(ks[23], (L, D_MODEL), 0.02),
        'w_up': nrm(ks[24], (L, D_MODEL, D_FF), D_MODEL ** -0.5),
        'w_down': nrm(ks[25], (L, D_FF, D_MODEL), D_FF ** -0.5),
        'norm_final_g': 1.0 + nrm(ks[26], (D_MODEL,), 0.02),
    }


def reference(x_prompt, x_sample, mem_prompt, state_gla, state_pool, cache_mem_k, cache_mem_v,
              norm_mix_g, w_in, w_gk_up, b_gk, gla_norm_g, w_pool_mix, pool_scale,
              w_branch_a, w_branch_b, w_out, norm_x_g, norm_mem_g, w_xq, w_xk, w_xv, w_xo,
              norm_mlp_g, w_up, w_down, norm_final_g):
    yp, ys = x_prompt, x_sample
    mk_p_all, mv_p_all, sg_p_all, sg_s_all, sp_p_all, sp_s_all = [], [], [], [], [], []
    for l in range(DEPTH):
        p = {
            'norm_mix_g': norm_mix_g[l], 'w_in': w_in[l], 'w_gk_up': w_gk_up[l], 'b_gk': b_gk[l],
            'gla_norm_g': gla_norm_g[l], 'w_pool_mix': w_pool_mix[l], 'pool_scale': pool_scale[l],
            'w_branch_a': w_branch_a[l], 'w_branch_b': w_branch_b[l], 'w_out': w_out[l],
            'norm_x_g': norm_x_g[l], 'w_xq': w_xq[l], 'w_xo': w_xo[l],
            'norm_mlp_g': norm_mlp_g[l], 'w_up': w_up[l], 'w_down': w_down[l],
        }
        mk_p, mv_p = mem_kv(mem_prompt, norm_mem_g[l], w_xk[l], w_xv[l])
        s0_p = jnp.zeros((yp.shape[0], GLA_HEADS, GLA_DK, GLA_DV), jnp.float32)
        buf_p = jnp.zeros((yp.shape[0], POOL_BUF, POOL_W), yp.dtype)
        yp, sg_p, sp_p = decoder_layer(yp, mk_p, mv_p, s0_p, buf_p, 0, p)
        ys, sg_s, sp_s = decoder_layer(ys, cache_mem_k[l], cache_mem_v[l], state_gla[l],
                                       state_pool[l], PAST_LEN, p)
        mk_p_all.append(mk_p)
        mv_p_all.append(mv_p)
        sg_p_all.append(sg_p)
        sg_s_all.append(sg_s)
        sp_p_all.append(sp_p)
        sp_s_all.append(sp_s)
    y_prompt = rmsnorm(yp, norm_final_g)
    y_sample = rmsnorm(ys, norm_final_g)
    return (y_prompt, y_sample, jnp.stack(mk_p_all), jnp.stack(mv_p_all),
            jnp.stack(sg_p_all), jnp.stack(sg_s_all), jnp.stack(sp_p_all), jnp.stack(sp_s_all))
```

```python
import functools

import jax
import jax.numpy as jnp
from jax import lax
from jax.experimental import pallas as pl
from jax.experimental.pallas import tpu as pltpu

F32 = jnp.float32
BF16 = jnp.bfloat16

D_MODEL = 1024
GLA_HEADS = 4
GLA_DK = 128
GLA_DV = 256
GLA_KEY_W = GLA_HEADS * GLA_DK
GLA_VAL_W = GLA_HEADS * GLA_DV
GLA_GATE_RANK = 16
GLA_GATE_NORM = 16.0
POOL_WINDOWS = (2, 4, 8, 16)
POOL_G = 256
POOL_BUF = 15
MEM_LEN = 256
X_HEADS = 4
X_HEAD_DIM = 256
D_FF = 4096
EPS = 1e-6
PAST_LEN = 16384

LANE = 128
HALO = 16
PAD = 8
VMEM_LIMIT = 52 * 1024 * 1024

COL_Q = 0
COL_K = 512
COL_V = 1024
COL_OG = 2048
COL_U = 3072
COL_GA = 4096
COL_GB = 5120
COL_GZ = 6144
PROJ_W = 6272
PROJ_TN = 896


def _params(sem):
    return pltpu.CompilerParams(dimension_semantics=sem, vmem_limit_bytes=VMEM_LIMIT)


def _rmsnorm(x, g):
    return x * lax.rsqrt(jnp.mean(x * x, axis=-1, keepdims=True) + EPS) * g


def _dot(a, b):
    return jnp.dot(a, b, preferred_element_type=F32)


def _dot_nt(a, b):
    return lax.dot_general(a, b, (((1,), (1,)), ((), ())), preferred_element_type=F32)


def _dot_tn(a, b):
    return lax.dot_general(a, b, (((0,), (0,)), ((), ())), preferred_element_type=F32)


def _norm_matmul_kernel(x_ref, g_ref, w_ref, o_ref, xn_ref):
    @pl.when(pl.program_id(1) == 0)
    def _():
        xn_ref[...] = _rmsnorm(x_ref[...], g_ref[...]).astype(BF16)

    o_ref[...] = _dot(xn_ref[...], w_ref[...]).astype(o_ref.dtype)


def norm_matmul(x, g, w, tm, tn, out_dtype=F32):
    m, k = x.shape
    n = w.shape[1]
    return pl.pallas_call(
        _norm_matmul_kernel,
        out_shape=jax.ShapeDtypeStruct((m, n), out_dtype),
        grid=(m // tm, n // tn),
        in_specs=[
            pl.BlockSpec((tm, k), lambda i, j: (i, 0)),
            pl.BlockSpec((1, k), lambda i, j: (0, 0)),
            pl.BlockSpec((k, tn), lambda i, j: (0, j)),
        ],
        out_specs=pl.BlockSpec((tm, tn), lambda i, j: (i, j)),
        scratch_shapes=[pltpu.VMEM((tm, k), BF16)],
        compiler_params=_params(("parallel", "arbitrary")),
        name="norm_matmul",
    )(x, g.reshape(1, k), w)


def _matmul_res_kernel(x_ref, w_ref, r_ref, o_ref):
    o_ref[...] = _dot(x_ref[...].astype(BF16), w_ref[...]) + r_ref[...]


def matmul_residual(x, w, res, tm):
    m, k = x.shape
    n = w.shape[1]
    return pl.pallas_call(
        _matmul_res_kernel,
        out_shape=jax.ShapeDtypeStruct((m, n), F32),
        grid=(m // tm,),
        in_specs=[
            pl.BlockSpec((tm, k), lambda i: (i, 0)),
            pl.BlockSpec((k, n), lambda i: (0, 0)),
            pl.BlockSpec((tm, n), lambda i: (i, 0)),
        ],
        out_specs=pl.BlockSpec((tm, n), lambda i: (i, 0)),
        compiler_params=_params(("parallel",)),
        name="matmul_residual",
    )(x, w, res)


def _gla_kernel(q_ref, k_ref, v_ref, gz_ref, og_ref, wgk_ref, bgk_ref, gn_ref, s0_ref,
                o_ref, s_ref, lahi_ref, lalo_ref, *, rows, seg, zero_init):
    tt = q_ref.shape[0]
    nchunks = tt // rows
    nseg = rows // seg
    seg_shift = seg.bit_length() - 1

    @pl.when(pl.program_id(1) == 0)
    def _():
        if zero_init:
            s_ref[...] = jnp.zeros(s_ref.shape, F32)
        else:
            s_ref[...] = s0_ref[...]

    z = _dot(gz_ref[...].astype(BF16), wgk_ref[...]) + bgk_ref[...]
    la = (jnp.minimum(z, 0.0) - jnp.log(1.0 + jnp.exp(-jnp.abs(z)))) * (1.0 / GLA_GATE_NORM)
    la_hi = la.astype(BF16)
    lahi_ref[...] = la_hi
    lalo_ref[...] = (la - la_hi.astype(F32)).astype(BF16)

    ri = lax.broadcasted_iota(jnp.int32, (rows, rows), 0)
    ci = lax.broadcasted_iota(jnp.int32, (rows, rows), 1)
    same_seg = (ri >> seg_shift) == (ci >> seg_shift)
    causal = jnp.logical_and(same_seg, ci <= ri)
    l_cum = jnp.where(causal, 1.0, 0.0).astype(BF16)
    l_seg = jnp.where(same_seg, 1.0, 0.0).astype(BF16)
    row_seg = lax.broadcasted_iota(jnp.int32, (rows, GLA_DK), 0) >> seg_shift
    row_seg_v = lax.broadcasted_iota(jnp.int32, (rows, GLA_DV), 0) >> seg_shift
    gn = gn_ref[...]
    qscale = GLA_DK ** -0.5

    def chunk(c, carry):
        r0 = pl.multiple_of(c * rows, rows)
        rsl = pl.ds(r0, rows)
        lah = lahi_ref[rsl, :]
        lal = lalo_ref[rsl, :]
        b = _dot(l_cum, lah) + _dot(l_cum, lal)
        b_end = _dot(l_seg, lah) + _dot(l_seg, lal)
        q = q_ref[rsl, :]
        k = k_ref[rsl, :]
        qt = (q * qscale) * jnp.exp(b)
        kt = k * jnp.exp(-b)
        kd = k * jnp.exp(b_end - b)
        qt_b = qt.astype(BF16)
        kt_b = kt.astype(BF16)
        for h in range(GLA_HEADS):
            ks = slice(h * GLA_DK, (h + 1) * GLA_DK)
            vs = slice(h * GLA_DV, (h + 1) * GLA_DV)
            v_h = v_ref[rsl, vs].astype(BF16)
            a = _dot_nt(qt_b[:, ks], kt_b[:, ks])
            a = jnp.where(causal, a, 0.0).astype(BF16)
            o_h = _dot(a, v_h)
            for j in range(nseg):
                s_old = s_ref[j, h]
                inter = _dot(qt_b[:, ks], s_old.astype(BF16))
                if nseg == 1:
                    o_h = o_h + inter
                    kd_j = kd[:, ks].astype(BF16)
                    ones_j = jnp.ones((rows, LANE), BF16)
                else:
                    o_h = o_h + jnp.where(row_seg_v == j, inter, 0.0)
                    kd_j = jnp.where(row_seg == j, kd[:, ks], 0.0).astype(BF16)
                    ones_j = jnp.where(row_seg == j, 1.0, 0.0).astype(BF16)
                upd = _dot_tn(kd_j, v_h)
                dcol = _dot_tn(lah[:, ks], ones_j) + _dot_tn(lal[:, ks], ones_j)
                e = jnp.exp(dcol)
                s_ref[j, h] = s_old * jnp.concatenate([e, e], axis=1) + upd
            on = _rmsnorm(o_h, gn)
            og = og_ref[rsl, vs]
            o_ref[rsl, vs] = (on * (og * jax.nn.sigmoid(og))).astype(o_ref.dtype)
        return carry

    lax.fori_loop(0, nchunks, chunk, 0)


def gla(proj, wgk, bgk, gn, s0, *, groups, tt, rows, seg, out_dtype=BF16):
    m = proj.shape[0]
    steps = m // (groups * tt)
    nseg = rows // seg
    zero_init = s0 is None
    nstate = groups * nseg
    if zero_init:
        s0 = jnp.zeros((nseg, GLA_HEADS, 8, LANE), F32)
        s0_spec = pl.BlockSpec((nseg, GLA_HEADS, 8, LANE), lambda g, t: (0, 0, 0, 0))
    else:
        s0_spec = pl.BlockSpec((nseg, GLA_HEADS, GLA_DK, GLA_DV), lambda g, t: (g, 0, 0, 0))

    def row_spec(width, col):
        blk = col // width
        return pl.BlockSpec((tt, width), lambda g, t: (g * steps + t, blk))

    kern = functools.partial(_gla_kernel, rows=rows, seg=seg, zero_init=zero_init)
    return pl.pallas_call(
        kern,
        out_shape=(jax.ShapeDtypeStruct((m, GLA_VAL_W), out_dtype),
                   jax.ShapeDtypeStruct((nstate, GLA_HEADS, GLA_DK, GLA_DV), F32)),
        grid=(groups, steps),
        in_specs=[
            row_spec(GLA_KEY_W, COL_Q),
            row_spec(GLA_KEY_W, COL_K),
            row_spec(GLA_VAL_W, COL_V),
            row_spec(LANE, COL_GZ),
            row_spec(GLA_VAL_W, COL_OG),
            pl.BlockSpec((LANE, GLA_KEY_W), lambda g, t: (0, 0)),
            pl.BlockSpec((1, GLA_KEY_W), lambda g, t: (0, 0)),
            pl.BlockSpec((1, GLA_DV), lambda g, t: (0, 0)),
            s0_spec,
        ],
        out_specs=(pl.BlockSpec((tt, GLA_VAL_W), lambda g, t: (g * steps + t, 0)),
                   pl.BlockSpec((nseg, GLA_HEADS, GLA_DK, GLA_DV), lambda g, t: (g, 0, 0, 0))),
        scratch_shapes=[pltpu.VMEM((tt, GLA_KEY_W), BF16), pltpu.VMEM((tt, GLA_KEY_W), BF16)],
        compiler_params=_params(("parallel", "arbitrary")),
        name="gla",
    )(proj, proj, proj, proj, proj, wgk, bgk.reshape(1, GLA_KEY_W), gn.reshape(1, GLA_DV), s0)


def _pool_sample_kernel(ext_ref, o_ref, *, pos0):
    t_new = o_ref.shape[0]
    for t in range(t_new):
        cur = POOL_BUF + t
        for g, w in enumerate(POOL_WINDOWS):
            cs = slice(g * POOL_G, (g + 1) * POOL_G)
            win = ext_ref[cur, :, cs]
            for j in range(1, w):
                win = win + ext_ref[cur - j, :, cs]
            cnt = float(min(pos0 + t + 1, w))
            o_ref[t, :, cs] = win * (1.0 / cnt) - ext_ref[cur, :, cs]


def pool_sample(ext_tm, pos0):
    rows_t, nseq, width = ext_tm.shape
    t_new = rows_t - POOL_BUF
    nb = 64
    return pl.pallas_call(
        functools.partial(_pool_sample_kernel, pos0=pos0),
        out_shape=jax.ShapeDtypeStruct((t_new, nseq, width), F32),
        grid=(nseq // nb,),
        in_specs=[pl.BlockSpec((rows_t, nb, width), lambda i: (0, i, 0))],
        out_specs=pl.BlockSpec((t_new, nb, width), lambda i: (0, i, 0)),
        compiler_params=_params(("parallel",)),
        name="pool_sample",
    )(ext_tm)


def _pool_diff_tile(u_ref, halo_ref, ext_ref, l2_ref, l4_ref, l8_ref, tiles_per_seq):
    tm = u_ref.shape[0]
    n = tm + HALO
    t_in_seq = pl.program_id(0) % tiles_per_seq
    zeros_pad = jnp.zeros((PAD, D_MODEL), F32)
    ext_ref[0:PAD, :] = zeros_pad
    halo = halo_ref[...]
    ext_ref[PAD:PAD + HALO, :] = jnp.where(t_in_seq == 0, 0.0, halo)
    ext_ref[PAD + HALO:, :] = u_ref[...]
    l2_ref[0:PAD, :] = zeros_pad
    l4_ref[0:PAD, :] = zeros_pad[:, :3 * POOL_G]
    l2_ref[PAD:PAD + n, :] = ext_ref[PAD:PAD + n, :] + ext_ref[PAD - 1:PAD - 1 + n, :]
    l4_ref[PAD:PAD + n, :] = l2_ref[PAD:PAD + n, POOL_G:] + l2_ref[PAD - 2:PAD - 2 + n, POOL_G:]
    l8_ref[PAD:PAD + n, :] = l4_ref[PAD:PAD + n, POOL_G:] + l4_ref[PAD - 4:PAD - 4 + n, POOL_G:]
    base = PAD + HALO
    wins = (
        l2_ref[base:base + tm, 0:POOL_G],
        l4_ref[base:base + tm, 0:POOL_G],
        l8_ref[base:base + tm, 0:POOL_G],
        l8_ref[base:base + tm, POOL_G:] + l8_ref[base - 8:base - 8 + tm, POOL_G:],
    )
    pos1 = (t_in_seq * tm + 1 + lax.broadcasted_iota(jnp.int32, (tm, 1), 0)).astype(F32)
    out = []
    for g, w in enumerate(POOL_WINDOWS):
        inv = 1.0 / jnp.minimum(pos1, float(w))
        out.append(wins[g] * inv - u_ref[:, g * POOL_G:(g + 1) * POOL_G])
    return out


def _mix_out_kernel(*refs, fused_pool, tiles_per_seq):
    if fused_pool:
        (o_ref, u_ref, halo_ref, ga_ref, gb_ref, x_ref, wmix_ref, ps_ref, wa_ref, wb_ref, wo_ref,
         h_ref, ext_ref, l2_ref, l4_ref, l8_ref) = refs
        diffs = _pool_diff_tile(u_ref, halo_ref, ext_ref, l2_ref, l4_ref, l8_ref, tiles_per_seq)
    else:
        (o_ref, d_ref, ga_ref, gb_ref, x_ref, wmix_ref, ps_ref, wa_ref, wb_ref, wo_ref, h_ref) = refs
        diffs = [d_ref[:, g * POOL_G:(g + 1) * POOL_G] for g in range(len(POOL_WINDOWS))]
    pooled = []
    for g in range(len(POOL_WINDOWS)):
        y = _dot(diffs[g].astype(BF16), wmix_ref[g]) * ps_ref[:, g * POOL_G:(g + 1) * POOL_G]
        pooled.append(y.astype(BF16))
    pooled = jnp.concatenate(pooled, axis=1)
    branch_b = _dot(pooled, wb_ref[...])
    branch_a = _dot(o_ref[...], wa_ref[...])
    merged = jax.nn.sigmoid(ga_ref[...]) * branch_a + jax.nn.sigmoid(gb_ref[...]) * branch_b
    h_ref[...] = x_ref[...] + _dot(merged.astype(BF16), wo_ref[...])


def mix_out(o, proj, d, x, wmix, pscale, wa, wb, wo, *, tm, seq_len):
    m = x.shape[0]
    fused_pool = d is None
    wide = D_MODEL

    def col_spec(col):
        blk = col // wide
        return pl.BlockSpec((tm, wide), lambda i: (i, blk))

    row_spec = pl.BlockSpec((tm, wide), lambda i: (i, 0))
    const2 = lambda shape: pl.BlockSpec(shape, lambda i: (0, 0))
    w_specs = [
        pl.BlockSpec((len(POOL_WINDOWS), POOL_G, POOL_G), lambda i: (0, 0, 0)),
        const2((1, wide)),
        const2((GLA_VAL_W, wide)),
        const2((wide, wide)),
        const2((wide, wide)),
    ]
    if fused_pool:
        halo_blk = tm // HALO
        ucol = COL_U // wide
        in_specs = [row_spec, col_spec(COL_U),
                    pl.BlockSpec((HALO, wide), lambda i: (jnp.maximum(i * halo_blk - 1, 0), ucol)),
                    col_spec(COL_GA), col_spec(COL_GB), row_spec] + w_specs
        args = (o, proj, proj, proj, proj, x)
        scratch = [pltpu.VMEM((tm + PAD + HALO, wide), F32),
                   pltpu.VMEM((tm + PAD + HALO, wide), F32),
                   pltpu.VMEM((tm + PAD + HALO, 3 * POOL_G), F32),
                   pltpu.VMEM((tm + PAD + HALO, 2 * POOL_G), F32)]
        tiles_per_seq = seq_len // tm
    else:
        in_specs = [row_spec, row_spec, col_spec(COL_GA), col_spec(COL_GB), row_spec] + w_specs
        args = (o, d, proj, proj, x)
        scratch = []
        tiles_per_seq = 1
    kern = functools.partial(_mix_out_kernel, fused_pool=fused_pool, tiles_per_seq=tiles_per_seq)
    return pl.pallas_call(
        kern,
        out_shape=jax.ShapeDtypeStruct((m, wide), F32),
        grid=(m // tm,),
        in_specs=in_specs,
        out_specs=row_spec,
        scratch_shapes=scratch,
        compiler_params=_params(("parallel",)),
        name="mix_out",
    )(*args, wmix, pscale.reshape(1, wide), wa, wb, wo)


def _softmax_rows(s):
    p = jnp.exp(s - jnp.max(s, axis=-1, keepdims=True))
    return p, 1.0 / jnp.sum(p, axis=-1, keepdims=True)


def _xattn_prompt_kernel(h_ref, g_ref, wq_ref, mk_ref, mv_ref, wo_ref, o_ref):
    h = h_ref[...]
    hn = _rmsnorm(h, g_ref[...]).astype(BF16)
    q = (_dot(hn, wq_ref[...]) * (X_HEAD_DIM ** -0.5)).astype(BF16)
    outs = []
    for hd in range(X_HEADS):
        cs = slice(hd * X_HEAD_DIM, (hd + 1) * X_HEAD_DIM)
        kb = mk_ref[0, :, cs].astype(BF16)
        vb = mv_ref[0, :, cs].astype(BF16)
        p, inv = _softmax_rows(_dot_nt(q[:, cs], kb))
        outs.append((_dot(p.astype(BF16), vb) * inv).astype(BF16))
    o = jnp.concatenate(outs, axis=1)
    o_ref[...] = h + _dot(o, wo_ref[...])


def xattn_prompt(h, g, wq, mk, mv, wo, *, tm, seq_len):
    m, d = h.shape
    tiles = seq_len // tm
    row_spec = pl.BlockSpec((tm, d), lambda i: (i, 0))
    mem_spec = pl.BlockSpec((1, MEM_LEN, d), lambda i: (i // tiles, 0, 0))
    w_spec = pl.BlockSpec((d, d), lambda i: (0, 0))
    return pl.pallas_call(
        _xattn_prompt_kernel,
        out_shape=jax.ShapeDtypeStruct((m, d), F32),
        grid=(m // tm,),
        in_specs=[row_spec, pl.BlockSpec((1, d), lambda i: (0, 0)), w_spec, mem_spec, mem_spec, w_spec],
        out_specs=row_spec,
        compiler_params=_params(("parallel",)),
        name="xattn_prompt",
    )(h, g.reshape(1, d), wq, mk, mv, wo)


def _xattn_sample_kernel(q_ref, k_ref, v_ref, o_ref, *, t_new):
    nseq = k_ref.shape[0]
    rows = q_ref.shape[0]
    q = (q_ref[...] * (X_HEAD_DIM ** -0.5)).astype(BF16)
    row_seq = lax.broadcasted_iota(jnp.int32, (rows, X_HEAD_DIM), 0) >> (t_new.bit_length() - 1)
    for hd in range(X_HEADS):
        cs = slice(hd * X_HEAD_DIM, (hd + 1) * X_HEAD_DIM)
        s = jnp.zeros((rows, MEM_LEN), F32)
        for j in range(nseq):
            s = jnp.where(row_seq == j, _dot_nt(q[:, cs], k_ref[j, :, cs].astype(BF16)), s)
        p, inv = _softmax_rows(s)
        p = p.astype(BF16)
        o = jnp.zeros((rows, X_HEAD_DIM), F32)
        for j in range(nseq):
            o = jnp.where(row_seq == j, _dot(p, v_ref[j, :, cs].astype(BF16)), o)
        o_ref[:, cs] = (o * inv).astype(o_ref.dtype)


def xattn_sample(q, ck, cv, *, t_new, nseq):
    m, d = q.shape
    rows = nseq * t_new
    kv_spec = pl.BlockSpec((nseq, MEM_LEN, d), lambda i: (i, 0, 0))
    row_spec = pl.BlockSpec((rows, d), lambda i: (i, 0))
    return pl.pallas_call(
        functools.partial(_xattn_sample_kernel, t_new=t_new),
        out_shape=jax.ShapeDtypeStruct((m, d), BF16),
        grid=(m // rows,),
        in_specs=[row_spec, kv_spec, kv_spec],
        out_specs=row_spec,
        compiler_params=_params(("parallel",)),
        name="xattn_sample",
    )(q, ck, cv)


def _mlp_kernel(h_ref, g_ref, wu_ref, wd_ref, gf_ref, y_ref, hn_ref, acc_ref):
    f = pl.program_id(1)

    @pl.when(f == 0)
    def _():
        h = h_ref[...]
        hn_ref[...] = _rmsnorm(h, g_ref[...]).astype(BF16)
        acc_ref[...] = h

    a = jnp.maximum(_dot(hn_ref[...], wu_ref[...]), 0.0)
    acc_ref[...] += _dot((a * a).astype(BF16), wd_ref[...])

    @pl.when(f == pl.num_programs(1) - 1)
    def _():
        y_ref[...] = _rmsnorm(acc_ref[...], gf_ref[...])


def mlp_final(h, g, wu, wd, gf, *, tm, tf):
    m, d = h.shape
    ff = wu.shape[1]
    row_spec = pl.BlockSpec((tm, d), lambda i, f: (i, 0))
    vec_spec = pl.BlockSpec((1, d), lambda i, f: (0, 0))
    return pl.pallas_call(
        _mlp_kernel,
        out_shape=jax.ShapeDtypeStruct((m, d), F32),
        grid=(m // tm, ff // tf),
        in_specs=[row_spec, vec_spec,
                  pl.BlockSpec((d, tf), lambda i, f: (0, f)),
                  pl.BlockSpec((tf, d), lambda i, f: (f, 0)),
                  vec_spec],
        out_specs=row_spec,
        scratch_shapes=[pltpu.VMEM((tm, d), BF16), pltpu.VMEM((tm, d), F32)],
        compiler_params=_params(("parallel", "arbitrary")),
        name="mlp_final",
    )(h, g.reshape(1, d), wu, wd, gf.reshape(1, d))


def kernel(x_prompt, x_sample, mem_prompt, state_gla, state_pool, cache_mem_k, cache_mem_v,
           norm_mix_g, w_in, w_gk_up, b_gk, gla_norm_g, w_pool_mix, pool_scale,
           w_branch_a, w_branch_b, w_out, norm_x_g, norm_mem_g, w_xq, w_xk, w_xv, w_xo,
           norm_mlp_g, w_up, w_down, norm_final_g):
    depth = w_in.shape[0]
    assert depth == 1
    batch, seq, d = x_prompt.shape
    dec_batch, dec_seq, _ = x_sample.shape
    mp = batch * seq
    ms = dec_batch * dec_seq

    w0 = w_in[0]
    gz_lo = COL_OG
    gz_hi = gz_lo + GLA_GATE_RANK
    w_in_r = jnp.concatenate(
        [w0[:, :gz_lo], w0[:, gz_hi:], w0[:, gz_lo:gz_hi],
         jnp.zeros((d, PROJ_W - COL_GZ - GLA_GATE_RANK), F32)], axis=1).astype(BF16)
    wgk = jnp.concatenate(
        [w_gk_up[0], jnp.zeros((LANE - GLA_GATE_RANK, GLA_KEY_W), F32)], axis=0).astype(BF16)
    wmix = w_pool_mix[0].astype(BF16)
    wa = w_branch_a[0].astype(BF16)
    wb = w_branch_b[0].astype(BF16)
    wo = w_out[0].astype(BF16)
    wxq = w_xq[0].astype(BF16)
    wxk = w_xk[0].astype(BF16)
    wxv = w_xv[0].astype(BF16)
    wxo = w_xo[0].astype(BF16)
    wu = w_up[0].astype(BF16)
    wd = w_down[0].astype(BF16)

    xp = x_prompt.reshape(mp, d)
    xs = x_sample.reshape(ms, d)

    mem = mem_prompt.reshape(batch * MEM_LEN, d)
    mk_p = norm_matmul(mem, norm_mem_g[0], wxk, tm=1024, tn=1024)
    mv_p = norm_matmul(mem, norm_mem_g[0], wxv, tm=1024, tn=1024)
    proj_p = norm_matmul(xp, norm_mix_g[0], w_in_r, tm=1024, tn=PROJ_TN)
    o_p, sg_p = gla(proj_p, wgk, b_gk[0], gla_norm_g[0], None,
                    groups=batch, tt=512, rows=128, seg=128)
    h_p = mix_out(o_p, proj_p, None, xp, wmix, pool_scale[0], wa, wb, wo, tm=512, seq_len=seq)
    h_p = xattn_prompt(h_p, norm_x_g[0], wxq, mk_p.reshape(batch, MEM_LEN, d),
                       mv_p.reshape(batch, MEM_LEN, d), wxo, tm=512, seq_len=seq)
    y_p = mlp_final(h_p, norm_mlp_g[0], wu, wd, norm_final_g, tm=1024, tf=1024)
    sp_p = proj_p[:, COL_U:COL_U + d].reshape(batch, seq, d)[:, seq - POOL_BUF:]

    proj_s = norm_matmul(xs, norm_mix_g[0], w_in_r, tm=ms, tn=PROJ_TN)
    seqs_per_step = 16
    o_s, sg_s = gla(proj_s, wgk, b_gk[0], gla_norm_g[0], state_gla[0],
                    groups=dec_batch // seqs_per_step, tt=seqs_per_step * dec_seq,
                    rows=seqs_per_step * dec_seq, seg=dec_seq)
    u_s = proj_s[:, COL_U:COL_U + d].reshape(dec_batch, dec_seq, d)
    ext_tm = jnp.concatenate([state_pool[0], u_s], axis=1).transpose(1, 0, 2)
    d_s = pool_sample(ext_tm, PAST_LEN).transpose(1, 0, 2).reshape(ms, d)
    h_s = mix_out(o_s, proj_s, d_s, xs, wmix, pool_scale[0], wa, wb, wo, tm=ms, seq_len=dec_seq)
    q_s = norm_matmul(h_s, norm_x_g[0], wxq, tm=ms, tn=d)
    a_s = xattn_sample(q_s, cache_mem_k[0].reshape(dec_batch, MEM_LEN, d),
                       cache_mem_v[0].reshape(dec_batch, MEM_LEN, d), t_new=dec_seq, nseq=4)
    h_s = matmul_residual(a_s, wxo, h_s, tm=ms)
    y_s = mlp_final(h_s, norm_mlp_g[0], wu, wd, norm_final_g, tm=ms, tf=1024)
    sp_s = jnp.concatenate([state_pool[0][:, dec_seq:], u_s], axis=1)

    return (y_p.reshape(batch, seq, d),
            y_s.reshape(dec_batch, dec_seq, d),
            mk_p.reshape(1, batch, MEM_LEN, X_HEADS, X_HEAD_DIM),
            mv_p.reshape(1, batch, MEM_LEN, X_HEADS, X_HEAD_DIM),
            sg_p[None],
            sg_s[None],
            sp_p[None],
            sp_s[None])
```

```python
import functools

import jax
import jax.numpy as jnp
from jax import lax
from jax.experimental import pallas as pl
from jax.experimental.pallas import tpu as pltpu

F32 = jnp.float32
BF16 = jnp.bfloat16

D_MODEL = 1024
GLA_HEADS = 4
GLA_DK = 128
GLA_DV = 256
GLA_KEY_W = GLA_HEADS * GLA_DK
GLA_VAL_W = GLA_HEADS * GLA_DV
GLA_GATE_RANK = 16
GLA_GATE_NORM = 16.0
POOL_WINDOWS = (2, 4, 8, 16)
POOL_G = 256
POOL_BUF = 15
MEM_LEN = 256
X_HEADS = 4
X_HEAD_DIM = 256
D_FF = 4096
EPS = 1e-6
PAST_LEN = 16384

LANE = 128
HALO = 16
PAD = 8
VMEM_LIMIT = 52 * 1024 * 1024

COL_Q = 0
COL_K = 512
COL_V = 1024
COL_OG = 2048
COL_U = 3072
COL_GA = 4096
COL_GB = 5120
COL_GZ = 6144
PROJ_W = 6272
PROJ_TN = 896
XQ_ROWS = 32


def _params(sem):
    return pltpu.CompilerParams(dimension_semantics=sem, vmem_limit_bytes=VMEM_LIMIT)


def _rmsnorm(x, g):
    return x * lax.rsqrt(jnp.mean(x * x, axis=-1, keepdims=True) + EPS) * g


def _dot(a, b):
    return jnp.dot(a, b, preferred_element_type=F32)


def _dot_nt(a, b):
    return lax.dot_general(a, b, (((1,), (1,)), ((), ())), preferred_element_type=F32)


def _dot_tn(a, b):
    return lax.dot_general(a, b, (((0,), (0,)), ((), ())), preferred_element_type=F32)


def _norm_matmul_kernel(x_ref, g_ref, w_ref, o_ref, xn_ref):
    @pl.when(pl.program_id(1) == 0)
    def _():
        xn_ref[...] = _rmsnorm(x_ref[...], g_ref[...]).astype(BF16)

    o_ref[...] = _dot(xn_ref[...], w_ref[...]).astype(o_ref.dtype)


def norm_matmul(x, g, w, tm, tn, out_dtype=F32):
    m, k = x.shape
    n = w.shape[1]
    return pl.pallas_call(
        _norm_matmul_kernel,
        out_shape=jax.ShapeDtypeStruct((m, n), out_dtype),
        grid=(m // tm, n // tn),
        in_specs=[
            pl.BlockSpec((tm, k), lambda i, j: (i, 0)),
            pl.BlockSpec((1, k), lambda i, j: (0, 0)),
            pl.BlockSpec((k, tn), lambda i, j: (0, j)),
        ],
        out_specs=pl.BlockSpec((tm, tn), lambda i, j: (i, j)),
        scratch_shapes=[pltpu.VMEM((tm, k), BF16)],
        compiler_params=_params(("parallel", "arbitrary")),
        name="norm_matmul",
    )(x, g.reshape(1, k), w)


def _matmul_res_kernel(x_ref, w_ref, r_ref, o_ref):
    o_ref[...] = _dot(x_ref[...].astype(BF16), w_ref[...]) + r_ref[...]


def matmul_residual(x, w, res, tm):
    m, k = x.shape
    n = w.shape[1]
    return pl.pallas_call(
        _matmul_res_kernel,
        out_shape=jax.ShapeDtypeStruct((m, n), F32),
        grid=(m // tm,),
        in_specs=[
            pl.BlockSpec((tm, k), lambda i: (i, 0)),
            pl.BlockSpec((k, n), lambda i: (0, 0)),
            pl.BlockSpec((tm, n), lambda i: (i, 0)),
        ],
        out_specs=pl.BlockSpec((tm, n), lambda i: (i, 0)),
        compiler_params=_params(("parallel",)),
        name="matmul_residual",
    )(x, w, res)


def _gla_kernel(q_ref, k_ref, v_ref, gz_ref, og_ref, wgk_ref, bgk_ref, gn_ref, s0_ref,
                o_ref, s_ref, lahi_ref, lalo_ref, *, rows, seg, zero_init):
    tt = q_ref.shape[0]
    nchunks = tt // rows
    nseg = rows // seg
    seg_shift = seg.bit_length() - 1

    @pl.when(pl.program_id(1) == 0)
    def _():
        if zero_init:
            s_ref[...] = jnp.zeros(s_ref.shape, F32)
        else:
            s_ref[...] = s0_ref[...]

    z = _dot(gz_ref[...].astype(BF16), wgk_ref[...]) + bgk_ref[...]
    la = (jnp.minimum(z, 0.0) - jnp.log(1.0 + jnp.exp(-jnp.abs(z)))) * (1.0 / GLA_GATE_NORM)
    la_hi = la.astype(BF16)
    lahi_ref[...] = la_hi
    lalo_ref[...] = (la - la_hi.astype(F32)).astype(BF16)

    ri = lax.broadcasted_iota(jnp.int32, (rows, rows), 0)
    ci = lax.broadcasted_iota(jnp.int32, (rows, rows), 1)
    same_seg = (ri >> seg_shift) == (ci >> seg_shift)
    causal = jnp.logical_and(same_seg, ci <= ri)
    l_cum = jnp.where(causal, 1.0, 0.0).astype(BF16)
    l_seg = jnp.where(same_seg, 1.0, 0.0).astype(BF16)
    row_seg = lax.broadcasted_iota(jnp.int32, (rows, GLA_DK), 0) >> seg_shift
    row_seg_v = lax.broadcasted_iota(jnp.int32, (rows, GLA_DV), 0) >> seg_shift
    gn = gn_ref[...]
    qscale = GLA_DK ** -0.5

    def chunk(c, carry):
        r0 = pl.multiple_of(c * rows, rows)
        rsl = pl.ds(r0, rows)
        lah = lahi_ref[rsl, :]
        lal = lalo_ref[rsl, :]
        b = _dot(l_cum, lah) + _dot(l_cum, lal)
        b_end = _dot(l_seg, lah) + _dot(l_seg, lal)
        q = q_ref[rsl, :]
        k = k_ref[rsl, :]
        qt = (q * qscale) * jnp.exp(b)
        kt = k * jnp.exp(-b)
        kd = k * jnp.exp(b_end - b)
        qt_b = qt.astype(BF16)
        kt_b = kt.astype(BF16)
        for h in range(GLA_HEADS):
            ks = slice(h * GLA_DK, (h + 1) * GLA_DK)
            vs = slice(h * GLA_DV, (h + 1) * GLA_DV)
            v_h = v_ref[rsl, vs].astype(BF16)
            a = _dot_nt(qt_b[:, ks], kt_b[:, ks])
            a = jnp.where(causal, a, 0.0).astype(BF16)
            o_h = _dot(a, v_h)
            for j in range(nseg):
                s_old = s_ref[j, h]
                inter = _dot(qt_b[:, ks], s_old.astype(BF16))
                if nseg == 1:
                    o_h = o_h + inter
                    kd_j = kd[:, ks].astype(BF16)
                    ones_j = jnp.ones((rows, LANE), BF16)
                else:
                    o_h = o_h + jnp.where(row_seg_v == j, inter, 0.0)
                    kd_j = jnp.where(row_seg == j, kd[:, ks], 0.0).astype(BF16)
                    ones_j = jnp.where(row_seg == j, 1.0, 0.0).astype(BF16)
                upd = _dot_tn(kd_j, v_h)
                dcol = _dot_tn(lah[:, ks], ones_j) + _dot_tn(lal[:, ks], ones_j)
                e = jnp.exp(dcol)
                s_ref[j, h] = s_old * jnp.concatenate([e, e], axis=1) + upd
            on = _rmsnorm(o_h, gn)
            og = og_ref[rsl, vs]
            o_ref[rsl, vs] = (on * (og * jax.nn.sigmoid(og))).astype(o_ref.dtype)
        return carry

    lax.fori_loop(0, nchunks, chunk, 0)


def gla(proj, wgk, bgk, gn, s0, *, groups, tt, rows, seg, out_dtype=BF16):
    m = proj.shape[0]
    steps = m // (groups * tt)
    nseg = rows // seg
    zero_init = s0 is None
    nstate = groups * nseg
    if zero_init:
        s0 = jnp.zeros((nseg, GLA_HEADS, 8, LANE), F32)
        s0_spec = pl.BlockSpec((nseg, GLA_HEADS, 8, LANE), lambda g, t: (0, 0, 0, 0))
    else:
        s0_spec = pl.BlockSpec((nseg, GLA_HEADS, GLA_DK, GLA_DV), lambda g, t: (g, 0, 0, 0))

    def row_spec(width, col):
        blk = col // width
        return pl.BlockSpec((tt, width), lambda g, t: (g * steps + t, blk))

    kern = functools.partial(_gla_kernel, rows=rows, seg=seg, zero_init=zero_init)
    return pl.pallas_call(
        kern,
        out_shape=(jax.ShapeDtypeStruct((m, GLA_VAL_W), out_dtype),
                   jax.ShapeDtypeStruct((nstate, GLA_HEADS, GLA_DK, GLA_DV), F32)),
        grid=(groups, steps),
        in_specs=[
            row_spec(GLA_KEY_W, COL_Q),
            row_spec(GLA_KEY_W, COL_K),
            row_spec(GLA_VAL_W, COL_V),
            row_spec(LANE, COL_GZ),
            row_spec(GLA_VAL_W, COL_OG),
            pl.BlockSpec((LANE, GLA_KEY_W), lambda g, t: (0, 0)),
            pl.BlockSpec((1, GLA_KEY_W), lambda g, t: (0, 0)),
            pl.BlockSpec((1, GLA_DV), lambda g, t: (0, 0)),
            s0_spec,
        ],
        out_specs=(pl.BlockSpec((tt, GLA_VAL_W), lambda g, t: (g * steps + t, 0)),
                   pl.BlockSpec((nseg, GLA_HEADS, GLA_DK, GLA_DV), lambda g, t: (g, 0, 0, 0))),
        scratch_shapes=[pltpu.VMEM((tt, GLA_KEY_W), BF16), pltpu.VMEM((tt, GLA_KEY_W), BF16)],
        compiler_params=_params(("parallel", "arbitrary")),
        name="gla",
    )(proj, proj, proj, proj, proj, wgk, bgk.reshape(1, GLA_KEY_W), gn.reshape(1, GLA_DV), s0)


def _pool_sample_kernel(ext_ref, o_ref, *, pos0):
    t_new = o_ref.shape[0]
    for t in range(t_new):
        cur = POOL_BUF + t
        for g, w in enumerate(POOL_WINDOWS):
            cs = slice(g * POOL_G, (g + 1) * POOL_G)
            win = ext_ref[cur, :, cs]
            for j in range(1, w):
                win = win + ext_ref[cur - j, :, cs]
            cnt = float(min(pos0 + t + 1, w))
            o_ref[t, :, cs] = win * (1.0 / cnt) - ext_ref[cur, :, cs]


def pool_sample(ext_tm, pos0):
    rows_t, nseq, width = ext_tm.shape
    t_new = rows_t - POOL_BUF
    nb = 64
    return pl.pallas_call(
        functools.partial(_pool_sample_kernel, pos0=pos0),
        out_shape=jax.ShapeDtypeStruct((t_new, nseq, width), F32),
        grid=(nseq // nb,),
        in_specs=[pl.BlockSpec((rows_t, nb, width), lambda i: (0, i, 0))],
        out_specs=pl.BlockSpec((t_new, nb, width), lambda i: (0, i, 0)),
        compiler_params=_params(("parallel",)),
        name="pool_sample",
    )(ext_tm)


def _pool_diff_tile(u_ref, halo_ref, ext_ref, l2_ref, l4_ref, l8_ref, tiles_per_seq):
    tm = u_ref.shape[0]
    n = tm + HALO
    t_in_seq = pl.program_id(0) % tiles_per_seq
    zeros_pad = jnp.zeros((PAD, D_MODEL), F32)
    ext_ref[0:PAD, :] = zeros_pad
    halo = halo_ref[...]
    ext_ref[PAD:PAD + HALO, :] = jnp.where(t_in_seq == 0, 0.0, halo)
    ext_ref[PAD + HALO:, :] = u_ref[...]
    l2_ref[0:PAD, :] = zeros_pad
    l4_ref[0:PAD, :] = zeros_pad[:, :3 * POOL_G]
    l2_ref[PAD:PAD + n, :] = ext_ref[PAD:PAD + n, :] + ext_ref[PAD - 1:PAD - 1 + n, :]
    l4_ref[PAD:PAD + n, :] = l2_ref[PAD:PAD + n, POOL_G:] + l2_ref[PAD - 2:PAD - 2 + n, POOL_G:]
    l8_ref[PAD:PAD + n, :] = l4_ref[PAD:PAD + n, POOL_G:] + l4_ref[PAD - 4:PAD - 4 + n, POOL_G:]
    base = PAD + HALO
    wins = (
        l2_ref[base:base + tm, 0:POOL_G],
        l4_ref[base:base + tm, 0:POOL_G],
        l8_ref[base:base + tm, 0:POOL_G],
        l8_ref[base:base + tm, POOL_G:] + l8_ref[base - 8:base - 8 + tm, POOL_G:],
    )
    pos1 = (t_in_seq * tm + 1 + lax.broadcasted_iota(jnp.int32, (tm, 1), 0)).astype(F32)
    out = []
    for g, w in enumerate(POOL_WINDOWS):
        inv = 1.0 / jnp.minimum(pos1, float(w))
        out.append(wins[g] * inv - u_ref[:, g * POOL_G:(g + 1) * POOL_G])
    return out


def _mix_out_kernel(*refs, fused_pool, tiles_per_seq):
    if fused_pool:
        (o_ref, u_ref, halo_ref, ga_ref, gb_ref, x_ref, wmix_ref, ps_ref, wa_ref, wb_ref, wo_ref,
         h_ref, ext_ref, l2_ref, l4_ref, l8_ref) = refs
        diffs = _pool_diff_tile(u_ref, halo_ref, ext_ref, l2_ref, l4_ref, l8_ref, tiles_per_seq)
    else:
        (o_ref, d_ref, ga_ref, gb_ref, x_ref, wmix_ref, ps_ref, wa_ref, wb_ref, wo_ref, h_ref) = refs
        diffs = [d_ref[:, g * POOL_G:(g + 1) * POOL_G] for g in range(len(POOL_WINDOWS))]
    pooled = []
    for g in range(len(POOL_WINDOWS)):
        y = _dot(diffs[g].astype(BF16), wmix_ref[g]) * ps_ref[:, g * POOL_G:(g + 1) * POOL_G]
        pooled.append(y.astype(BF16))
    pooled = jnp.concatenate(pooled, axis=1)
    branch_b = _dot(pooled, wb_ref[...])
    branch_a = _dot(o_ref[...], wa_ref[...])
    merged = jax.nn.sigmoid(ga_ref[...]) * branch_a + jax.nn.sigmoid(gb_ref[...]) * branch_b
    h_ref[...] = x_ref[...] + _dot(merged.astype(BF16), wo_ref[...])


def mix_out(o, proj, d, x, wmix, pscale, wa, wb, wo, *, tm, seq_len):
    m = x.shape[0]
    fused_pool = d is None
    wide = D_MODEL

    def col_spec(col):
        blk = col // wide
        return pl.BlockSpec((tm, wide), lambda i: (i, blk))

    row_spec = pl.BlockSpec((tm, wide), lambda i: (i, 0))
    const2 = lambda shape: pl.BlockSpec(shape, lambda i: (0, 0))
    w_specs = [
        pl.BlockSpec((len(POOL_WINDOWS), POOL_G, POOL_G), lambda i: (0, 0, 0)),
        const2((1, wide)),
        const2((GLA_VAL_W, wide)),
        const2((wide, wide)),
        const2((wide, wide)),
    ]
    if fused_pool:
        halo_blk = tm // HALO
        ucol = COL_U // wide
        in_specs = [row_spec, col_spec(COL_U),
                    pl.BlockSpec((HALO, wide), lambda i: (jnp.maximum(i * halo_blk - 1, 0), ucol)),
                    col_spec(COL_GA), col_spec(COL_GB), row_spec] + w_specs
        args = (o, proj, proj, proj, proj, x)
        scratch = [pltpu.VMEM((tm + PAD + HALO, wide), F32),
                   pltpu.VMEM((tm + PAD + HALO, wide), F32),
                   pltpu.VMEM((tm + PAD + HALO, 3 * POOL_G), F32),
                   pltpu.VMEM((tm + PAD + HALO, 2 * POOL_G), F32)]
        tiles_per_seq = seq_len // tm
    else:
        in_specs = [row_spec, row_spec, col_spec(COL_GA), col_spec(COL_GB), row_spec] + w_specs
        args = (o, d, proj, proj, x)
        scratch = []
        tiles_per_seq = 1
    kern = functools.partial(_mix_out_kernel, fused_pool=fused_pool, tiles_per_seq=tiles_per_seq)
    return pl.pallas_call(
        kern,
        out_shape=jax.ShapeDtypeStruct((m, wide), F32),
        grid=(m // tm,),
        in_specs=in_specs,
        out_specs=row_spec,
        scratch_shapes=scratch,
        compiler_params=_params(("parallel",)),
        name="mix_out",
    )(*args, wmix, pscale.reshape(1, wide), wa, wb, wo)


def _softmax_rows(s):
    p = jnp.exp(s - jnp.max(s, axis=-1, keepdims=True))
    return p, 1.0 / jnp.sum(p, axis=-1, keepdims=True)


def _xattn_prompt_kernel(h_ref, g_ref, wq_ref, mk_ref, mv_ref, wo_ref, o_ref):
    h = h_ref[...]
    hn = _rmsnorm(h, g_ref[...]).astype(BF16)
    q = (_dot(hn, wq_ref[...]) * (X_HEAD_DIM ** -0.5)).astype(BF16)
    outs = []
    for hd in range(X_HEADS):
        cs = slice(hd * X_HEAD_DIM, (hd + 1) * X_HEAD_DIM)
        kb = mk_ref[0, :, cs].astype(BF16)
        vb = mv_ref[0, :, cs].astype(BF16)
        p, inv = _softmax_rows(_dot_nt(q[:, cs], kb))
        outs.append((_dot(p.astype(BF16), vb) * inv).astype(BF16))
    o = jnp.concatenate(outs, axis=1)
    o_ref[...] = h + _dot(o, wo_ref[...])


def xattn_prompt(h, g, wq, mk, mv, wo, *, tm, seq_len):
    m, d = h.shape
    tiles = seq_len // tm
    row_spec = pl.BlockSpec((tm, d), lambda i: (i, 0))
    mem_spec = pl.BlockSpec((1, MEM_LEN, d), lambda i: (i // tiles, 0, 0))
    w_spec = pl.BlockSpec((d, d), lambda i: (0, 0))
    return pl.pallas_call(
        _xattn_prompt_kernel,
        out_shape=jax.ShapeDtypeStruct((m, d), F32),
        grid=(m // tm,),
        in_specs=[row_spec, pl.BlockSpec((1, d), lambda i: (0, 0)), w_spec, mem_spec, mem_spec, w_spec],
        out_specs=row_spec,
        compiler_params=_params(("parallel",)),
        name="xattn_prompt",
    )(h, g.reshape(1, d), wq, mk, mv, wo)


def _xattn_sample_kernel(q_ref, k_ref, v_ref, o_ref):
    nseq, krows, _ = k_ref.shape
    half = XQ_ROWS // 2
    lane = lax.broadcasted_iota(jnp.int32, (half, krows), 1)
    row = lax.broadcasted_iota(jnp.int32, (half, krows), 0)
    valid = (lane & 7) == (row >> 2)
    for j in range(nseq):
        rs = slice(j * XQ_ROWS, (j + 1) * XQ_ROWS)
        q = (q_ref[rs, :] * (X_HEAD_DIM ** -0.5)).astype(BF16)
        g = _dot_nt(q, k_ref[j].astype(BF16))
        s = g[:half] + pltpu.roll(g[half:], krows - 4, axis=1)
        s = jnp.where(valid, s, -1e30)
        p = jnp.exp(s - jnp.max(s, axis=-1, keepdims=True))
        inv = 1.0 / jnp.sum(p, axis=-1, keepdims=True)
        pe = jnp.concatenate([p, pltpu.roll(p, 4, axis=1)], axis=0).astype(BF16)
        o = _dot(pe, v_ref[j].astype(BF16)) * jnp.concatenate([inv, inv], axis=0)
        o_ref[rs, :] = o.astype(o_ref.dtype)


def xattn_sample(qhat, ck, cv, *, nseq):
    m = qhat.shape[0]
    rows = nseq * XQ_ROWS
    kv_spec = pl.BlockSpec((nseq,) + ck.shape[1:], lambda i: (i, 0, 0))
    row_spec = pl.BlockSpec((rows, LANE), lambda i: (i, 0))
    return pl.pallas_call(
        _xattn_sample_kernel,
        out_shape=jax.ShapeDtypeStruct((m, LANE), BF16),
        grid=(m // rows,),
        in_specs=[row_spec, kv_spec, kv_spec],
        out_specs=row_spec,
        compiler_params=_params(("parallel",)),
        name="xattn_sample",
    )(qhat, ck, cv)


def _mlp_kernel(h_ref, g_ref, wu_ref, wd_ref, gf_ref, y_ref, hn_ref, acc_ref):
    f = pl.program_id(1)

    @pl.when(f == 0)
    def _():
        h = h_ref[...]
        hn_ref[...] = _rmsnorm(h, g_ref[...]).astype(BF16)
        acc_ref[...] = h

    a = jnp.maximum(_dot(hn_ref[...], wu_ref[...]), 0.0)
    acc_ref[...] += _dot((a * a).astype(BF16), wd_ref[...])

    @pl.when(f == pl.num_programs(1) - 1)
    def _():
        y_ref[...] = _rmsnorm(acc_ref[...], gf_ref[...])


def mlp_final(h, g, wu, wd, gf, *, tm, tf):
    m, d = h.shape
    ff = wu.shape[1]
    row_spec = pl.BlockSpec((tm, d), lambda i, f: (i, 0))
    vec_spec = pl.BlockSpec((1, d), lambda i, f: (0, 0))
    return pl.pallas_call(
        _mlp_kernel,
        out_shape=jax.ShapeDtypeStruct((m, d), F32),
        grid=(m // tm, ff // tf),
        in_specs=[row_spec, vec_spec,
                  pl.BlockSpec((d, tf), lambda i, f: (0, f)),
                  pl.BlockSpec((tf, d), lambda i, f: (f, 0)),
                  vec_spec],
        out_specs=row_spec,
        scratch_shapes=[pltpu.VMEM((tm, d), BF16), pltpu.VMEM((tm, d), F32)],
        compiler_params=_params(("parallel", "arbitrary")),
        name="mlp_final",
    )(h, g.reshape(1, d), wu, wd, gf.reshape(1, d))


def _cache_rows(c):
    b, m, h, dh = c.shape
    return c.reshape(b, m, h, dh // LANE, LANE).transpose(0, 1, 3, 2, 4).reshape(b, m * h * (dh // LANE), LANE)


def kernel(x_prompt, x_sample, mem_prompt, state_gla, state_pool, cache_mem_k, cache_mem_v,
           norm_mix_g, w_in, w_gk_up, b_gk, gla_norm_g, w_pool_mix, pool_scale,
           w_branch_a, w_branch_b, w_out, norm_x_g, norm_mem_g, w_xq, w_xk, w_xv, w_xo,
           norm_mlp_g, w_up, w_down, norm_final_g):
    depth = w_in.shape[0]
    assert depth == 1
    batch, seq, d = x_prompt.shape
    dec_batch, dec_seq, _ = x_sample.shape
    mp = batch * seq
    ms = dec_batch * dec_seq

    w0 = w_in[0]
    gz_lo = COL_OG
    gz_hi = gz_lo + GLA_GATE_RANK
    w_in_r = jnp.concatenate(
        [w0[:, :gz_lo], w0[:, gz_hi:], w0[:, gz_lo:gz_hi],
         jnp.zeros((d, PROJ_W - COL_GZ - GLA_GATE_RANK), F32)], axis=1).astype(BF16)
    wgk = jnp.concatenate(
        [w_gk_up[0], jnp.zeros((LANE - GLA_GATE_RANK, GLA_KEY_W), F32)], axis=0).astype(BF16)
    wmix = w_pool_mix[0].astype(BF16)
    wa = w_branch_a[0].astype(BF16)
    wb = w_branch_b[0].astype(BF16)
    wo = w_out[0].astype(BF16)
    wxq = w_xq[0].astype(BF16)
    wxk = w_xk[0].astype(BF16)
    wxv = w_xv[0].astype(BF16)
    wxo = w_xo[0].astype(BF16)
    wu = w_up[0].astype(BF16)
    wd = w_down[0].astype(BF16)

    xp = x_prompt.reshape(mp, d)
    xs = x_sample.reshape(ms, d)

    mem = mem_prompt.reshape(batch * MEM_LEN, d)
    mk_p = norm_matmul(mem, norm_mem_g[0], wxk, tm=1024, tn=1024)
    mv_p = norm_matmul(mem, norm_mem_g[0], wxv, tm=1024, tn=1024)
    proj_p = norm_matmul(xp, norm_mix_g[0], w_in_r, tm=1024, tn=PROJ_TN)
    o_p, sg_p = gla(proj_p, wgk, b_gk[0], gla_norm_g[0], None,
                    groups=batch, tt=512, rows=128, seg=128)
    h_p = mix_out(o_p, proj_p, None, xp, wmix, pool_scale[0], wa, wb, wo, tm=512, seq_len=seq)
    h_p = xattn_prompt(h_p, norm_x_g[0], wxq, mk_p.reshape(batch, MEM_LEN, d),
                       mv_p.reshape(batch, MEM_LEN, d), wxo, tm=512, seq_len=seq)
    y_p = mlp_final(h_p, norm_mlp_g[0], wu, wd, norm_final_g, tm=1024, tf=1024)
    sp_p = proj_p[:, COL_U:COL_U + d].reshape(batch, seq, d)[:, seq - POOL_BUF:]

    proj_s = norm_matmul(xs, norm_mix_g[0], w_in_r, tm=ms, tn=PROJ_TN)
    seqs_per_step = 16
    o_s, sg_s = gla(proj_s, wgk, b_gk[0], gla_norm_g[0], state_gla[0],
                    groups=dec_batch // seqs_per_step, tt=seqs_per_step * dec_seq,
                    rows=seqs_per_step * dec_seq, seg=dec_seq)
    u_s = proj_s[:, COL_U:COL_U + d].reshape(dec_batch, dec_seq, d)
    ext_tm = jnp.concatenate([state_pool[0], u_s], axis=1).transpose(1, 0, 2)
    d_s = pool_sample(ext_tm, PAST_LEN).transpose(1, 0, 2).reshape(ms, d)
    h_s = mix_out(o_s, proj_s, d_s, xs, wmix, pool_scale[0], wa, wb, wo, tm=ms, seq_len=dec_seq)
    q_s = norm_matmul(h_s, norm_x_g[0], wxq, tm=ms, tn=d)
    halves = X_HEAD_DIM // LANE
    qhat = q_s.reshape(dec_batch, dec_seq, X_HEADS, halves, LANE).transpose(0, 3, 2, 1, 4)
    a_s = xattn_sample(qhat.reshape(dec_batch * XQ_ROWS, LANE), _cache_rows(cache_mem_k[0]),
                       _cache_rows(cache_mem_v[0]), nseq=4)
    a_s = a_s.reshape(dec_batch, halves, X_HEADS, dec_seq, LANE).transpose(0, 3, 2, 1, 4).reshape(ms, d)
    h_s = matmul_residual(a_s, wxo, h_s, tm=ms)
    y_s = mlp_final(h_s, norm_mlp_g[0], wu, wd, norm_final_g, tm=ms, tf=1024)
    sp_s = jnp.concatenate([state_pool[0][:, dec_seq:], u_s], axis=1)

    return (y_p.reshape(batch, seq, d),
            y_s.reshape(dec_batch, dec_seq, d),
            mk_p.reshape(1, batch, MEM_LEN, X_HEADS, X_HEAD_DIM),
            mv_p.reshape(1, batch, MEM_LEN, X_HEADS, X_HEAD_DIM),
            sg_p[None],
            sg_s[None],
            sp_p[None],
            sp_s[None])
```

```python
import functools

import jax
import jax.numpy as jnp
from jax import lax
from jax.experimental import pallas as pl
from jax.experimental.pallas import tpu as pltpu

F32 = jnp.float32
BF16 = jnp.bfloat16

D_MODEL = 1024
GLA_HEADS = 4
GLA_DK = 128
GLA_DV = 256
GLA_KEY_W = GLA_HEADS * GLA_DK
GLA_VAL_W = GLA_HEADS * GLA_DV
GLA_GATE_RANK = 16
GLA_GATE_NORM = 16.0
POOL_WINDOWS = (2, 4, 8, 16)
POOL_G = 256
POOL_BUF = 15
MEM_LEN = 256
X_HEADS = 4
X_HEAD_DIM = 256
EPS = 1e-6
PAST_LEN = 16384

LANE = 128
HALO = 16
PAD = 8
VMEM_LIMIT = 52 * 1024 * 1024
XQ_ROWS = 32

IN_PIECES = (
    (0, 2 * GLA_KEY_W, F32),
    (1024, GLA_VAL_W, BF16),
    (2048, GLA_VAL_W, BF16),
    (3072, D_MODEL, F32),
    (4096, D_MODEL, BF16),
    (5120, D_MODEL, BF16),
    (6144, LANE, BF16),
)
PROJ_W = 6272

TM_PROJ = 512
TM_MIX = 512
TM_XATTN = 512
TM_MLP = 1024
TF_MLP = 1024
TT_GLA = 512
GLA_CHUNK = 128
SAMPLE_SEQS_PER_GLA_STEP = 16
SAMPLE_SEQS_PER_XATTN_STEP = 4
SAMPLE_SEQS_PER_POOL_STEP = 32


def _params(sem):
    return pltpu.CompilerParams(dimension_semantics=sem, vmem_limit_bytes=VMEM_LIMIT)


def _rmsnorm(x, g):
    return x * lax.rsqrt(jnp.mean(x * x, axis=-1, keepdims=True) + EPS) * g


def _dot(a, b):
    return jnp.dot(a, b, preferred_element_type=F32)


def _dot_nt(a, b):
    return lax.dot_general(a, b, (((1,), (1,)), ((), ())), preferred_element_type=F32)


def _dot_tn(a, b):
    return lax.dot_general(a, b, (((0,), (0,)), ((), ())), preferred_element_type=F32)


def _norm_matmul_kernel(x_ref, g_ref, w_ref, o_ref, xn_ref):
    @pl.when(pl.program_id(1) == 0)
    def _():
        xn_ref[...] = _rmsnorm(x_ref[...], g_ref[...]).astype(BF16)

    o_ref[...] = _dot(xn_ref[...], w_ref[...]).astype(o_ref.dtype)


def norm_matmul(x, g, w, tm, tn, out_dtype=F32):
    m, k = x.shape
    n = w.shape[1]
    return pl.pallas_call(
        _norm_matmul_kernel,
        out_shape=jax.ShapeDtypeStruct((m, n), out_dtype),
        grid=(m // tm, n // tn),
        in_specs=[
            pl.BlockSpec((tm, k), lambda i, j: (i, 0)),
            pl.BlockSpec((1, k), lambda i, j: (0, 0)),
            pl.BlockSpec((k, tn), lambda i, j: (0, j)),
        ],
        out_specs=pl.BlockSpec((tm, tn), lambda i, j: (i, j)),
        scratch_shapes=[pltpu.VMEM((tm, k), BF16)],
        compiler_params=_params(("parallel", "arbitrary")),
        name="norm_matmul",
    )(x, g.reshape(1, k), w)


def _in_proj_kernel(x_ref, g_ref, w_ref, *o_refs):
    xn = _rmsnorm(x_ref[...], g_ref[...]).astype(BF16)
    for (col, width, dtype), o_ref in zip(IN_PIECES, o_refs):
        o_ref[...] = _dot(xn, w_ref[:, col:col + width]).astype(dtype)


def in_proj(x, g, w, tm):
    m, k = x.shape
    return pl.pallas_call(
        _in_proj_kernel,
        out_shape=tuple(jax.ShapeDtypeStruct((m, width), dtype) for _, width, dtype in IN_PIECES),
        grid=(m // tm,),
        in_specs=[
            pl.BlockSpec((tm, k), lambda i: (i, 0)),
            pl.BlockSpec((1, k), lambda i: (0, 0)),
            pl.BlockSpec((k, PROJ_W), lambda i: (0, 0)),
        ],
        out_specs=tuple(pl.BlockSpec((tm, width), lambda i: (i, 0)) for _, width, _ in IN_PIECES),
        compiler_params=_params(("parallel",)),
        name="in_proj",
    )(x, g.reshape(1, k), w)


def _matmul_res_kernel(x_ref, w_ref, r_ref, o_ref):
    o_ref[...] = _dot(x_ref[...].astype(BF16), w_ref[...]) + r_ref[...]


def matmul_residual(x, w, res, tm):
    m, k = x.shape
    n = w.shape[1]
    return pl.pallas_call(
        _matmul_res_kernel,
        out_shape=jax.ShapeDtypeStruct((m, n), F32),
        grid=(m // tm,),
        in_specs=[
            pl.BlockSpec((tm, k), lambda i: (i, 0)),
            pl.BlockSpec((k, n), lambda i: (0, 0)),
            pl.BlockSpec((tm, n), lambda i: (i, 0)),
        ],
        out_specs=pl.BlockSpec((tm, n), lambda i: (i, 0)),
        compiler_params=_params(("parallel",)),
        name="matmul_residual",
    )(x, w, res)


def _gla_kernel(q_ref, k_ref, v_ref, gz_ref, og_ref, wgk_ref, bgk_ref, gn_ref, s0_ref,
                o_ref, s_ref, lahi_ref, lalo_ref, *, rows, seg, zero_init):
    tt = q_ref.shape[0]
    nchunks = tt // rows
    nseg = rows // seg
    seg_shift = seg.bit_length() - 1

    @pl.when(pl.program_id(1) == 0)
    def _():
        if zero_init:
            s_ref[...] = jnp.zeros(s_ref.shape, F32)
        else:
            s_ref[...] = s0_ref[...]

    z = _dot(gz_ref[...], wgk_ref[...]) + bgk_ref[...]
    la = (jnp.minimum(z, 0.0) - jnp.log(1.0 + jnp.exp(-jnp.abs(z)))) * (1.0 / GLA_GATE_NORM)
    la_hi = la.astype(BF16)
    lahi_ref[...] = la_hi
    lalo_ref[...] = (la - la_hi.astype(F32)).astype(BF16)

    ri = lax.broadcasted_iota(jnp.int32, (rows, rows), 0)
    ci = lax.broadcasted_iota(jnp.int32, (rows, rows), 1)
    same_seg = (ri >> seg_shift) == (ci >> seg_shift)
    causal = jnp.logical_and(same_seg, ci <= ri)
    l_cum = jnp.where(causal, 1.0, 0.0).astype(BF16)
    l_seg = jnp.where(same_seg, 1.0, 0.0).astype(BF16)
    row_seg = lax.broadcasted_iota(jnp.int32, (rows, GLA_DK), 0) >> seg_shift
    row_seg_v = lax.broadcasted_iota(jnp.int32, (rows, GLA_DV), 0) >> seg_shift
    gn = gn_ref[...]
    qscale = GLA_DK ** -0.5

    for c in range(nchunks):
        rsl = slice(c * rows, (c + 1) * rows)
        lah = lahi_ref[rsl, :]
        lal = lalo_ref[rsl, :]
        b = _dot(l_cum, lah) + _dot(l_cum, lal)
        if nseg == 1:
            b_end = b[rows - 1:rows, :]
        else:
            b_end = _dot(l_seg, lah) + _dot(l_seg, lal)
        q = q_ref[rsl, :]
        k = k_ref[rsl, :]
        qt_b = ((q * qscale) * jnp.exp(b)).astype(BF16)
        kt_b = (k * jnp.exp(-b)).astype(BF16)
        kd = k * jnp.exp(b_end - b)
        for h in range(GLA_HEADS):
            ks = slice(h * GLA_DK, (h + 1) * GLA_DK)
            vs = slice(h * GLA_DV, (h + 1) * GLA_DV)
            v_h = v_ref[rsl, vs]
            a = _dot_nt(qt_b[:, ks], kt_b[:, ks])
            a = jnp.where(causal, a, 0.0).astype(BF16)
            o_h = _dot(a, v_h)
            for j in range(nseg):
                s_old = s_ref[j, h]
                inter = _dot(qt_b[:, ks], s_old.astype(BF16))
                if nseg == 1:
                    o_h = o_h + inter
                    kd_j = kd[:, ks].astype(BF16)
                    dcol = jnp.broadcast_to(b_end[:, ks], (GLA_DK, GLA_DK)).T
                else:
                    o_h = o_h + jnp.where(row_seg_v == j, inter, 0.0)
                    kd_j = jnp.where(row_seg == j, kd[:, ks], 0.0).astype(BF16)
                    ones_j = jnp.where(row_seg == j, 1.0, 0.0).astype(BF16)
                    dcol = _dot_tn(lah[:, ks], ones_j) + _dot_tn(lal[:, ks], ones_j)
                e = jnp.exp(dcol)
                s_ref[j, h] = s_old * jnp.concatenate([e, e], axis=1) + _dot_tn(kd_j, v_h)
            on = _rmsnorm(o_h, gn)
            og = og_ref[rsl, vs].astype(F32)
            o_ref[rsl, vs] = (on * (og * jax.nn.sigmoid(og))).astype(o_ref.dtype)


def gla(qk, v, gz, og, wgk, bgk, gn, s0, *, groups, tt, rows, seg):
    m = qk.shape[0]
    steps = m // (groups * tt)
    nseg = rows // seg
    zero_init = s0 is None
    if zero_init:
        s0 = jnp.zeros((nseg, GLA_HEADS, 8, LANE), F32)
        s0_spec = pl.BlockSpec((nseg, GLA_HEADS, 8, LANE), lambda g, t: (0, 0, 0, 0))
    else:
        s0_spec = pl.BlockSpec((nseg, GLA_HEADS, GLA_DK, GLA_DV), lambda g, t: (g, 0, 0, 0))

    def row_spec(width, blk=0):
        return pl.BlockSpec((tt, width), lambda g, t: (g * steps + t, blk))

    kern = functools.partial(_gla_kernel, rows=rows, seg=seg, zero_init=zero_init)
    return pl.pallas_call(
        kern,
        out_shape=(jax.ShapeDtypeStruct((m, GLA_VAL_W), BF16),
                   jax.ShapeDtypeStruct((groups * nseg, GLA_HEADS, GLA_DK, GLA_DV), F32)),
        grid=(groups, steps),
        in_specs=[
            row_spec(GLA_KEY_W, 0),
            row_spec(GLA_KEY_W, 1),
            row_spec(GLA_VAL_W),
            row_spec(LANE),
            row_spec(GLA_VAL_W),
            pl.BlockSpec((LANE, GLA_KEY_W), lambda g, t: (0, 0)),
            pl.BlockSpec((1, GLA_KEY_W), lambda g, t: (0, 0)),
            pl.BlockSpec((1, GLA_DV), lambda g, t: (0, 0)),
            s0_spec,
        ],
        out_specs=(row_spec(GLA_VAL_W),
                   pl.BlockSpec((nseg, GLA_HEADS, GLA_DK, GLA_DV), lambda g, t: (g, 0, 0, 0))),
        scratch_shapes=[pltpu.VMEM((tt, GLA_KEY_W), BF16), pltpu.VMEM((tt, GLA_KEY_W), BF16)],
        compiler_params=_params(("parallel", "arbitrary")),
        name="gla",
    )(qk, qk, v, gz, og, wgk, bgk.reshape(1, GLA_KEY_W), gn.reshape(1, GLA_DV), s0)


def _pool_sample_kernel(buf_ref, u_ref, d_ref, new_ref, *, pos0):
    t_new = u_ref.shape[1]
    ext = [buf_ref[:, i, :] for i in range(POOL_BUF)] + [u_ref[:, t, :] for t in range(t_new)]
    for r in range(POOL_BUF):
        new_ref[:, r, :] = ext[r + t_new]
    for t in range(t_new):
        cur = POOL_BUF + t
        for g, w in enumerate(POOL_WINDOWS):
            cs = slice(g * POOL_G, (g + 1) * POOL_G)
            win = ext[cur][:, cs]
            for j in range(1, w):
                win = win + ext[cur - j][:, cs]
            cnt = float(min(pos0 + t + 1, w))
            d_ref[:, t, cs] = win * (1.0 / cnt) - ext[cur][:, cs]


def pool_sample(buf, u, pos0, nb):
    nseq, nbuf, width = buf.shape
    t_new = u.shape[1]
    buf_spec = pl.BlockSpec((nb, nbuf, width), lambda i: (i, 0, 0))
    new_spec = pl.BlockSpec((nb, t_new, width), lambda i: (i, 0, 0))
    return pl.pallas_call(
        functools.partial(_pool_sample_kernel, pos0=pos0),
        out_shape=(jax.ShapeDtypeStruct(u.shape, F32), jax.ShapeDtypeStruct(buf.shape, F32)),
        grid=(nseq // nb,),
        in_specs=[buf_spec, new_spec],
        out_specs=(new_spec, buf_spec),
        compiler_params=_params(("parallel",)),
        name="pool_sample",
    )(buf, u)


def _pool_diff_tile(u_ref, halo_ref, ext_ref, l2_ref, l4_ref, l8_ref, tiles_per_seq):
    tm = u_ref.shape[0]
    n = tm + HALO
    t_in_seq = pl.program_id(0) % tiles_per_seq
    zeros_pad = jnp.zeros((PAD, D_MODEL), F32)
    ext_ref[0:PAD, :] = zeros_pad
    halo = halo_ref[...]
    ext_ref[PAD:PAD + HALO, :] = jnp.where(t_in_seq == 0, 0.0, halo)
    ext_ref[PAD + HALO:, :] = u_ref[...]
    l2_ref[0:PAD, :] = zeros_pad
    l4_ref[0:PAD, :] = zeros_pad[:, :3 * POOL_G]
    l2_ref[PAD:PAD + n, :] = ext_ref[PAD:PAD + n, :] + ext_ref[PAD - 1:PAD - 1 + n, :]
    l4_ref[PAD:PAD + n, :] = l2_ref[PAD:PAD + n, POOL_G:] + l2_ref[PAD - 2:PAD - 2 + n, POOL_G:]
    l8_ref[PAD:PAD + n, :] = l4_ref[PAD:PAD + n, POOL_G:] + l4_ref[PAD - 4:PAD - 4 + n, POOL_G:]
    base = PAD + HALO
    wins = (
        l2_ref[base:base + tm, 0:POOL_G],
        l4_ref[base:base + tm, 0:POOL_G],
        l8_ref[base:base + tm, 0:POOL_G],
        l8_ref[base:base + tm, POOL_G:] + l8_ref[base - 8:base - 8 + tm, POOL_G:],
    )
    pos1 = (t_in_seq * tm + 1 + lax.broadcasted_iota(jnp.int32, (tm, 1), 0)).astype(F32)
    out = []
    for g, w in enumerate(POOL_WINDOWS):
        inv = 1.0 / jnp.minimum(pos1, float(w))
        out.append(wins[g] * inv - u_ref[:, g * POOL_G:(g + 1) * POOL_G])
    return out


def _mix_out_kernel(*refs, fused_pool, tiles_per_seq):
    if fused_pool:
        (o_ref, u_ref, halo_ref, ga_ref, gb_ref, x_ref, wmix_ref, ps_ref, wa_ref, wb_ref, wo_ref,
         h_ref, ext_ref, l2_ref, l4_ref, l8_ref) = refs
        diffs = _pool_diff_tile(u_ref, halo_ref, ext_ref, l2_ref, l4_ref, l8_ref, tiles_per_seq)
    else:
        (o_ref, d_ref, ga_ref, gb_ref, x_ref, wmix_ref, ps_ref, wa_ref, wb_ref, wo_ref, h_ref) = refs
        diffs = [d_ref[:, g * POOL_G:(g + 1) * POOL_G] for g in range(len(POOL_WINDOWS))]
    pooled = []
    for g in range(len(POOL_WINDOWS)):
        y = _dot(diffs[g].astype(BF16), wmix_ref[g]) * ps_ref[:, g * POOL_G:(g + 1) * POOL_G]
        pooled.append(y.astype(BF16))
    pooled = jnp.concatenate(pooled, axis=1)
    branch_b = _dot(pooled, wb_ref[...])
    branch_a = _dot(o_ref[...], wa_ref[...])
    merged = (jax.nn.sigmoid(ga_ref[...].astype(F32)) * branch_a
              + jax.nn.sigmoid(gb_ref[...].astype(F32)) * branch_b)
    h_ref[...] = x_ref[...] + _dot(merged.astype(BF16), wo_ref[...])


def mix_out(o, u_or_d, ga, gb, x, wmix, pscale, wa, wb, wo, *, tm, seq_len, fused_pool):
    m, wide = x.shape
    row_spec = pl.BlockSpec((tm, wide), lambda i: (i, 0))
    const2 = lambda shape: pl.BlockSpec(shape, lambda i: (0, 0))
    w_specs = [
        pl.BlockSpec((len(POOL_WINDOWS), POOL_G, POOL_G), lambda i: (0, 0, 0)),
        const2((1, wide)),
        const2((GLA_VAL_W, wide)),
        const2((wide, wide)),
        const2((wide, wide)),
    ]
    if fused_pool:
        halo_blk = tm // HALO
        in_specs = [row_spec, row_spec,
                    pl.BlockSpec((HALO, wide), lambda i: (jnp.maximum(i * halo_blk - 1, 0), 0)),
                    row_spec, row_spec, row_spec] + w_specs
        args = (o, u_or_d, u_or_d, ga, gb, x)
        scratch = [pltpu.VMEM((tm + PAD + HALO, wide), F32),
                   pltpu.VMEM((tm + PAD + HALO, wide), F32),
                   pltpu.VMEM((tm + PAD + HALO, 3 * POOL_G), F32),
                   pltpu.VMEM((tm + PAD + HALO, 2 * POOL_G), F32)]
        tiles_per_seq = seq_len // tm
    else:
        in_specs = [row_spec] * 5 + w_specs
        args = (o, u_or_d, ga, gb, x)
        scratch = []
        tiles_per_seq = 1
    kern = functools.partial(_mix_out_kernel, fused_pool=fused_pool, tiles_per_seq=tiles_per_seq)
    return pl.pallas_call(
        kern,
        out_shape=jax.ShapeDtypeStruct((m, wide), F32),
        grid=(m // tm,),
        in_specs=in_specs,
        out_specs=row_spec,
        scratch_shapes=scratch,
        compiler_params=_params(("parallel",)),
        name="mix_out",
    )(*args, wmix, pscale.reshape(1, wide), wa, wb, wo)


def _softmax_rows(s):
    p = jnp.exp(s - jnp.max(s, axis=-1, keepdims=True))
    return p, 1.0 / jnp.sum(p, axis=-1, keepdims=True)


def _xattn_prompt_kernel(h_ref, g_ref, wq_ref, mk_ref, mv_ref, wo_ref, o_ref):
    h = h_ref[...]
    hn = _rmsnorm(h, g_ref[...]).astype(BF16)
    q = (_dot(hn, wq_ref[...]) * (X_HEAD_DIM ** -0.5)).astype(BF16)
    outs = []
    for hd in range(X_HEADS):
        cs = slice(hd * X_HEAD_DIM, (hd + 1) * X_HEAD_DIM)
        kb = mk_ref[0, :, cs].astype(BF16)
        vb = mv_ref[0, :, cs].astype(BF16)
        p, inv = _softmax_rows(_dot_nt(q[:, cs], kb))
        outs.append((_dot(p.astype(BF16), vb) * inv).astype(BF16))
    o = jnp.concatenate(outs, axis=1)
    o_ref[...] = h + _dot(o, wo_ref[...])


def xattn_prompt(h, g, wq, mk, mv, wo, *, tm, seq_len):
    m, d = h.shape
    tiles = seq_len // tm
    row_spec = pl.BlockSpec((tm, d), lambda i: (i, 0))
    mem_spec = pl.BlockSpec((1, MEM_LEN, d), lambda i: (i // tiles, 0, 0))
    w_spec = pl.BlockSpec((d, d), lambda i: (0, 0))
    return pl.pallas_call(
        _xattn_prompt_kernel,
        out_shape=jax.ShapeDtypeStruct((m, d), F32),
        grid=(m // tm,),
        in_specs=[row_spec, pl.BlockSpec((1, d), lambda i: (0, 0)), w_spec, mem_spec, mem_spec, w_spec],
        out_specs=row_spec,
        compiler_params=_params(("parallel",)),
        name="xattn_prompt",
    )(h, g.reshape(1, d), wq, mk, mv, wo)


def _xattn_sample_kernel(q_ref, k_ref, v_ref, o_ref):
    nseq, krows, _ = k_ref.shape
    half = XQ_ROWS // 2
    lane = lax.broadcasted_iota(jnp.int32, (half, krows), 1)
    row = lax.broadcasted_iota(jnp.int32, (half, krows), 0)
    valid = (lane & 7) == (row >> 2)
    for j in range(nseq):
        rs = slice(j * XQ_ROWS, (j + 1) * XQ_ROWS)
        q = (q_ref[rs, :] * (X_HEAD_DIM ** -0.5)).astype(BF16)
        g = _dot_nt(q, k_ref[j].astype(BF16))
        s = g[:half] + pltpu.roll(g[half:], krows - 4, axis=1)
        s = jnp.where(valid, s, -1e30)
        p = jnp.exp(s - jnp.max(s, axis=-1, keepdims=True))
        inv = 1.0 / jnp.sum(p, axis=-1, keepdims=True)
        pe = jnp.concatenate([p, pltpu.roll(p, 4, axis=1)], axis=0).astype(BF16)
        o = _dot(pe, v_ref[j].astype(BF16)) * jnp.concatenate([inv, inv], axis=0)
        o_ref[rs, :] = o.astype(o_ref.dtype)


def xattn_sample(qhat, ck, cv, *, nseq):
    m = qhat.shape[0]
    rows = nseq * XQ_ROWS
    kv_spec = pl.BlockSpec((nseq,) + ck.shape[1:], lambda i: (i, 0, 0))
    row_spec = pl.BlockSpec((rows, LANE), lambda i: (i, 0))
    return pl.pallas_call(
        _xattn_sample_kernel,
        out_shape=jax.ShapeDtypeStruct((m, LANE), BF16),
        grid=(m // rows,),
        in_specs=[row_spec, kv_spec, kv_spec],
        out_specs=row_spec,
        compiler_params=_params(("parallel",)),
        name="xattn_sample",
    )(qhat, ck, cv)


def _mlp_kernel(h_ref, g_ref, wu_ref, wd_ref, gf_ref, y_ref, hn_ref, acc_ref):
    f = pl.program_id(1)

    @pl.when(f == 0)
    def _():
        h = h_ref[...]
        hn_ref[...] = _rmsnorm(h, g_ref[...]).astype(BF16)
        acc_ref[...] = h

    a = jnp.maximum(_dot(hn_ref[...], wu_ref[...]), 0.0)
    acc_ref[...] += _dot((a * a).astype(BF16), wd_ref[...])

    @pl.when(f == pl.num_programs(1) - 1)
    def _():
        y_ref[...] = _rmsnorm(acc_ref[...], gf_ref[...])


def mlp_final(h, g, wu, wd, gf, *, tm, tf):
    m, d = h.shape
    ff = wu.shape[1]
    row_spec = pl.BlockSpec((tm, d), lambda i, f: (i, 0))
    vec_spec = pl.BlockSpec((1, d), lambda i, f: (0, 0))
    return pl.pallas_call(
        _mlp_kernel,
        out_shape=jax.ShapeDtypeStruct((m, d), F32),
        grid=(m // tm, ff // tf),
        in_specs=[row_spec, vec_spec,
                  pl.BlockSpec((d, tf), lambda i, f: (0, f)),
                  pl.BlockSpec((tf, d), lambda i, f: (f, 0)),
                  vec_spec],
        out_specs=row_spec,
        scratch_shapes=[pltpu.VMEM((tm, d), BF16), pltpu.VMEM((tm, d), F32)],
        compiler_params=_params(("parallel", "arbitrary")),
        name="mlp_final",
    )(h, g.reshape(1, d), wu, wd, gf.reshape(1, d))


def _cache_rows(c):
    b, m, h, dh = c.shape
    return c.reshape(b, m, h, dh // LANE, LANE).transpose(0, 1, 3, 2, 4).reshape(b, m * h * (dh // LANE), LANE)


def kernel(x_prompt, x_sample, mem_prompt, state_gla, state_pool, cache_mem_k, cache_mem_v,
           norm_mix_g, w_in, w_gk_up, b_gk, gla_norm_g, w_pool_mix, pool_scale,
           w_branch_a, w_branch_b, w_out, norm_x_g, norm_mem_g, w_xq, w_xk, w_xv, w_xo,
           norm_mlp_g, w_up, w_down, norm_final_g):
    depth = w_in.shape[0]
    assert depth == 1
    batch, seq, d = x_prompt.shape
    dec_batch, dec_seq, _ = x_sample.shape
    mp = batch * seq
    ms = dec_batch * dec_seq

    w0 = w_in[0]
    gz_lo = 2 * GLA_KEY_W + GLA_VAL_W
    gz_hi = gz_lo + GLA_GATE_RANK
    w_in_r = jnp.concatenate(
        [w0[:, :gz_lo], w0[:, gz_hi:], w0[:, gz_lo:gz_hi],
         jnp.zeros((d, LANE - GLA_GATE_RANK), F32)], axis=1).astype(BF16)
    wgk = jnp.concatenate(
        [w_gk_up[0], jnp.zeros((LANE - GLA_GATE_RANK, GLA_KEY_W), F32)], axis=0).astype(BF16)
    wmix = w_pool_mix[0].astype(BF16)
    wa = w_branch_a[0].astype(BF16)
    wb = w_branch_b[0].astype(BF16)
    wo = w_out[0].astype(BF16)
    wxq = w_xq[0].astype(BF16)
    wxk = w_xk[0].astype(BF16)
    wxv = w_xv[0].astype(BF16)
    wxo = w_xo[0].astype(BF16)
    wu = w_up[0].astype(BF16)
    wd = w_down[0].astype(BF16)

    xp = x_prompt.reshape(mp, d)
    xs = x_sample.reshape(ms, d)

    mem = mem_prompt.reshape(batch * MEM_LEN, d)
    mk_p = norm_matmul(mem, norm_mem_g[0], wxk, tm=1024, tn=1024)
    mv_p = norm_matmul(mem, norm_mem_g[0], wxv, tm=1024, tn=1024)
    qk_p, v_p, og_p, u_p, ga_p, gb_p, gz_p = in_proj(xp, norm_mix_g[0], w_in_r, TM_PROJ)
    o_p, sg_p = gla(qk_p, v_p, gz_p, og_p, wgk, b_gk[0], gla_norm_g[0], None,
                    groups=batch, tt=TT_GLA, rows=GLA_CHUNK, seg=GLA_CHUNK)
    h_p = mix_out(o_p, u_p, ga_p, gb_p, xp, wmix, pool_scale[0], wa, wb, wo,
                  tm=TM_MIX, seq_len=seq, fused_pool=True)
    h_p = xattn_prompt(h_p, norm_x_g[0], wxq, mk_p.reshape(batch, MEM_LEN, d),
                       mv_p.reshape(batch, MEM_LEN, d), wxo, tm=TM_XATTN, seq_len=seq)
    y_p = mlp_final(h_p, norm_mlp_g[0], wu, wd, norm_final_g, tm=TM_MLP, tf=TF_MLP)
    sp_p = u_p.reshape(batch, seq, d)[:, seq - POOL_BUF:]

    qk_s, v_s, og_s, u_s, ga_s, gb_s, gz_s = in_proj(xs, norm_mix_g[0], w_in_r, ms)
    gla_rows = SAMPLE_SEQS_PER_GLA_STEP * dec_seq
    o_s, sg_s = gla(qk_s, v_s, gz_s, og_s, wgk, b_gk[0], gla_norm_g[0], state_gla[0],
                    groups=ms // gla_rows, tt=gla_rows, rows=gla_rows, seg=dec_seq)
    d_s, sp_s = pool_sample(state_pool[0], u_s.reshape(dec_batch, dec_seq, d), PAST_LEN,
                            SAMPLE_SEQS_PER_POOL_STEP)
    h_s = mix_out(o_s, d_s.reshape(ms, d), ga_s, gb_s, xs, wmix, pool_scale[0], wa, wb, wo,
                  tm=ms, seq_len=dec_seq, fused_pool=False)
    q_s = norm_matmul(h_s, norm_x_g[0], wxq, tm=ms, tn=d)
    halves = X_HEAD_DIM // LANE
    qhat = q_s.reshape(dec_batch, dec_seq, X_HEADS, halves, LANE).transpose(0, 3, 2, 1, 4)
    a_s = xattn_sample(qhat.reshape(dec_batch * XQ_ROWS, LANE), _cache_rows(cache_mem_k[0]),
                       _cache_rows(cache_mem_v[0]), nseq=SAMPLE_SEQS_PER_XATTN_STEP)
    a_s = a_s.reshape(dec_batch, halves, X_HEADS, dec_seq, LANE).transpose(0, 3, 2, 1, 4).reshape(ms, d)
    h_s = matmul_residual(a_s, wxo, h_s, tm=ms)
    y_s = mlp_final(h_s, norm_mlp_g[0], wu, wd, norm_final_g, tm=ms, tf=TF_MLP)

    return (y_p.reshape(batch, seq, d),
            y_s.reshape(dec_batch, dec_seq, d),
            mk_p.reshape(1, batch, MEM_LEN, X_HEADS, X_HEAD_DIM),
            mv_p.reshape(1, batch, MEM_LEN, X_HEADS, X_HEAD_DIM),
            sg_p[None],
            sg_s[None],
            sp_p[None],
            sp_s[None])
```

```python
import functools

import jax
import jax.numpy as jnp
from jax import lax
from jax.experimental import pallas as pl
from jax.experimental.pallas import tpu as pltpu

F32 = jnp.float32
BF16 = jnp.bfloat16

D_MODEL = 1024
GLA_HEADS = 4
GLA_DK = 128
GLA_DV = 256
GLA_KEY_W = GLA_HEADS * GLA_DK
GLA_VAL_W = GLA_HEADS * GLA_DV
GLA_GATE_RANK = 16
GLA_GATE_NORM = 16.0
POOL_WINDOWS = (2, 4, 8, 16)
POOL_G = 256
POOL_BUF = 15
MEM_LEN = 256
X_HEADS = 4
X_HEAD_DIM = 256
EPS = 1e-6
PAST_LEN = 16384

LANE = 128
HALO = 16
PAD = 8
VMEM_LIMIT = 52 * 1024 * 1024
XQ_ROWS = 32

IN_PIECES = (
    (0, 2 * GLA_KEY_W, F32),
    (1024, GLA_VAL_W, BF16),
    (2048, GLA_VAL_W, BF16),
    (3072, D_MODEL, F32),
    (4096, D_MODEL, BF16),
    (5120, D_MODEL, BF16),
    (6144, LANE, BF16),
)
PROJ_W = 6272

TM_PROJ = 512
TM_MIX = 512
TM_XATTN = 512
TM_MLP = 1024
TF_MLP = 1024
TT_GLA = 512
GLA_CHUNK = 128
SAMPLE_SEQS_PER_GLA_STEP = 16
SAMPLE_SEQS_PER_POOL_STEP = 32


def _params(sem):
    return pltpu.CompilerParams(dimension_semantics=sem, vmem_limit_bytes=VMEM_LIMIT)


def _rmsnorm(x, g):
    return x * lax.rsqrt(jnp.mean(x * x, axis=-1, keepdims=True) + EPS) * g


def _dot(a, b):
    return jnp.dot(a, b, preferred_element_type=F32)


def _dot_nt(a, b):
    return lax.dot_general(a, b, (((1,), (1,)), ((), ())), preferred_element_type=F32)


def _dot_tn(a, b):
    return lax.dot_general(a, b, (((0,), (0,)), ((), ())), preferred_element_type=F32)


def _norm_matmul_kernel(x_ref, g_ref, w_ref, o_ref, xn_ref):
    @pl.when(pl.program_id(1) == 0)
    def _():
        xn_ref[...] = _rmsnorm(x_ref[...], g_ref[...]).astype(BF16)

    o_ref[...] = _dot(xn_ref[...], w_ref[...]).astype(o_ref.dtype)


def norm_matmul(x, g, w, tm, tn, out_dtype=F32):
    m, k = x.shape
    n = w.shape[1]
    return pl.pallas_call(
        _norm_matmul_kernel,
        out_shape=jax.ShapeDtypeStruct((m, n), out_dtype),
        grid=(m // tm, n // tn),
        in_specs=[
            pl.BlockSpec((tm, k), lambda i, j: (i, 0)),
            pl.BlockSpec((1, k), lambda i, j: (0, 0)),
            pl.BlockSpec((k, tn), lambda i, j: (0, j)),
        ],
        out_specs=pl.BlockSpec((tm, tn), lambda i, j: (i, j)),
        scratch_shapes=[pltpu.VMEM((tm, k), BF16)],
        compiler_params=_params(("parallel", "arbitrary")),
        name="norm_matmul",
    )(x, g.reshape(1, k), w)


def _in_proj_kernel(x_ref, g_ref, w_ref, *o_refs):
    xn = _rmsnorm(x_ref[...], g_ref[...]).astype(BF16)
    for (col, width, dtype), o_ref in zip(IN_PIECES, o_refs):
        o_ref[...] = _dot_nt(xn, w_ref[col:col + width, :]).astype(dtype)


def in_proj(x, g, w, tm):
    m, k = x.shape
    return pl.pallas_call(
        _in_proj_kernel,
        out_shape=tuple(jax.ShapeDtypeStruct((m, width), dtype) for _, width, dtype in IN_PIECES),
        grid=(m // tm,),
        in_specs=[
            pl.BlockSpec((tm, k), lambda i: (i, 0)),
            pl.BlockSpec((1, k), lambda i: (0, 0)),
            pl.BlockSpec((PROJ_W, k), lambda i: (0, 0)),
        ],
        out_specs=tuple(pl.BlockSpec((tm, width), lambda i: (i, 0)) for _, width, _ in IN_PIECES),
        compiler_params=_params(("parallel",)),
        name="in_proj",
    )(x, g.reshape(1, k), w)


def _matmul_res_kernel(x_ref, w_ref, r_ref, o_ref):
    o_ref[...] = _dot(x_ref[...].astype(BF16), w_ref[...]) + r_ref[...]


def matmul_residual(x, w, res, tm):
    m, k = x.shape
    n = w.shape[1]
    return pl.pallas_call(
        _matmul_res_kernel,
        out_shape=jax.ShapeDtypeStruct((m, n), F32),
        grid=(m // tm,),
        in_specs=[
            pl.BlockSpec((tm, k), lambda i: (i, 0)),
            pl.BlockSpec((k, n), lambda i: (0, 0)),
            pl.BlockSpec((tm, n), lambda i: (i, 0)),
        ],
        out_specs=pl.BlockSpec((tm, n), lambda i: (i, 0)),
        compiler_params=_params(("parallel",)),
        name="matmul_residual",
    )(x, w, res)


def _gla_kernel(q_ref, k_ref, v_ref, gz_ref, og_ref, wgk_ref, bgk_ref, gn_ref, s0_ref,
                o_ref, s_ref, lahi_ref, lalo_ref, *, rows, seg, zero_init):
    tt = q_ref.shape[0]
    nchunks = tt // rows
    nseg = rows // seg
    seg_shift = seg.bit_length() - 1

    @pl.when(pl.program_id(1) == 0)
    def _():
        if zero_init:
            s_ref[...] = jnp.zeros(s_ref.shape, F32)
        else:
            s_ref[...] = s0_ref[...]

    z = _dot(gz_ref[...], wgk_ref[...]) + bgk_ref[...]
    la = (jnp.minimum(z, 0.0) - jnp.log(1.0 + jnp.exp(-jnp.abs(z)))) * (1.0 / GLA_GATE_NORM)
    la_hi = la.astype(BF16)
    lahi_ref[...] = la_hi
    lalo_ref[...] = (la - la_hi.astype(F32)).astype(BF16)

    ri = lax.broadcasted_iota(jnp.int32, (rows, rows), 0)
    ci = lax.broadcasted_iota(jnp.int32, (rows, rows), 1)
    same_seg = (ri >> seg_shift) == (ci >> seg_shift)
    causal = jnp.logical_and(same_seg, ci <= ri)
    l_cum = jnp.where(causal, 1.0, 0.0).astype(BF16)
    l_seg = jnp.where(same_seg, 1.0, 0.0).astype(BF16)
    row_seg = lax.broadcasted_iota(jnp.int32, (rows, GLA_DK), 0) >> seg_shift
    row_seg_v = lax.broadcasted_iota(jnp.int32, (rows, GLA_DV), 0) >> seg_shift
    gn = gn_ref[...]
    qscale = GLA_DK ** -0.5

    for c in range(nchunks):
        rsl = slice(c * rows, (c + 1) * rows)
        lah = lahi_ref[rsl, :]
        lal = lalo_ref[rsl, :]
        b = _dot(l_cum, lah) + _dot(l_cum, lal)
        if nseg == 1:
            b_end = b[rows - 1:rows, :]
        else:
            b_end = _dot(l_seg, lah) + _dot(l_seg, lal)
        q = q_ref[rsl, :]
        k = k_ref[rsl, :]
        qt_b = ((q * qscale) * jnp.exp(b)).astype(BF16)
        kt_b = (k * jnp.exp(-b)).astype(BF16)
        kd = k * jnp.exp(b_end - b)
        for h in range(GLA_HEADS):
            ks = slice(h * GLA_DK, (h + 1) * GLA_DK)
            vs = slice(h * GLA_DV, (h + 1) * GLA_DV)
            v_h = v_ref[rsl, vs]
            a = _dot_nt(qt_b[:, ks], kt_b[:, ks])
            a = jnp.where(causal, a, 0.0).astype(BF16)
            o_h = _dot(a, v_h)
            for j in range(nseg):
                s_old = s_ref[j, h]
                inter = _dot(qt_b[:, ks], s_old.astype(BF16))
                if nseg == 1:
                    o_h = o_h + inter
                    kd_j = kd[:, ks].astype(BF16)
                    dcol = jnp.broadcast_to(b_end[:, ks], (GLA_DK, GLA_DK)).T
                else:
                    o_h = o_h + jnp.where(row_seg_v == j, inter, 0.0)
                    kd_j = jnp.where(row_seg == j, kd[:, ks], 0.0).astype(BF16)
                    ones_j = jnp.where(row_seg == j, 1.0, 0.0).astype(BF16)
                    dcol = _dot_tn(lah[:, ks], ones_j) + _dot_tn(lal[:, ks], ones_j)
                e = jnp.exp(dcol)
                s_ref[j, h] = s_old * jnp.concatenate([e, e], axis=1) + _dot_tn(kd_j, v_h)
            on = _rmsnorm(o_h, gn)
            og = og_ref[rsl, vs].astype(F32)
            o_ref[rsl, vs] = (on * (og * jax.nn.sigmoid(og))).astype(o_ref.dtype)


def gla(qk, v, gz, og, wgk, bgk, gn, s0, *, groups, tt, rows, seg):
    m = qk.shape[0]
    steps = m // (groups * tt)
    nseg = rows // seg
    zero_init = s0 is None
    if zero_init:
        s0 = jnp.zeros((nseg, GLA_HEADS, 8, LANE), F32)
        s0_spec = pl.BlockSpec((nseg, GLA_HEADS, 8, LANE), lambda g, t: (0, 0, 0, 0))
    else:
        s0_spec = pl.BlockSpec((nseg, GLA_HEADS, GLA_DK, GLA_DV), lambda g, t: (g, 0, 0, 0))

    def row_spec(width, blk=0):
        return pl.BlockSpec((tt, width), lambda g, t: (g * steps + t, blk))

    kern = functools.partial(_gla_kernel, rows=rows, seg=seg, zero_init=zero_init)
    return pl.pallas_call(
        kern,
        out_shape=(jax.ShapeDtypeStruct((m, GLA_VAL_W), BF16),
                   jax.ShapeDtypeStruct((groups * nseg, GLA_HEADS, GLA_DK, GLA_DV), F32)),
        grid=(groups, steps),
        in_specs=[
            row_spec(GLA_KEY_W, 0),
            row_spec(GLA_KEY_W, 1),
            row_spec(GLA_VAL_W),
            row_spec(LANE),
            row_spec(GLA_VAL_W),
            pl.BlockSpec((LANE, GLA_KEY_W), lambda g, t: (0, 0)),
            pl.BlockSpec((1, GLA_KEY_W), lambda g, t: (0, 0)),
            pl.BlockSpec((1, GLA_DV), lambda g, t: (0, 0)),
            s0_spec,
        ],
        out_specs=(row_spec(GLA_VAL_W),
                   pl.BlockSpec((nseg, GLA_HEADS, GLA_DK, GLA_DV), lambda g, t: (g, 0, 0, 0))),
        scratch_shapes=[pltpu.VMEM((tt, GLA_KEY_W), BF16), pltpu.VMEM((tt, GLA_KEY_W), BF16)],
        compiler_params=_params(("parallel", "arbitrary")),
        name="gla",
    )(qk, qk, v, gz, og, wgk, bgk.reshape(1, GLA_KEY_W), gn.reshape(1, GLA_DV), s0)


def _pool_sample_kernel(buf_ref, u_ref, d_ref, new_ref, *, pos0):
    t_new = u_ref.shape[0]

    def ext(i, cs=slice(None)):
        return buf_ref[i, :, cs] if i < POOL_BUF else u_ref[i - POOL_BUF, :, cs]

    for r in range(POOL_BUF):
        new_ref[r] = ext(r + t_new)
    for t in range(t_new):
        cur = POOL_BUF + t
        for g, w in enumerate(POOL_WINDOWS):
            cs = slice(g * POOL_G, (g + 1) * POOL_G)
            win = ext(cur, cs)
            for j in range(1, w):
                win = win + ext(cur - j, cs)
            cnt = float(min(pos0 + t + 1, w))
            d_ref[t, :, cs] = win * (1.0 / cnt) - ext(cur, cs)


def pool_sample(buf_tm, u_tm, pos0, nb):
    nbuf, nseq, width = buf_tm.shape
    t_new = u_tm.shape[0]
    buf_spec = pl.BlockSpec((nbuf, nb, width), lambda i: (0, i, 0))
    new_spec = pl.BlockSpec((t_new, nb, width), lambda i: (0, i, 0))
    return pl.pallas_call(
        functools.partial(_pool_sample_kernel, pos0=pos0),
        out_shape=(jax.ShapeDtypeStruct(u_tm.shape, F32), jax.ShapeDtypeStruct(buf_tm.shape, F32)),
        grid=(nseq // nb,),
        in_specs=[buf_spec, new_spec],
        out_specs=(new_spec, buf_spec),
        compiler_params=_params(("parallel",)),
        name="pool_sample",
    )(buf_tm, u_tm)


def _pool_diff_tile(u_ref, halo_ref, ext_ref, l2_ref, l4_ref, l8_ref, tiles_per_seq):
    tm = u_ref.shape[0]
    n = tm + HALO
    t_in_seq = pl.program_id(0) % tiles_per_seq
    zeros_pad = jnp.zeros((PAD, D_MODEL), F32)
    ext_ref[0:PAD, :] = zeros_pad
    halo = halo_ref[...]
    ext_ref[PAD:PAD + HALO, :] = jnp.where(t_in_seq == 0, 0.0, halo)
    ext_ref[PAD + HALO:, :] = u_ref[...]
    l2_ref[0:PAD, :] = zeros_pad
    l4_ref[0:PAD, :] = zeros_pad[:, :3 * POOL_G]
    l2_ref[PAD:PAD + n, :] = ext_ref[PAD:PAD + n, :] + ext_ref[PAD - 1:PAD - 1 + n, :]
    l4_ref[PAD:PAD + n, :] = l2_ref[PAD:PAD + n, POOL_G:] + l2_ref[PAD - 2:PAD - 2 + n, POOL_G:]
    l8_ref[PAD:PAD + n, :] = l4_ref[PAD:PAD + n, POOL_G:] + l4_ref[PAD - 4:PAD - 4 + n, POOL_G:]
    base = PAD + HALO
    wins = (
        l2_ref[base:base + tm, 0:POOL_G],
        l4_ref[base:base + tm, 0:POOL_G],
        l8_ref[base:base + tm, 0:POOL_G],
        l8_ref[base:base + tm, POOL_G:] + l8_ref[base - 8:base - 8 + tm, POOL_G:],
    )
    pos1 = (t_in_seq * tm + 1 + lax.broadcasted_iota(jnp.int32, (tm, 1), 0)).astype(F32)
    out = []
    for g, w in enumerate(POOL_WINDOWS):
        inv = 1.0 / jnp.minimum(pos1, float(w))
        out.append(wins[g] * inv - u_ref[:, g * POOL_G:(g + 1) * POOL_G])
    return out


def _mix_out_kernel(*refs, fused_pool, tiles_per_seq):
    if fused_pool:
        (o_ref, u_ref, halo_ref, ga_ref, gb_ref, x_ref, wmix_ref, ps_ref, wa_ref, wb_ref, wo_ref,
         h_ref, ext_ref, l2_ref, l4_ref, l8_ref) = refs
        diffs = _pool_diff_tile(u_ref, halo_ref, ext_ref, l2_ref, l4_ref, l8_ref, tiles_per_seq)
    else:
        (o_ref, d_ref, ga_ref, gb_ref, x_ref, wmix_ref, ps_ref, wa_ref, wb_ref, wo_ref, h_ref) = refs
        diffs = [d_ref[:, g * POOL_G:(g + 1) * POOL_G] for g in range(len(POOL_WINDOWS))]
    pooled = []
    for g in range(len(POOL_WINDOWS)):
        y = _dot(diffs[g].astype(BF16), wmix_ref[g]) * ps_ref[:, g * POOL_G:(g + 1) * POOL_G]
        pooled.append(y.astype(BF16))
    pooled = jnp.concatenate(pooled, axis=1)
    branch_b = _dot(pooled, wb_ref[...])
    branch_a = _dot(o_ref[...], wa_ref[...])
    merged = (jax.nn.sigmoid(ga_ref[...].astype(F32)) * branch_a
              + jax.nn.sigmoid(gb_ref[...].astype(F32)) * branch_b)
    h_ref[...] = x_ref[...] + _dot(merged.astype(BF16), wo_ref[...])


def mix_out(o, u_or_d, ga, gb, x, wmix, pscale, wa, wb, wo, *, tm, seq_len, fused_pool):
    m, wide = x.shape
    row_spec = pl.BlockSpec((tm, wide), lambda i: (i, 0))
    const2 = lambda shape: pl.BlockSpec(shape, lambda i: (0, 0))
    w_specs = [
        pl.BlockSpec((len(POOL_WINDOWS), POOL_G, POOL_G), lambda i: (0, 0, 0)),
        const2((1, wide)),
        const2((GLA_VAL_W, wide)),
        const2((wide, wide)),
        const2((wide, wide)),
    ]
    if fused_pool:
        halo_blk = tm // HALO
        in_specs = [row_spec, row_spec,
                    pl.BlockSpec((HALO, wide), lambda i: (jnp.maximum(i * halo_blk - 1, 0), 0)),
                    row_spec, row_spec, row_spec] + w_specs
        args = (o, u_or_d, u_or_d, ga, gb, x)
        scratch = [pltpu.VMEM((tm + PAD + HALO, wide), F32),
                   pltpu.VMEM((tm + PAD + HALO, wide), F32),
                   pltpu.VMEM((tm + PAD + HALO, 3 * POOL_G), F32),
                   pltpu.VMEM((tm + PAD + HALO, 2 * POOL_G), F32)]
        tiles_per_seq = seq_len // tm
    else:
        in_specs = [row_spec] * 5 + w_specs
        args = (o, u_or_d, ga, gb, x)
        scratch = []
        tiles_per_seq = 1
    kern = functools.partial(_mix_out_kernel, fused_pool=fused_pool, tiles_per_seq=tiles_per_seq)
    return pl.pallas_call(
        kern,
        out_shape=jax.ShapeDtypeStruct((m, wide), F32),
        grid=(m // tm,),
        in_specs=in_specs,
        out_specs=row_spec,
        scratch_shapes=scratch,
        compiler_params=_params(("parallel",)),
        name="mix_out",
    )(*args, wmix, pscale.reshape(1, wide), wa, wb, wo)


def _softmax_rows(s):
    p = jnp.exp(s - jnp.max(s, axis=-1, keepdims=True))
    return p, 1.0 / jnp.sum(p, axis=-1, keepdims=True)


def _xattn_prompt_kernel(h_ref, g_ref, wq_ref, mk_ref, mv_ref, wo_ref, o_ref):
    h = h_ref[...]
    hn = _rmsnorm(h, g_ref[...]).astype(BF16)
    q = (_dot(hn, wq_ref[...]) * (X_HEAD_DIM ** -0.5)).astype(BF16)
    outs = []
    for hd in range(X_HEADS):
        cs = slice(hd * X_HEAD_DIM, (hd + 1) * X_HEAD_DIM)
        kb = mk_ref[0, :, cs].astype(BF16)
        vb = mv_ref[0, :, cs].astype(BF16)
        p, inv = _softmax_rows(_dot_nt(q[:, cs], kb))
        outs.append((_dot(p.astype(BF16), vb) * inv).astype(BF16))
    o = jnp.concatenate(outs, axis=1)
    o_ref[...] = h + _dot(o, wo_ref[...])


def xattn_prompt(h, g, wq, mk, mv, wo, *, tm, seq_len):
    m, d = h.shape
    tiles = seq_len // tm
    row_spec = pl.BlockSpec((tm, d), lambda i: (i, 0))
    mem_spec = pl.BlockSpec((1, MEM_LEN, d), lambda i: (i // tiles, 0, 0))
    w_spec = pl.BlockSpec((d, d), lambda i: (0, 0))
    return pl.pallas_call(
        _xattn_prompt_kernel,
        out_shape=jax.ShapeDtypeStruct((m, d), F32),
        grid=(m // tm,),
        in_specs=[row_spec, pl.BlockSpec((1, d), lambda i: (0, 0)), w_spec, mem_spec, mem_spec, w_spec],
        out_specs=row_spec,
        compiler_params=_params(("parallel",)),
        name="xattn_prompt",
    )(h, g.reshape(1, d), wq, mk, mv, wo)


def _sample_attention_probs(q_ref, k_ref):
    nseq, krows, _ = k_ref.shape
    half = XQ_ROWS // 2
    lane = lax.broadcasted_iota(jnp.int32, (half, krows), 1)
    row = lax.broadcasted_iota(jnp.int32, (half, krows), 0)
    valid = (lane & 7) == (row >> 2)
    out = []
    for j in range(nseq):
        q = (q_ref[j * XQ_ROWS:(j + 1) * XQ_ROWS, :] * (X_HEAD_DIM ** -0.5)).astype(BF16)
        g = _dot_nt(q, k_ref[j].astype(BF16))
        s = g[:half] + pltpu.roll(g[half:], krows - 4, axis=1)
        s = jnp.where(valid, s, -1e30)
        p = jnp.exp(s - jnp.max(s, axis=-1, keepdims=True))
        inv = 1.0 / jnp.sum(p, axis=-1, keepdims=True)
        pe = jnp.concatenate([p, pltpu.roll(p, 4, axis=1)], axis=0).astype(BF16)
        out.append((pe, jnp.concatenate([inv, inv], axis=0)))
    return out


def _sample_attention_values(probs, v_ref, o_ref):
    for j, (pe, inv) in enumerate(probs):
        o = _dot(pe, v_ref[j].astype(BF16)) * inv
        o_ref[j * XQ_ROWS:(j + 1) * XQ_ROWS, :] = o.astype(o_ref.dtype)


def _mlp_kernel(*refs, with_attn):
    if with_attn:
        (h_ref, g_ref, wu_ref, wd_ref, gf_ref, q_ref, k_ref, v_ref,
         y_ref, a_ref, hn_ref, acc_ref) = refs
    else:
        h_ref, g_ref, wu_ref, wd_ref, gf_ref, y_ref, hn_ref, acc_ref = refs
    f = pl.program_id(1)

    @pl.when(f == 0)
    def _():
        h = h_ref[...]
        hn_ref[...] = _rmsnorm(h, g_ref[...]).astype(BF16)
        acc_ref[...] = h

    if with_attn:
        probs = _sample_attention_probs(q_ref, k_ref)
    a = jnp.maximum(_dot(hn_ref[...], wu_ref[...]), 0.0)
    if with_attn:
        _sample_attention_values(probs, v_ref, a_ref)
    acc_ref[...] += _dot((a * a).astype(BF16), wd_ref[...])

    @pl.when(f == pl.num_programs(1) - 1)
    def _():
        y_ref[...] = _rmsnorm(acc_ref[...], gf_ref[...])


def mlp_final(h, g, wu, wd, gf, *, tm, tf, attn=None):
    m, d = h.shape
    ff = wu.shape[1]
    nf = ff // tf
    row_spec = pl.BlockSpec((tm, d), lambda i, f: (i, 0))
    vec_spec = pl.BlockSpec((1, d), lambda i, f: (0, 0))
    in_specs = [row_spec, vec_spec,
                pl.BlockSpec((d, tf), lambda i, f: (0, f)),
                pl.BlockSpec((tf, d), lambda i, f: (f, 0)),
                vec_spec]
    args = [h, g.reshape(1, d), wu, wd, gf.reshape(1, d)]
    out_shape = jax.ShapeDtypeStruct((m, d), F32)
    out_specs = row_spec
    if attn is not None:
        qhat, ck, cv = attn
        steps = (m // tm) * nf
        nseq = ck.shape[0] // steps
        assert nseq * steps == ck.shape[0]
        q_spec = pl.BlockSpec((nseq * XQ_ROWS, LANE), lambda i, f: (i * nf + f, 0))
        kv_spec = pl.BlockSpec((nseq,) + ck.shape[1:], lambda i, f: (i * nf + f, 0, 0))
        in_specs += [q_spec, kv_spec, kv_spec]
        args += [qhat, ck, cv]
        out_shape = (out_shape, jax.ShapeDtypeStruct(qhat.shape, BF16))
        out_specs = (row_spec, q_spec)
    return pl.pallas_call(
        functools.partial(_mlp_kernel, with_attn=attn is not None),
        out_shape=out_shape,
        grid=(m // tm, nf),
        in_specs=in_specs,
        out_specs=out_specs,
        scratch_shapes=[pltpu.VMEM((tm, d), BF16), pltpu.VMEM((tm, d), F32)],
        compiler_params=_params(("parallel", "arbitrary")),
        name="mlp_final",
    )(*args)


def _prep_weights_kernel(win_ref, *refs):
    n = (len(refs) - 1) // 2
    in_refs, win_out, out_refs = refs[:n], refs[n], refs[n + 1:]
    gz_lo = 2 * GLA_KEY_W + GLA_VAL_W
    gz_hi = gz_lo + GLA_GATE_RANK
    gz_out = IN_PIECES[-1][0]
    win_out[:gz_lo, :] = win_ref[:gz_lo, :].astype(BF16)
    win_out[gz_lo:gz_out, :] = win_ref[gz_hi:, :].astype(BF16)
    win_out[gz_out:gz_out + GLA_GATE_RANK, :] = win_ref[gz_lo:gz_hi, :].astype(BF16)
    win_out[gz_out + GLA_GATE_RANK:, :] = jnp.zeros(
        (PROJ_W - gz_out - GLA_GATE_RANK, win_out.shape[1]), BF16)
    for i_ref, o_ref in zip(in_refs, out_refs):
        o_ref[...] = i_ref[...].astype(BF16)


def prep_weights(w_in_t, others, nblk=8):
    def spec(shape):
        return pl.BlockSpec((None, shape[1] // nblk, shape[2]), lambda i: (0, i, 0))

    def out_spec(rows, cols):
        return pl.BlockSpec((rows // nblk, cols), lambda i: (i, 0))

    width, d = w_in_t.shape
    out_shapes = [jax.ShapeDtypeStruct((PROJ_W, d), BF16)]
    out_specs = [pl.BlockSpec((PROJ_W, d // nblk), lambda i: (0, i))]
    for w in others:
        out_shapes.append(jax.ShapeDtypeStruct(w.shape[1:], BF16))
        out_specs.append(out_spec(*w.shape[1:]))
    return pl.pallas_call(
        _prep_weights_kernel,
        out_shape=tuple(out_shapes),
        grid=(nblk,),
        in_specs=[pl.BlockSpec((width, d // nblk), lambda i: (0, i))] + [spec(w.shape) for w in others],
        out_specs=tuple(out_specs),
        compiler_params=_params(("parallel",)),
        name="prep_weights",
    )(w_in_t, *others)


def _cache_rows(c):
    b, m, h, dh = c.shape
    return c.reshape(b, m, h, dh // LANE, LANE).transpose(0, 1, 3, 2, 4).reshape(b, m * h * (dh // LANE), LANE)


def kernel(x_prompt, x_sample, mem_prompt, state_gla, state_pool, cache_mem_k, cache_mem_v,
           norm_mix_g, w_in, w_gk_up, b_gk, gla_norm_g, w_pool_mix, pool_scale,
           w_branch_a, w_branch_b, w_out, norm_x_g, norm_mem_g, w_xq, w_xk, w_xv, w_xo,
           norm_mlp_g, w_up, w_down, norm_final_g):
    depth = w_in.shape[0]
    assert depth == 1
    batch, seq, d = x_prompt.shape
    dec_batch, dec_seq, _ = x_sample.shape
    mp = batch * seq
    ms = dec_batch * dec_seq

    (w_in_r, wa, wb, wo, wxq, wxk, wxv, wxo, wu, wd) = prep_weights(
        w_in[0].T, (w_branch_a, w_branch_b, w_out, w_xq, w_xk, w_xv, w_xo, w_up, w_down))
    wgk = jnp.concatenate(
        [w_gk_up[0], jnp.zeros((LANE - GLA_GATE_RANK, GLA_KEY_W), F32)], axis=0).astype(BF16)
    wmix = w_pool_mix[0].astype(BF16)

    xp = x_prompt.reshape(mp, d)
    xs = x_sample.reshape(ms, d)

    qk_s, v_s, og_s, u_s, ga_s, gb_s, gz_s = in_proj(xs, norm_mix_g[0], w_in_r, ms)
    gla_rows = SAMPLE_SEQS_PER_GLA_STEP * dec_seq
    o_s, sg_s = gla(qk_s, v_s, gz_s, og_s, wgk, b_gk[0], gla_norm_g[0], state_gla[0],
                    groups=ms // gla_rows, tt=gla_rows, rows=gla_rows, seg=dec_seq)
    d_tm, sp_tm = pool_sample(state_pool[0].transpose(1, 0, 2),
                              u_s.reshape(dec_batch, dec_seq, d).transpose(1, 0, 2),
                              PAST_LEN, SAMPLE_SEQS_PER_POOL_STEP)
    h_s = mix_out(o_s, d_tm.transpose(1, 0, 2).reshape(ms, d), ga_s, gb_s, xs, wmix, pool_scale[0],
                  wa, wb, wo, tm=ms, seq_len=dec_seq, fused_pool=False)
    q_s = norm_matmul(h_s, norm_x_g[0], wxq, tm=ms, tn=d)
    halves = X_HEAD_DIM // LANE
    qhat = q_s.reshape(dec_batch, dec_seq, X_HEADS, halves, LANE).transpose(0, 3, 2, 1, 4)
    qhat = qhat.reshape(dec_batch * XQ_ROWS, LANE)

    mem = mem_prompt.reshape(batch * MEM_LEN, d)
    mk_p = norm_matmul(mem, norm_mem_g[0], wxk, tm=1024, tn=1024)
    mv_p = norm_matmul(mem, norm_mem_g[0], wxv, tm=1024, tn=1024)
    qk_p, v_p, og_p, u_p, ga_p, gb_p, gz_p = in_proj(xp, norm_mix_g[0], w_in_r, TM_PROJ)
    o_p, sg_p = gla(qk_p, v_p, gz_p, og_p, wgk, b_gk[0], gla_norm_g[0], None,
                    groups=batch, tt=TT_GLA, rows=GLA_CHUNK, seg=GLA_CHUNK)
    h_p = mix_out(o_p, u_p, ga_p, gb_p, xp, wmix, pool_scale[0], wa, wb, wo,
                  tm=TM_MIX, seq_len=seq, fused_pool=True)
    h_p = xattn_prompt(h_p, norm_x_g[0], wxq, mk_p.reshape(batch, MEM_LEN, d),
                       mv_p.reshape(batch, MEM_LEN, d), wxo, tm=TM_XATTN, seq_len=seq)
    y_p, a_s = mlp_final(h_p, norm_mlp_g[0], wu, wd, norm_final_g, tm=TM_MLP, tf=TF_MLP,
                         attn=(qhat, _cache_rows(cache_mem_k[0]), _cache_rows(cache_mem_v[0])))
    sp_p = u_p.reshape(batch, seq, d)[:, seq - POOL_BUF:]

    a_s = a_s.reshape(dec_batch, halves, X_HEADS, dec_seq, LANE).transpose(0, 3, 2, 1, 4).reshape(ms, d)
    h_s = matmul_residual(a_s, wxo, h_s, tm=ms)
    y_s = mlp_final(h_s, norm_mlp_g[0], wu, wd, norm_final_g, tm=ms, tf=TF_MLP)
    sp_s = sp_tm.transpose(1, 0, 2)

    return (y_p.reshape(batch, seq, d),
            y_s.reshape(dec_batch, dec_seq, d),
            mk_p.reshape(1, batch, MEM_LEN, X_HEADS, X_HEAD_DIM),
            mv_p.reshape(1, batch, MEM_LEN, X_HEADS, X_HEAD_DIM),
            sg_p[None],
            sg_s[None],
            sp_p[None],
            sp_s[None])
```

```python
import functools

import jax
import jax.numpy as jnp
from jax import lax
from jax.experimental import pallas as pl
from jax.experimental.pallas import tpu as pltpu

F32 = jnp.float32
BF16 = jnp.bfloat16

D_MODEL = 1024
GLA_HEADS = 4
GLA_DK = 128
GLA_DV = 256
GLA_KEY_W = GLA_HEADS * GLA_DK
GLA_VAL_W = GLA_HEADS * GLA_DV
GLA_GATE_RANK = 16
GLA_GATE_NORM = 16.0
POOL_WINDOWS = (2, 4, 8, 16)
POOL_G = 256
POOL_BUF = 15
MEM_LEN = 256
X_HEADS = 4
X_HEAD_DIM = 256
EPS = 1e-6
PAST_LEN = 16384

LANE = 128
HALO = 16
PAD = 8
VMEM_LIMIT = 52 * 1024 * 1024
XQ_ROWS = 32

IN_COLS = {
    "qk": (0, 2 * GLA_KEY_W),
    "v": (1024, GLA_VAL_W),
    "og": (2048, GLA_VAL_W),
    "u": (3072, D_MODEL),
    "ga": (4096, D_MODEL),
    "gb": (5120, D_MODEL),
    "gz": (6144, LANE),
}
PROJ_W = 6272

TM_PROJ = 512
TM_MIX = 512
TM_XATTN = 512
TM_MLP = 1024
TF_MLP = 1024
TT_GLA = 512
GLA_CHUNK = 128
SAMPLE_SEQS_PER_GLA_STEP = 16
SAMPLE_SEQS_PER_POOL_STEP = 32


def _params(sem):
    return pltpu.CompilerParams(dimension_semantics=sem, vmem_limit_bytes=VMEM_LIMIT)


def _rmsnorm(x, g):
    return x * lax.rsqrt(jnp.mean(x * x, axis=-1, keepdims=True) + EPS) * g


def _dot(a, b):
    return jnp.dot(a, b, preferred_element_type=F32)


def _dot_nt(a, b):
    return lax.dot_general(a, b, (((1,), (1,)), ((), ())), preferred_element_type=F32)


def _dot_tn(a, b):
    return lax.dot_general(a, b, (((0,), (0,)), ((), ())), preferred_element_type=F32)


def _norm_matmul_kernel(x_ref, g_ref, w_ref, o_ref, xn_ref):
    @pl.when(pl.program_id(1) == 0)
    def _():
        xn_ref[...] = _rmsnorm(x_ref[...], g_ref[...]).astype(BF16)

    o_ref[...] = _dot(xn_ref[...], w_ref[...]).astype(o_ref.dtype)


def norm_matmul(x, g, w, tm, tn, out_dtype=F32):
    m, k = x.shape
    n = w.shape[1]
    return pl.pallas_call(
        _norm_matmul_kernel,
        out_shape=jax.ShapeDtypeStruct((m, n), out_dtype),
        grid=(m // tm, n // tn),
        in_specs=[
            pl.BlockSpec((tm, k), lambda i, j: (i, 0)),
            pl.BlockSpec((1, k), lambda i, j: (0, 0)),
            pl.BlockSpec((k, tn), lambda i, j: (0, j)),
        ],
        out_specs=pl.BlockSpec((tm, tn), lambda i, j: (i, j)),
        scratch_shapes=[pltpu.VMEM((tm, k), BF16)],
        compiler_params=_params(("parallel", "arbitrary")),
        name="norm_matmul",
    )(x, g.reshape(1, k), w)


def _log_decay_split(z):
    la = (jnp.minimum(z, 0.0) - jnp.log(1.0 + jnp.exp(-jnp.abs(z)))) * (1.0 / GLA_GATE_NORM)
    la_hi = la.astype(BF16)
    return la_hi, (la - la_hi.astype(F32)).astype(BF16)


def _in_proj_kernel(x_ref, g_ref, w_ref, wgk_ref, bgk_ref,
                    qk_ref, v_ref, gate_ref, u_ref, siga_ref, sigb_ref, lahi_ref, lalo_ref):
    xn = _rmsnorm(x_ref[...], g_ref[...]).astype(BF16)

    def piece(name):
        col, width = IN_COLS[name]
        return _dot_nt(xn, w_ref[col:col + width, :])

    z = _dot(piece("gz").astype(BF16), wgk_ref[...]) + bgk_ref[...]
    lahi_ref[...], lalo_ref[...] = _log_decay_split(z)
    og = piece("og")
    gate_ref[...] = (og * jax.nn.sigmoid(og)).astype(BF16)
    siga_ref[...] = jax.nn.sigmoid(piece("ga")).astype(BF16)
    sigb_ref[...] = jax.nn.sigmoid(piece("gb")).astype(BF16)
    v_ref[...] = piece("v").astype(BF16)
    qk_ref[...] = piece("qk")
    u_ref[...] = piece("u")


def in_proj(x, g, w, wgk, bgk, tm):
    m, k = x.shape
    outs = ((2 * GLA_KEY_W, F32), (GLA_VAL_W, BF16), (GLA_VAL_W, BF16), (D_MODEL, F32),
            (D_MODEL, BF16), (D_MODEL, BF16), (GLA_KEY_W, BF16), (GLA_KEY_W, BF16))
    const = lambda shape: pl.BlockSpec(shape, lambda i: (0, 0))
    return pl.pallas_call(
        _in_proj_kernel,
        out_shape=tuple(jax.ShapeDtypeStruct((m, width), dtype) for width, dtype in outs),
        grid=(m // tm,),
        in_specs=[pl.BlockSpec((tm, k), lambda i: (i, 0)), const((1, k)), const((PROJ_W, k)),
                  const((LANE, GLA_KEY_W)), const((1, GLA_KEY_W))],
        out_specs=tuple(pl.BlockSpec((tm, width), lambda i: (i, 0)) for width, _ in outs),
        compiler_params=_params(("parallel",)),
        name="in_proj",
    )(x, g.reshape(1, k), w, wgk, bgk.reshape(1, GLA_KEY_W))


def _matmul_res_kernel(x_ref, w_ref, r_ref, o_ref):
    o_ref[...] = _dot(x_ref[...].astype(BF16), w_ref[...]) + r_ref[...]


def matmul_residual(x, w, res, tm):
    m, k = x.shape
    n = w.shape[1]
    return pl.pallas_call(
        _matmul_res_kernel,
        out_shape=jax.ShapeDtypeStruct((m, n), F32),
        grid=(m // tm,),
        in_specs=[
            pl.BlockSpec((tm, k), lambda i: (i, 0)),
            pl.BlockSpec((k, n), lambda i: (0, 0)),
            pl.BlockSpec((tm, n), lambda i: (i, 0)),
        ],
        out_specs=pl.BlockSpec((tm, n), lambda i: (i, 0)),
        compiler_params=_params(("parallel",)),
        name="matmul_residual",
    )(x, w, res)


def _gla_kernel(q_ref, k_ref, v_ref, lahi_ref, lalo_ref, gate_ref, gn_ref, s0_ref,
                o_ref, s_ref, *, rows, seg, zero_init):
    tt = q_ref.shape[0]
    nchunks = tt // rows
    nseg = rows // seg
    seg_shift = seg.bit_length() - 1

    @pl.when(pl.program_id(1) == 0)
    def _():
        if zero_init:
            s_ref[...] = jnp.zeros(s_ref.shape, F32)
        else:
            s_ref[...] = s0_ref[...]

    ri = lax.broadcasted_iota(jnp.int32, (rows, rows), 0)
    ci = lax.broadcasted_iota(jnp.int32, (rows, rows), 1)
    same_seg = (ri >> seg_shift) == (ci >> seg_shift)
    causal = jnp.logical_and(same_seg, ci <= ri)
    l_cum = jnp.where(causal, 1.0, 0.0).astype(BF16)
    l_seg = jnp.where(same_seg, 1.0, 0.0).astype(BF16)
    row_seg = lax.broadcasted_iota(jnp.int32, (rows, GLA_DK), 0) >> seg_shift
    row_seg_v = lax.broadcasted_iota(jnp.int32, (rows, GLA_DV), 0) >> seg_shift
    gn = gn_ref[...]
    qscale = GLA_DK ** -0.5

    def cumulative(c):
        rsl = slice(c * rows, (c + 1) * rows)
        lah = lahi_ref[rsl, :]
        lal = lalo_ref[rsl, :]
        b = _dot(l_cum, lah) + _dot(l_cum, lal)
        if nseg == 1:
            b_end = b[rows - 1:rows, :]
        else:
            b_end = _dot(l_seg, lah) + _dot(l_seg, lal)
        return lah, lal, b, b_end

    ahead = cumulative(0)
    for c in range(nchunks):
        rsl = slice(c * rows, (c + 1) * rows)
        lah, lal, b, b_end = ahead
        if c + 1 < nchunks:
            ahead = cumulative(c + 1)
        q = q_ref[rsl, :]
        k = k_ref[rsl, :]
        qt_b = ((q * qscale) * jnp.exp(b)).astype(BF16)
        kt_b = (k * jnp.exp(-b)).astype(BF16)
        kd = k * jnp.exp(b_end - b)
        heads = []
        for h in range(GLA_HEADS):
            ks = slice(h * GLA_DK, (h + 1) * GLA_DK)
            v_h = v_ref[rsl, h * GLA_DV:(h + 1) * GLA_DV]
            scores = _dot_nt(qt_b[:, ks], kt_b[:, ks])
            inter = None
            new_states = []
            for j in range(nseg):
                s_old = s_ref[j, h]
                inter_j = _dot(qt_b[:, ks], s_old.astype(BF16))
                if nseg == 1:
                    inter = inter_j
                    kd_j = kd[:, ks].astype(BF16)
                    dcol = jnp.broadcast_to(b_end[:, ks], (GLA_DK, GLA_DK)).T
                else:
                    inter_j = jnp.where(row_seg_v == j, inter_j, 0.0)
                    inter = inter_j if inter is None else inter + inter_j
                    kd_j = jnp.where(row_seg == j, kd[:, ks], 0.0).astype(BF16)
                    ones_j = jnp.where(row_seg == j, 1.0, 0.0).astype(BF16)
                    dcol = _dot_tn(lah[:, ks], ones_j) + _dot_tn(lal[:, ks], ones_j)
                new_states.append((s_old, dcol, _dot_tn(kd_j, v_h)))
            heads.append((v_h, scores, inter, new_states))
        outs = []
        for v_h, scores, inter, _ in heads:
            a = jnp.where(causal, scores, 0.0).astype(BF16)
            outs.append(_dot(a, v_h) + inter)
        for h, (_, _, _, new_states) in enumerate(heads):
            vs = slice(h * GLA_DV, (h + 1) * GLA_DV)
            for j, (s_old, dcol, upd) in enumerate(new_states):
                e = jnp.exp(dcol)
                s_ref[j, h] = s_old * jnp.concatenate([e, e], axis=1) + upd
            on = _rmsnorm(outs[h], gn)
            o_ref[rsl, vs] = (on * gate_ref[rsl, vs].astype(F32)).astype(o_ref.dtype)


def gla(qk, v, la_hi, la_lo, gate, gn, s0, *, groups, tt, rows, seg):
    m = qk.shape[0]
    steps = m // (groups * tt)
    nseg = rows // seg
    zero_init = s0 is None
    if zero_init:
        s0 = jnp.zeros((nseg, GLA_HEADS, 8, LANE), F32)
        s0_spec = pl.BlockSpec((nseg, GLA_HEADS, 8, LANE), lambda g, t: (0, 0, 0, 0))
    else:
        s0_spec = pl.BlockSpec((nseg, GLA_HEADS, GLA_DK, GLA_DV), lambda g, t: (g, 0, 0, 0))

    def row_spec(width, blk=0):
        return pl.BlockSpec((tt, width), lambda g, t: (g * steps + t, blk))

    kern = functools.partial(_gla_kernel, rows=rows, seg=seg, zero_init=zero_init)
    return pl.pallas_call(
        kern,
        out_shape=(jax.ShapeDtypeStruct((m, GLA_VAL_W), BF16),
                   jax.ShapeDtypeStruct((groups * nseg, GLA_HEADS, GLA_DK, GLA_DV), F32)),
        grid=(groups, steps),
        in_specs=[
            row_spec(GLA_KEY_W, 0),
            row_spec(GLA_KEY_W, 1),
            row_spec(GLA_VAL_W),
            row_spec(GLA_KEY_W),
            row_spec(GLA_KEY_W),
            row_spec(GLA_VAL_W),
            pl.BlockSpec((1, GLA_DV), lambda g, t: (0, 0)),
            s0_spec,
        ],
        out_specs=(row_spec(GLA_VAL_W),
                   pl.BlockSpec((nseg, GLA_HEADS, GLA_DK, GLA_DV), lambda g, t: (g, 0, 0, 0))),
        compiler_params=_params(("parallel", "arbitrary")),
        name="gla",
    )(qk, qk, v, la_hi, la_lo, gate, gn.reshape(1, GLA_DV), s0)


def _pool_sample_kernel(buf_ref, u_ref, d_ref, new_ref, *, pos0):
    t_new = u_ref.shape[0]

    def ext(i, cs=slice(None)):
        return buf_ref[i, :, cs] if i < POOL_BUF else u_ref[i - POOL_BUF, :, cs]

    for r in range(POOL_BUF):
        new_ref[r] = ext(r + t_new)
    for t in range(t_new):
        cur = POOL_BUF + t
        for g, w in enumerate(POOL_WINDOWS):
            cs = slice(g * POOL_G, (g + 1) * POOL_G)
            win = ext(cur, cs)
            for j in range(1, w):
                win = win + ext(cur - j, cs)
            cnt = float(min(pos0 + t + 1, w))
            d_ref[t, :, cs] = win * (1.0 / cnt) - ext(cur, cs)


def pool_sample(buf_tm, u_tm, pos0, nb):
    nbuf, nseq, width = buf_tm.shape
    t_new = u_tm.shape[0]
    buf_spec = pl.BlockSpec((nbuf, nb, width), lambda i: (0, i, 0))
    new_spec = pl.BlockSpec((t_new, nb, width), lambda i: (0, i, 0))
    return pl.pallas_call(
        functools.partial(_pool_sample_kernel, pos0=pos0),
        out_shape=(jax.ShapeDtypeStruct(u_tm.shape, F32), jax.ShapeDtypeStruct(buf_tm.shape, F32)),
        grid=(nseq // nb,),
        in_specs=[buf_spec, new_spec],
        out_specs=(new_spec, buf_spec),
        compiler_params=_params(("parallel",)),
        name="pool_sample",
    )(buf_tm, u_tm)


def _pool_diff_tile(u_ref, halo_ref, ext_ref, l2_ref, l4_ref, l8_ref, tiles_per_seq):
    tm = u_ref.shape[0]
    n = tm + HALO
    t_in_seq = pl.program_id(0) % tiles_per_seq
    zeros_pad = jnp.zeros((PAD, D_MODEL), F32)
    ext_ref[0:PAD, :] = zeros_pad
    halo = halo_ref[...]
    ext_ref[PAD:PAD + HALO, :] = jnp.where(t_in_seq == 0, 0.0, halo)
    ext_ref[PAD + HALO:, :] = u_ref[...]
    l2_ref[0:PAD, :] = zeros_pad
    l4_ref[0:PAD, :] = zeros_pad[:, :3 * POOL_G]
    l2_ref[PAD:PAD + n, :] = ext_ref[PAD:PAD + n, :] + ext_ref[PAD - 1:PAD - 1 + n, :]
    l4_ref[PAD:PAD + n, :] = l2_ref[PAD:PAD + n, POOL_G:] + l2_ref[PAD - 2:PAD - 2 + n, POOL_G:]
    l8_ref[PAD:PAD + n, :] = l4_ref[PAD:PAD + n, POOL_G:] + l4_ref[PAD - 4:PAD - 4 + n, POOL_G:]
    base = PAD + HALO
    wins = (
        l2_ref[base:base + tm, 0:POOL_G],
        l4_ref[base:base + tm, 0:POOL_G],
        l8_ref[base:base + tm, 0:POOL_G],
        l8_ref[base:base + tm, POOL_G:] + l8_ref[base - 8:base - 8 + tm, POOL_G:],
    )
    pos1 = (t_in_seq * tm + 1 + lax.broadcasted_iota(jnp.int32, (tm, 1), 0)).astype(F32)
    out = []
    for g, w in enumerate(POOL_WINDOWS):
        inv = 1.0 / jnp.minimum(pos1, float(w))
        out.append(wins[g] * inv - u_ref[:, g * POOL_G:(g + 1) * POOL_G])
    return out


def _mix_out_kernel(*refs, fused_pool, tiles_per_seq):
    if fused_pool:
        (o_ref, u_ref, halo_ref, siga_ref, sigb_ref, x_ref, wmix_ref, ps_ref, wa_ref, wb_ref,
         wo_ref, h_ref, ext_ref, l2_ref, l4_ref, l8_ref) = refs
        branch_a = _dot(o_ref[...], wa_ref[...])
        diffs = _pool_diff_tile(u_ref, halo_ref, ext_ref, l2_ref, l4_ref, l8_ref, tiles_per_seq)
    else:
        (o_ref, d_ref, siga_ref, sigb_ref, x_ref, wmix_ref, ps_ref, wa_ref, wb_ref, wo_ref,
         h_ref) = refs
        branch_a = _dot(o_ref[...], wa_ref[...])
        diffs = [d_ref[:, g * POOL_G:(g + 1) * POOL_G] for g in range(len(POOL_WINDOWS))]
    pooled = []
    for g in range(len(POOL_WINDOWS)):
        y = _dot(diffs[g].astype(BF16), wmix_ref[g]) * ps_ref[:, g * POOL_G:(g + 1) * POOL_G]
        pooled.append(y.astype(BF16))
    pooled = jnp.concatenate(pooled, axis=1)
    merged = (siga_ref[...].astype(F32) * branch_a
              + sigb_ref[...].astype(F32) * _dot(pooled, wb_ref[...]))
    h_ref[...] = x_ref[...] + _dot(merged.astype(BF16), wo_ref[...])


def mix_out(o, u_or_d, sig_a, sig_b, x, wmix, pscale, wa, wb, wo, *, tm, seq_len, fused_pool):
    m, wide = x.shape
    row_spec = pl.BlockSpec((tm, wide), lambda i: (i, 0))
    const2 = lambda shape: pl.BlockSpec(shape, lambda i: (0, 0))
    w_specs = [
        pl.BlockSpec((len(POOL_WINDOWS), POOL_G, POOL_G), lambda i: (0, 0, 0)),
        const2((1, wide)),
        const2((GLA_VAL_W, wide)),
        const2((wide, wide)),
        const2((wide, wide)),
    ]
    if fused_pool:
        halo_blk = tm // HALO
        in_specs = [row_spec, row_spec,
                    pl.BlockSpec((HALO, wide), lambda i: (jnp.maximum(i * halo_blk - 1, 0), 0)),
                    row_spec, row_spec, row_spec] + w_specs
        args = (o, u_or_d, u_or_d, sig_a, sig_b, x)
        scratch = [pltpu.VMEM((tm + PAD + HALO, wide), F32),
                   pltpu.VMEM((tm + PAD + HALO, wide), F32),
                   pltpu.VMEM((tm + PAD + HALO, 3 * POOL_G), F32),
                   pltpu.VMEM((tm + PAD + HALO, 2 * POOL_G), F32)]
        tiles_per_seq = seq_len // tm
    else:
        in_specs = [row_spec] * 5 + w_specs
        args = (o, u_or_d, sig_a, sig_b, x)
        scratch = []
        tiles_per_seq = 1
    kern = functools.partial(_mix_out_kernel, fused_pool=fused_pool, tiles_per_seq=tiles_per_seq)
    return pl.pallas_call(
        kern,
        out_shape=jax.ShapeDtypeStruct((m, wide), F32),
        grid=(m // tm,),
        in_specs=in_specs,
        out_specs=row_spec,
        scratch_shapes=scratch,
        compiler_params=_params(("parallel",)),
        name="mix_out",
    )(*args, wmix, pscale.reshape(1, wide), wa, wb, wo)


def _softmax_rows(s):
    p = jnp.exp(s - jnp.max(s, axis=-1, keepdims=True))
    return p, 1.0 / jnp.sum(p, axis=-1, keepdims=True)


def _xattn_prompt_kernel(h_ref, g_ref, wq_ref, mk_ref, mv_ref, wo_ref, o_ref):
    h = h_ref[...]
    hn = _rmsnorm(h, g_ref[...]).astype(BF16)
    q = (_dot(hn, wq_ref[...]) * (X_HEAD_DIM ** -0.5)).astype(BF16)

    def scores(hd):
        cs = slice(hd * X_HEAD_DIM, (hd + 1) * X_HEAD_DIM)
        return _dot_nt(q[:, cs], mk_ref[0, :, cs].astype(BF16))

    outs = []
    s_next = scores(0)
    for hd in range(X_HEADS):
        s_cur = s_next
        if hd + 1 < X_HEADS:
            s_next = scores(hd + 1)
        p, inv = _softmax_rows(s_cur)
        vb = mv_ref[0, :, hd * X_HEAD_DIM:(hd + 1) * X_HEAD_DIM].astype(BF16)
        outs.append((_dot(p.astype(BF16), vb) * inv).astype(BF16))
    o = jnp.concatenate(outs, axis=1)
    o_ref[...] = h + _dot(o, wo_ref[...])


def xattn_prompt(h, g, wq, mk, mv, wo, *, tm, seq_len):
    m, d = h.shape
    tiles = seq_len // tm
    row_spec = pl.BlockSpec((tm, d), lambda i: (i, 0))
    mem_spec = pl.BlockSpec((1, MEM_LEN, d), lambda i: (i // tiles, 0, 0))
    w_spec = pl.BlockSpec((d, d), lambda i: (0, 0))
    return pl.pallas_call(
        _xattn_prompt_kernel,
        out_shape=jax.ShapeDtypeStruct((m, d), F32),
        grid=(m // tm,),
        in_specs=[row_spec, pl.BlockSpec((1, d), lambda i: (0, 0)), w_spec, mem_spec, mem_spec, w_spec],
        out_specs=row_spec,
        compiler_params=_params(("parallel",)),
        name="xattn_prompt",
    )(h, g.reshape(1, d), wq, mk, mv, wo)


def _sample_attention_probs(q_ref, k_ref):
    nseq, krows, _ = k_ref.shape
    half = XQ_ROWS // 2
    lane = lax.broadcasted_iota(jnp.int32, (half, krows), 1)
    row = lax.broadcasted_iota(jnp.int32, (half, krows), 0)
    valid = (lane & 7) == (row >> 2)
    out = []
    for j in range(nseq):
        q = (q_ref[j * XQ_ROWS:(j + 1) * XQ_ROWS, :] * (X_HEAD_DIM ** -0.5)).astype(BF16)
        g = _dot_nt(q, k_ref[j].astype(BF16))
        s = g[:half] + pltpu.roll(g[half:], krows - 4, axis=1)
        s = jnp.where(valid, s, -1e30)
        p = jnp.exp(s - jnp.max(s, axis=-1, keepdims=True))
        inv = 1.0 / jnp.sum(p, axis=-1, keepdims=True)
        pe = jnp.concatenate([p, pltpu.roll(p, 4, axis=1)], axis=0).astype(BF16)
        out.append((pe, jnp.concatenate([inv, inv], axis=0)))
    return out


def _sample_attention_values(probs, v_ref, o_ref):
    for j, (pe, inv) in enumerate(probs):
        o = _dot(pe, v_ref[j].astype(BF16)) * inv
        o_ref[j * XQ_ROWS:(j + 1) * XQ_ROWS, :] = o.astype(o_ref.dtype)


def _mlp_kernel(*refs, with_attn):
    if with_attn:
        (h_ref, g_ref, wu_ref, wd_ref, gf_ref, q_ref, k_ref, v_ref,
         y_ref, a_ref, hn_ref, acc_ref) = refs
    else:
        h_ref, g_ref, wu_ref, wd_ref, gf_ref, y_ref, hn_ref, acc_ref = refs
    f = pl.program_id(1)

    @pl.when(f == 0)
    def _():
        h = h_ref[...]
        hn_ref[...] = _rmsnorm(h, g_ref[...]).astype(BF16)
        acc_ref[...] = h

    if with_attn:
        probs = _sample_attention_probs(q_ref, k_ref)
    a = jnp.maximum(_dot(hn_ref[...], wu_ref[...]), 0.0)
    if with_attn:
        _sample_attention_values(probs, v_ref, a_ref)
    acc_ref[...] += _dot((a * a).astype(BF16), wd_ref[...])

    @pl.when(f == pl.num_programs(1) - 1)
    def _():
        y_ref[...] = _rmsnorm(acc_ref[...], gf_ref[...])


def mlp_final(h, g, wu, wd, gf, *, tm, tf, attn=None):
    m, d = h.shape
    ff = wu.shape[1]
    nf = ff // tf
    row_spec = pl.BlockSpec((tm, d), lambda i, f: (i, 0))
    vec_spec = pl.BlockSpec((1, d), lambda i, f: (0, 0))
    in_specs = [row_spec, vec_spec,
                pl.BlockSpec((d, tf), lambda i, f: (0, f)),
                pl.BlockSpec((tf, d), lambda i, f: (f, 0)),
                vec_spec]
    args = [h, g.reshape(1, d), wu, wd, gf.reshape(1, d)]
    out_shape = jax.ShapeDtypeStruct((m, d), F32)
    out_specs = row_spec
    if attn is not None:
        qhat, ck, cv = attn
        steps = (m // tm) * nf
        nseq = ck.shape[0] // steps
        assert nseq * steps == ck.shape[0]
        q_spec = pl.BlockSpec((nseq * XQ_ROWS, LANE), lambda i, f: (i * nf + f, 0))
        kv_spec = pl.BlockSpec((nseq,) + ck.shape[1:], lambda i, f: (i * nf + f, 0, 0))
        in_specs += [q_spec, kv_spec, kv_spec]
        args += [qhat, ck, cv]
        out_shape = (out_shape, jax.ShapeDtypeStruct(qhat.shape, BF16))
        out_specs = (row_spec, q_spec)
    return pl.pallas_call(
        functools.partial(_mlp_kernel, with_attn=attn is not None),
        out_shape=out_shape,
        grid=(m // tm, nf),
        in_specs=in_specs,
        out_specs=out_specs,
        scratch_shapes=[pltpu.VMEM((tm, d), BF16), pltpu.VMEM((tm, d), F32)],
        compiler_params=_params(("parallel", "arbitrary")),
        name="mlp_final",
    )(*args)


def _prep_weights_kernel(win_ref, *refs):
    n = (len(refs) - 1) // 2
    in_refs, win_out, out_refs = refs[:n], refs[n], refs[n + 1:]
    gz_lo = 2 * GLA_KEY_W + GLA_VAL_W
    gz_hi = gz_lo + GLA_GATE_RANK
    gz_out = IN_COLS["gz"][0]
    win_out[:gz_lo, :] = win_ref[:gz_lo, :].astype(BF16)
    win_out[gz_lo:gz_out, :] = win_ref[gz_hi:, :].astype(BF16)
    win_out[gz_out:gz_out + GLA_GATE_RANK, :] = win_ref[gz_lo:gz_hi, :].astype(BF16)
    win_out[gz_out + GLA_GATE_RANK:, :] = jnp.zeros(
        (PROJ_W - gz_out - GLA_GATE_RANK, win_out.shape[1]), BF16)
    for i_ref, o_ref in zip(in_refs, out_refs):
        o_ref[...] = i_ref[...].astype(BF16)


def prep_weights(w_in_t, others, nblk=8):
    def spec(shape):
        return pl.BlockSpec((None, shape[1] // nblk, shape[2]), lambda i: (0, i, 0))

    def out_spec(rows, cols):
        return pl.BlockSpec((rows // nblk, cols), lambda i: (i, 0))

    width, d = w_in_t.shape
    out_shapes = [jax.ShapeDtypeStruct((PROJ_W, d), BF16)]
    out_specs = [pl.BlockSpec((PROJ_W, d // nblk), lambda i: (0, i))]
    for w in others:
        out_shapes.append(jax.ShapeDtypeStruct(w.shape[1:], BF16))
        out_specs.append(out_spec(*w.shape[1:]))
    return pl.pallas_call(
        _prep_weights_kernel,
        out_shape=tuple(out_shapes),
        grid=(nblk,),
        in_specs=[pl.BlockSpec((width, d // nblk), lambda i: (0, i))] + [spec(w.shape) for w in others],
        out_specs=tuple(out_specs),
        compiler_params=_params(("parallel",)),
        name="prep_weights",
    )(w_in_t, *others)


def _cache_rows(c):
    b, m, h, dh = c.shape
    return c.reshape(b, m, h, dh // LANE, LANE).transpose(0, 1, 3, 2, 4).reshape(b, m * h * (dh // LANE), LANE)


def kernel(x_prompt, x_sample, mem_prompt, state_gla, state_pool, cache_mem_k, cache_mem_v,
           norm_mix_g, w_in, w_gk_up, b_gk, gla_norm_g, w_pool_mix, pool_scale,
           w_branch_a, w_branch_b, w_out, norm_x_g, norm_mem_g, w_xq, w_xk, w_xv, w_xo,
           norm_mlp_g, w_up, w_down, norm_final_g):
    depth = w_in.shape[0]
    assert depth == 1
    batch, seq, d = x_prompt.shape
    dec_batch, dec_seq, _ = x_sample.shape
    mp = batch * seq
    ms = dec_batch * dec_seq

    (w_in_r, wa, wb, wo, wxq, wxk, wxv, wxo, wu, wd) = prep_weights(
        w_in[0].T, (w_branch_a, w_branch_b, w_out, w_xq, w_xk, w_xv, w_xo, w_up, w_down))
    wgk = jnp.concatenate(
        [w_gk_up[0], jnp.zeros((LANE - GLA_GATE_RANK, GLA_KEY_W), F32)], axis=0).astype(BF16)
    wmix = w_pool_mix[0].astype(BF16)

    xp = x_prompt.reshape(mp, d)
    xs = x_sample.reshape(ms, d)

    qk_s, v_s, gate_s, u_s, sa_s, sb_s, lah_s, lal_s = in_proj(xs, norm_mix_g[0], w_in_r, wgk, b_gk[0], ms)
    gla_rows = SAMPLE_SEQS_PER_GLA_STEP * dec_seq
    o_s, sg_s = gla(qk_s, v_s, lah_s, lal_s, gate_s, gla_norm_g[0], state_gla[0],
                    groups=ms // gla_rows, tt=gla_rows, rows=gla_rows, seg=dec_seq)
    d_tm, sp_tm = pool_sample(state_pool[0].transpose(1, 0, 2),
                              u_s.reshape(dec_batch, dec_seq, d).transpose(1, 0, 2),
                              PAST_LEN, SAMPLE_SEQS_PER_POOL_STEP)
    h_s = mix_out(o_s, d_tm.transpose(1, 0, 2).reshape(ms, d), sa_s, sb_s, xs, wmix, pool_scale[0],
                  wa, wb, wo, tm=ms, seq_len=dec_seq, fused_pool=False)
    q_s = norm_matmul(h_s, norm_x_g[0], wxq, tm=ms, tn=d)
    halves = X_HEAD_DIM // LANE
    qhat = q_s.reshape(dec_batch, dec_seq, X_HEADS, halves, LANE).transpose(0, 3, 2, 1, 4)
    qhat = qhat.reshape(dec_batch * XQ_ROWS, LANE)

    mem = mem_prompt.reshape(batch * MEM_LEN, d)
    mk_p = norm_matmul(mem, norm_mem_g[0], wxk, tm=1024, tn=1024)
    mv_p = norm_matmul(mem, norm_mem_g[0], wxv, tm=1024, tn=1024)
    qk_p, v_p, gate_p, u_p, sa_p, sb_p, lah_p, lal_p = in_proj(xp, norm_mix_g[0], w_in_r, wgk, b_gk[0],
                                                               TM_PROJ)
    o_p, sg_p = gla(qk_p, v_p, lah_p, lal_p, gate_p, gla_norm_g[0], None,
                    groups=batch, tt=TT_GLA, rows=GLA_CHUNK, seg=GLA_CHUNK)
    h_p = mix_out(o_p, u_p, sa_p, sb_p, xp, wmix, pool_scale[0], wa, wb, wo,
                  tm=TM_MIX, seq_len=seq, fused_pool=True)
    h_p = xattn_prompt(h_p, norm_x_g[0], wxq, mk_p.reshape(batch, MEM_LEN, d),
                       mv_p.reshape(batch, MEM_LEN, d), wxo, tm=TM_XATTN, seq_len=seq)
    y_p, a_s = mlp_final(h_p, norm_mlp_g[0], wu, wd, norm_final_g, tm=TM_MLP, tf=TF_MLP,
                         attn=(qhat, _cache_rows(cache_mem_k[0]), _cache_rows(cache_mem_v[0])))
    sp_p = u_p.reshape(batch, seq, d)[:, seq - POOL_BUF:]

    a_s = a_s.reshape(dec_batch, halves, X_HEADS, dec_seq, LANE).transpose(0, 3, 2, 1, 4).reshape(ms, d)
    h_s = matmul_residual(a_s, wxo, h_s, tm=ms)
    y_s = mlp_final(h_s, norm_mlp_g[0], wu, wd, norm_final_g, tm=ms, tf=TF_MLP)
    sp_s = sp_tm.transpose(1, 0, 2)

    return (y_p.reshape(batch, seq, d),
            y_s.reshape(dec_batch, dec_seq, d),
            mk_p.reshape(1, batch, MEM_LEN, X_HEADS, X_HEAD_DIM),
            mv_p.reshape(1, batch, MEM_LEN, X_HEADS, X_HEAD_DIM),
            sg_p[None],
            sg_s[None],
            sp_p[None],
            sp_s[None])
```

```python
import functools

import jax
import jax.numpy as jnp
from jax import lax
from jax.experimental import pallas as pl
from jax.experimental.pallas import tpu as pltpu

F32 = jnp.float32
BF16 = jnp.bfloat16

D_MODEL = 1024
GLA_HEADS = 4
GLA_DK = 128
GLA_DV = 256
GLA_KEY_W = GLA_HEADS * GLA_DK
GLA_VAL_W = GLA_HEADS * GLA_DV
GLA_GATE_RANK = 16
GLA_GATE_NORM = 16.0
POOL_WINDOWS = (2, 4, 8, 16)
POOL_G = 256
POOL_BUF = 15
MEM_LEN = 256
X_HEADS = 4
X_HEAD_DIM = 256
EPS = 1e-6
PAST_LEN = 16384

LANE = 128
HALO = 16
PAD = 8
VMEM_LIMIT = 52 * 1024 * 1024
XQ_ROWS = 32

IN_COLS = {
    "qk": (0, 2 * GLA_KEY_W),
    "v": (1024, GLA_VAL_W),
    "og": (2048, GLA_VAL_W),
    "u": (3072, D_MODEL),
    "ga": (4096, D_MODEL),
    "gb": (5120, D_MODEL),
    "gz": (6144, LANE),
}
PROJ_W = 6272

TM_PROJ = 512
TM_MIX = 512
TM_XATTN = 512
TM_MLP = 1024
TF_MLP = 1024
TT_GLA = 512
GLA_CHUNK = 128
SAMPLE_SEQS_PER_GLA_STEP = 16
SAMPLE_SEQS_PER_POOL_STEP = 32


def _params(sem):
    return pltpu.CompilerParams(dimension_semantics=sem, vmem_limit_bytes=VMEM_LIMIT)


def _rmsnorm(x, g):
    return x * lax.rsqrt(jnp.mean(x * x, axis=-1, keepdims=True) + EPS) * g


def _dot(a, b):
    return jnp.dot(a, b, preferred_element_type=F32)


def _dot_nt(a, b):
    return lax.dot_general(a, b, (((1,), (1,)), ((), ())), preferred_element_type=F32)


def _dot_tn(a, b):
    return lax.dot_general(a, b, (((0,), (0,)), ((), ())), preferred_element_type=F32)


def _norm_matmul_kernel(x_ref, g_ref, w_ref, o_ref, xn_ref):
    @pl.when(pl.program_id(1) == 0)
    def _():
        xn_ref[...] = _rmsnorm(x_ref[...], g_ref[...]).astype(BF16)

    o_ref[...] = _dot(xn_ref[...], w_ref[...]).astype(o_ref.dtype)


def norm_matmul(x, g, w, tm, tn, out_dtype=F32):
    m, k = x.shape
    n = w.shape[1]
    return pl.pallas_call(
        _norm_matmul_kernel,
        out_shape=jax.ShapeDtypeStruct((m, n), out_dtype),
        grid=(m // tm, n // tn),
        in_specs=[
            pl.BlockSpec((tm, k), lambda i, j: (i, 0)),
            pl.BlockSpec((1, k), lambda i, j: (0, 0)),
            pl.BlockSpec((k, tn), lambda i, j: (0, j)),
        ],
        out_specs=pl.BlockSpec((tm, tn), lambda i, j: (i, j)),
        scratch_shapes=[pltpu.VMEM((tm, k), BF16)],
        compiler_params=_params(("parallel", "arbitrary")),
        name="norm_matmul",
    )(x, g.reshape(1, k), w)


def _log_decay_split(z):
    la = (jnp.minimum(z, 0.0) - jnp.log(1.0 + jnp.exp(-jnp.abs(z)))) * (1.0 / GLA_GATE_NORM)
    la_hi = la.astype(BF16)
    return la_hi, (la - la_hi.astype(F32)).astype(BF16)


def _in_proj_kernel(*refs, ncast):
    (x_ref, g_ref, w_ref, wgk_ref, bgk_ref) = refs[:5]
    cast_in = refs[5:5 + ncast]
    (qk_ref, v_ref, gate_ref, u_ref, siga_ref, sigb_ref, lahi_ref, lalo_ref) = refs[5 + ncast:13 + ncast]
    cast_out = refs[13 + ncast:]
    xn = _rmsnorm(x_ref[...], g_ref[...]).astype(BF16)

    def piece(name):
        col, width = IN_COLS[name]
        return _dot_nt(xn, w_ref[col:col + width, :])

    z = _dot(piece("gz").astype(BF16), wgk_ref[...]) + bgk_ref[...]
    lahi_ref[...], lalo_ref[...] = _log_decay_split(z)
    og = piece("og")
    gate_ref[...] = (og * jax.nn.sigmoid(og)).astype(BF16)
    siga_ref[...] = jax.nn.sigmoid(piece("ga")).astype(BF16)
    sigb_ref[...] = jax.nn.sigmoid(piece("gb")).astype(BF16)
    v_ref[...] = piece("v").astype(BF16)
    qk_ref[...] = piece("qk")
    u_ref[...] = piece("u")
    for i_ref, o_ref in zip(cast_in, cast_out):
        o_ref[...] = i_ref[...].astype(BF16)


def in_proj(x, g, w, wgk, bgk, tm, cast=()):
    m, k = x.shape
    steps = m // tm
    outs = ((2 * GLA_KEY_W, F32), (GLA_VAL_W, BF16), (GLA_VAL_W, BF16), (D_MODEL, F32),
            (D_MODEL, BF16), (D_MODEL, BF16), (GLA_KEY_W, BF16), (GLA_KEY_W, BF16))
    const = lambda shape: pl.BlockSpec(shape, lambda i: (0, 0))
    out_shape = [jax.ShapeDtypeStruct((m, width), dtype) for width, dtype in outs]
    out_specs = [pl.BlockSpec((tm, width), lambda i: (i, 0)) for width, _ in outs]
    in_specs = [pl.BlockSpec((tm, k), lambda i: (i, 0)), const((1, k)), const((PROJ_W, k)),
                const((LANE, GLA_KEY_W)), const((1, GLA_KEY_W))]
    for wc in cast:
        _, rows, cols = wc.shape
        in_specs.append(pl.BlockSpec((None, rows // steps, cols), lambda i: (0, i, 0)))
        out_specs.append(pl.BlockSpec((rows // steps, cols), lambda i: (i, 0)))
        out_shape.append(jax.ShapeDtypeStruct((rows, cols), BF16))
    return pl.pallas_call(
        functools.partial(_in_proj_kernel, ncast=len(cast)),
        out_shape=tuple(out_shape),
        grid=(steps,),
        in_specs=in_specs,
        out_specs=tuple(out_specs),
        compiler_params=_params(("parallel",)),
        name="in_proj",
    )(x, g.reshape(1, k), w, wgk, bgk.reshape(1, GLA_KEY_W), *cast)


def _matmul_res_kernel(x_ref, w_ref, r_ref, o_ref):
    o_ref[...] = _dot(x_ref[...].astype(BF16), w_ref[...]) + r_ref[...]


def matmul_residual(x, w, res, tm):
    m, k = x.shape
    n = w.shape[1]
    return pl.pallas_call(
        _matmul_res_kernel,
        out_shape=jax.ShapeDtypeStruct((m, n), F32),
        grid=(m // tm,),
        in_specs=[
            pl.BlockSpec((tm, k), lambda i: (i, 0)),
            pl.BlockSpec((k, n), lambda i: (0, 0)),
            pl.BlockSpec((tm, n), lambda i: (i, 0)),
        ],
        out_specs=pl.BlockSpec((tm, n), lambda i: (i, 0)),
        compiler_params=_params(("parallel",)),
        name="matmul_residual",
    )(x, w, res)


def _gla_kernel(q_ref, k_ref, v_ref, lahi_ref, lalo_ref, gate_ref, gn_ref, s0_ref,
                o_ref, s_ref, *, rows, seg, zero_init):
    tt = q_ref.shape[0]
    nchunks = tt // rows
    nseg = rows // seg
    seg_shift = seg.bit_length() - 1

    @pl.when(pl.program_id(1) == 0)
    def _():
        if zero_init:
            s_ref[...] = jnp.zeros(s_ref.shape, F32)
        else:
            s_ref[...] = s0_ref[...]

    ri = lax.broadcasted_iota(jnp.int32, (rows, rows), 0)
    ci = lax.broadcasted_iota(jnp.int32, (rows, rows), 1)
    same_seg = (ri >> seg_shift) == (ci >> seg_shift)
    causal = jnp.logical_and(same_seg, ci <= ri)
    l_cum = jnp.where(causal, 1.0, 0.0).astype(BF16)
    l_seg = jnp.where(same_seg, 1.0, 0.0).astype(BF16)
    row_seg = lax.broadcasted_iota(jnp.int32, (rows, GLA_DK), 0) >> seg_shift
    row_seg_v = lax.broadcasted_iota(jnp.int32, (rows, GLA_DV), 0) >> seg_shift
    gn = gn_ref[...]
    qscale = GLA_DK ** -0.5

    def cumulative(c):
        rsl = slice(c * rows, (c + 1) * rows)
        lah = lahi_ref[rsl, :]
        lal = lalo_ref[rsl, :]
        b = _dot(l_cum, lah) + _dot(l_cum, lal)
        if nseg == 1:
            b_end = b[rows - 1:rows, :]
        else:
            b_end = _dot(l_seg, lah) + _dot(l_seg, lal)
        return lah, lal, b, b_end

    ahead = cumulative(0)
    for c in range(nchunks):
        rsl = slice(c * rows, (c + 1) * rows)
        lah, lal, b, b_end = ahead
        if c + 1 < nchunks:
            ahead = cumulative(c + 1)
        q = q_ref[rsl, :]
        k = k_ref[rsl, :]
        qt_b = ((q * qscale) * jnp.exp(b)).astype(BF16)
        kt_b = (k * jnp.exp(-b)).astype(BF16)
        kd = k * jnp.exp(b_end - b)
        heads = []
        for h in range(GLA_HEADS):
            ks = slice(h * GLA_DK, (h + 1) * GLA_DK)
            v_h = v_ref[rsl, h * GLA_DV:(h + 1) * GLA_DV]
            scores = _dot_nt(qt_b[:, ks], kt_b[:, ks])
            inter = None
            new_states = []
            for j in range(nseg):
                s_old = s_ref[j, h]
                inter_j = _dot(qt_b[:, ks], s_old.astype(BF16))
                if nseg == 1:
                    inter = inter_j
                    kd_j = kd[:, ks].astype(BF16)
                    dcol = jnp.broadcast_to(b_end[:, ks], (GLA_DK, GLA_DK)).T
                else:
                    inter_j = jnp.where(row_seg_v == j, inter_j, 0.0)
                    inter = inter_j if inter is None else inter + inter_j
                    kd_j = jnp.where(row_seg == j, kd[:, ks], 0.0).astype(BF16)
                    ones_j = jnp.where(row_seg == j, 1.0, 0.0).astype(BF16)
                    dcol = _dot_tn(lah[:, ks], ones_j) + _dot_tn(lal[:, ks], ones_j)
                new_states.append((s_old, dcol, _dot_tn(kd_j, v_h)))
            heads.append((v_h, scores, inter, new_states))
        outs = []
        for v_h, scores, inter, _ in heads:
            a = jnp.where(causal, scores, 0.0).astype(BF16)
            outs.append(_dot(a, v_h) + inter)
        for h, (_, _, _, new_states) in enumerate(heads):
            vs = slice(h * GLA_DV, (h + 1) * GLA_DV)
            for j, (s_old, dcol, upd) in enumerate(new_states):
                e = jnp.exp(dcol)
                s_ref[j, h] = s_old * jnp.concatenate([e, e], axis=1) + upd
            on = _rmsnorm(outs[h], gn)
            o_ref[rsl, vs] = (on * gate_ref[rsl, vs].astype(F32)).astype(o_ref.dtype)


def gla(qk, v, la_hi, la_lo, gate, gn, s0, *, groups, tt, rows, seg):
    m = qk.shape[0]
    steps = m // (groups * tt)
    nseg = rows // seg
    zero_init = s0 is None
    if zero_init:
        s0 = jnp.zeros((nseg, GLA_HEADS, 8, LANE), F32)
        s0_spec = pl.BlockSpec((nseg, GLA_HEADS, 8, LANE), lambda g, t: (0, 0, 0, 0))
    else:
        s0_spec = pl.BlockSpec((nseg, GLA_HEADS, GLA_DK, GLA_DV), lambda g, t: (g, 0, 0, 0))

    def row_spec(width, blk=0):
        return pl.BlockSpec((tt, width), lambda g, t: (g * steps + t, blk))

    kern = functools.partial(_gla_kernel, rows=rows, seg=seg, zero_init=zero_init)
    return pl.pallas_call(
        kern,
        out_shape=(jax.ShapeDtypeStruct((m, GLA_VAL_W), BF16),
                   jax.ShapeDtypeStruct((groups * nseg, GLA_HEADS, GLA_DK, GLA_DV), F32)),
        grid=(groups, steps),
        in_specs=[
            row_spec(GLA_KEY_W, 0),
            row_spec(GLA_KEY_W, 1),
            row_spec(GLA_VAL_W),
            row_spec(GLA_KEY_W),
            row_spec(GLA_KEY_W),
            row_spec(GLA_VAL_W),
            pl.BlockSpec((1, GLA_DV), lambda g, t: (0, 0)),
            s0_spec,
        ],
        out_specs=(row_spec(GLA_VAL_W),
                   pl.BlockSpec((nseg, GLA_HEADS, GLA_DK, GLA_DV), lambda g, t: (g, 0, 0, 0))),
        compiler_params=_params(("parallel", "arbitrary")),
        name="gla",
    )(qk, qk, v, la_hi, la_lo, gate, gn.reshape(1, GLA_DV), s0)


def _pool_sample_kernel(buf_ref, u_ref, d_ref, new_ref, *, pos0):
    t_new = u_ref.shape[0]

    def ext(i, cs=slice(None)):
        return buf_ref[i, :, cs] if i < POOL_BUF else u_ref[i - POOL_BUF, :, cs]

    for r in range(POOL_BUF):
        new_ref[r] = ext(r + t_new)
    for t in range(t_new):
        cur = POOL_BUF + t
        for g, w in enumerate(POOL_WINDOWS):
            cs = slice(g * POOL_G, (g + 1) * POOL_G)
            win = ext(cur, cs)
            for j in range(1, w):
                win = win + ext(cur - j, cs)
            cnt = float(min(pos0 + t + 1, w))
            d_ref[t, :, cs] = win * (1.0 / cnt) - ext(cur, cs)


def pool_sample(buf_tm, u_tm, pos0, nb):
    nbuf, nseq, width = buf_tm.shape
    t_new = u_tm.shape[0]
    buf_spec = pl.BlockSpec((nbuf, nb, width), lambda i: (0, i, 0))
    new_spec = pl.BlockSpec((t_new, nb, width), lambda i: (0, i, 0))
    return pl.pallas_call(
        functools.partial(_pool_sample_kernel, pos0=pos0),
        out_shape=(jax.ShapeDtypeStruct(u_tm.shape, F32), jax.ShapeDtypeStruct(buf_tm.shape, F32)),
        grid=(nseq // nb,),
        in_specs=[buf_spec, new_spec],
        out_specs=(new_spec, buf_spec),
        compiler_params=_params(("parallel",)),
        name="pool_sample",
    )(buf_tm, u_tm)


def _pool_diff_tile(u_ref, halo_ref, ext_ref, l2_ref, l4_ref, l8_ref, tiles_per_seq):
    tm = u_ref.shape[0]
    n = tm + HALO
    t_in_seq = pl.program_id(0) % tiles_per_seq
    zeros_pad = jnp.zeros((PAD, D_MODEL), F32)
    ext_ref[0:PAD, :] = zeros_pad
    halo = halo_ref[...]
    ext_ref[PAD:PAD + HALO, :] = jnp.where(t_in_seq == 0, 0.0, halo)
    ext_ref[PAD + HALO:, :] = u_ref[...]
    l2_ref[0:PAD, :] = zeros_pad
    l4_ref[0:PAD, :] = zeros_pad[:, :3 * POOL_G]
    l2_ref[PAD:PAD + n, :] = ext_ref[PAD:PAD + n, :] + ext_ref[PAD - 1:PAD - 1 + n, :]
    l4_ref[PAD:PAD + n, :] = l2_ref[PAD:PAD + n, POOL_G:] + l2_ref[PAD - 2:PAD - 2 + n, POOL_G:]
    l8_ref[PAD:PAD + n, :] = l4_ref[PAD:PAD + n, POOL_G:] + l4_ref[PAD - 4:PAD - 4 + n, POOL_G:]
    base = PAD + HALO
    wins = (
        l2_ref[base:base + tm, 0:POOL_G],
        l4_ref[base:base + tm, 0:POOL_G],
        l8_ref[base:base + tm, 0:POOL_G],
        l8_ref[base:base + tm, POOL_G:] + l8_ref[base - 8:base - 8 + tm, POOL_G:],
    )
    pos1 = (t_in_seq * tm + 1 + lax.broadcasted_iota(jnp.int32, (tm, 1), 0)).astype(F32)
    out = []
    for g, w in enumerate(POOL_WINDOWS):
        inv = 1.0 / jnp.minimum(pos1, float(w))
        out.append(wins[g] * inv - u_ref[:, g * POOL_G:(g + 1) * POOL_G])
    return out


def _mix_out_kernel(*refs, fused_pool, tiles_per_seq):
    if fused_pool:
        (o_ref, u_ref, halo_ref, siga_ref, sigb_ref, x_ref, wmix_ref, ps_ref, wa_ref, wb_ref,
         wo_ref, h_ref, ext_ref, l2_ref, l4_ref, l8_ref) = refs
        branch_a = _dot(o_ref[...], wa_ref[...])
        diffs = _pool_diff_tile(u_ref, halo_ref, ext_ref, l2_ref, l4_ref, l8_ref, tiles_per_seq)
    else:
        (o_ref, d_ref, siga_ref, sigb_ref, x_ref, wmix_ref, ps_ref, wa_ref, wb_ref, wo_ref,
         h_ref) = refs
        branch_a = _dot(o_ref[...], wa_ref[...])
        diffs = [d_ref[:, g * POOL_G:(g + 1) * POOL_G] for g in range(len(POOL_WINDOWS))]
    pooled = []
    for g in range(len(POOL_WINDOWS)):
        y = _dot(diffs[g].astype(BF16), wmix_ref[g]) * ps_ref[:, g * POOL_G:(g + 1) * POOL_G]
        pooled.append(y.astype(BF16))
    pooled = jnp.concatenate(pooled, axis=1)
    merged = (siga_ref[...].astype(F32) * branch_a
              + sigb_ref[...].astype(F32) * _dot(pooled, wb_ref[...]))
    h_ref[...] = x_ref[...] + _dot(merged.astype(BF16), wo_ref[...])


def mix_out(o, u_or_d, sig_a, sig_b, x, wmix, pscale, wa, wb, wo, *, tm, seq_len, fused_pool):
    m, wide = x.shape
    row_spec = pl.BlockSpec((tm, wide), lambda i: (i, 0))
    const2 = lambda shape: pl.BlockSpec(shape, lambda i: (0, 0))
    w_specs = [
        pl.BlockSpec((len(POOL_WINDOWS), POOL_G, POOL_G), lambda i: (0, 0, 0)),
        const2((1, wide)),
        const2((GLA_VAL_W, wide)),
        const2((wide, wide)),
        const2((wide, wide)),
    ]
    if fused_pool:
        halo_blk = tm // HALO
        in_specs = [row_spec, row_spec,
                    pl.BlockSpec((HALO, wide), lambda i: (jnp.maximum(i * halo_blk - 1, 0), 0)),
                    row_spec, row_spec, row_spec] + w_specs
        args = (o, u_or_d, u_or_d, sig_a, sig_b, x)
        scratch = [pltpu.VMEM((tm + PAD + HALO, wide), F32),
                   pltpu.VMEM((tm + PAD + HALO, wide), F32),
                   pltpu.VMEM((tm + PAD + HALO, 3 * POOL_G), F32),
                   pltpu.VMEM((tm + PAD + HALO, 2 * POOL_G), F32)]
        tiles_per_seq = seq_len // tm
    else:
        in_specs = [row_spec] * 5 + w_specs
        args = (o, u_or_d, sig_a, sig_b, x)
        scratch = []
        tiles_per_seq = 1
    kern = functools.partial(_mix_out_kernel, fused_pool=fused_pool, tiles_per_seq=tiles_per_seq)
    return pl.pallas_call(
        kern,
        out_shape=jax.ShapeDtypeStruct((m, wide), F32),
        grid=(m // tm,),
        in_specs=in_specs,
        out_specs=row_spec,
        scratch_shapes=scratch,
        compiler_params=_params(("parallel",)),
        name="mix_out",
    )(*args, wmix, pscale.reshape(1, wide), wa, wb, wo)


def _softmax_rows(s):
    p = jnp.exp(s - jnp.max(s, axis=-1, keepdims=True))
    return p, 1.0 / jnp.sum(p, axis=-1, keepdims=True)


def _xattn_prompt_kernel(h_ref, g_ref, wq_ref, mk_ref, mv_ref, wo_ref, o_ref):
    h = h_ref[...]
    hn = _rmsnorm(h, g_ref[...]).astype(BF16)
    q = (_dot(hn, wq_ref[...]) * (X_HEAD_DIM ** -0.5)).astype(BF16)

    def scores(hd):
        cs = slice(hd * X_HEAD_DIM, (hd + 1) * X_HEAD_DIM)
        return _dot_nt(q[:, cs], mk_ref[0, :, cs].astype(BF16))

    outs = []
    s_next = scores(0)
    for hd in range(X_HEADS):
        s_cur = s_next
        if hd + 1 < X_HEADS:
            s_next = scores(hd + 1)
        p, inv = _softmax_rows(s_cur)
        vb = mv_ref[0, :, hd * X_HEAD_DIM:(hd + 1) * X_HEAD_DIM].astype(BF16)
        outs.append((_dot(p.astype(BF16), vb) * inv).astype(BF16))
    o = jnp.concatenate(outs, axis=1)
    o_ref[...] = h + _dot(o, wo_ref[...])


def xattn_prompt(h, g, wq, mk, mv, wo, *, tm, seq_len):
    m, d = h.shape
    tiles = seq_len // tm
    row_spec = pl.BlockSpec((tm, d), lambda i: (i, 0))
    mem_spec = pl.BlockSpec((1, MEM_LEN, d), lambda i: (i // tiles, 0, 0))
    w_spec = pl.BlockSpec((d, d), lambda i: (0, 0))
    return pl.pallas_call(
        _xattn_prompt_kernel,
        out_shape=jax.ShapeDtypeStruct((m, d), F32),
        grid=(m // tm,),
        in_specs=[row_spec, pl.BlockSpec((1, d), lambda i: (0, 0)), w_spec, mem_spec, mem_spec, w_spec],
        out_specs=row_spec,
        compiler_params=_params(("parallel",)),
        name="xattn_prompt",
    )(h, g.reshape(1, d), wq, mk, mv, wo)


def _sample_attention_probs(q_ref, k_ref):
    nseq, krows, _ = k_ref.shape
    half = XQ_ROWS // 2
    lane = lax.broadcasted_iota(jnp.int32, (half, krows), 1)
    row = lax.broadcasted_iota(jnp.int32, (half, krows), 0)
    valid = (lane & 7) == (row >> 2)
    out = []
    for j in range(nseq):
        q = (q_ref[j * XQ_ROWS:(j + 1) * XQ_ROWS, :] * (X_HEAD_DIM ** -0.5)).astype(BF16)
        g = _dot_nt(q, k_ref[j].astype(BF16))
        s = g[:half] + pltpu.roll(g[half:], krows - 4, axis=1)
        s = jnp.where(valid, s, -1e30)
        p = jnp.exp(s - jnp.max(s, axis=-1, keepdims=True))
        inv = 1.0 / jnp.sum(p, axis=-1, keepdims=True)
        pe = jnp.concatenate([p, pltpu.roll(p, 4, axis=1)], axis=0).astype(BF16)
        out.append((pe, jnp.concatenate([inv, inv], axis=0)))
    return out


def _sample_attention_values(probs, v_ref, o_ref):
    for j, (pe, inv) in enumerate(probs):
        o = _dot(pe, v_ref[j].astype(BF16)) * inv
        o_ref[j * XQ_ROWS:(j + 1) * XQ_ROWS, :] = o.astype(o_ref.dtype)


def _mlp_kernel(*refs, with_attn):
    if with_attn:
        (h_ref, g_ref, wu_ref, wd_ref, gf_ref, q_ref, k_ref, v_ref,
         y_ref, a_ref, hn_ref, acc_ref) = refs
    else:
        h_ref, g_ref, wu_ref, wd_ref, gf_ref, y_ref, hn_ref, acc_ref = refs
    f = pl.program_id(1)

    @pl.when(f == 0)
    def _():
        h = h_ref[...]
        hn_ref[...] = _rmsnorm(h, g_ref[...]).astype(BF16)
        acc_ref[...] = h

    if with_attn:
        probs = _sample_attention_probs(q_ref, k_ref)
    a = jnp.maximum(_dot(hn_ref[...], wu_ref[...]), 0.0)
    if with_attn:
        _sample_attention_values(probs, v_ref, a_ref)
    acc_ref[...] += _dot((a * a).astype(BF16), wd_ref[...])

    @pl.when(f == pl.num_programs(1) - 1)
    def _():
        y_ref[...] = _rmsnorm(acc_ref[...], gf_ref[...])


def mlp_final(h, g, wu, wd, gf, *, tm, tf, attn=None):
    m, d = h.shape
    ff = wu.shape[1]
    nf = ff // tf
    row_spec = pl.BlockSpec((tm, d), lambda i, f: (i, 0))
    vec_spec = pl.BlockSpec((1, d), lambda i, f: (0, 0))
    in_specs = [row_spec, vec_spec,
                pl.BlockSpec((d, tf), lambda i, f: (0, f)),
                pl.BlockSpec((tf, d), lambda i, f: (f, 0)),
                vec_spec]
    args = [h, g.reshape(1, d), wu, wd, gf.reshape(1, d)]
    out_shape = jax.ShapeDtypeStruct((m, d), F32)
    out_specs = row_spec
    if attn is not None:
        qhat, ck, cv = attn
        steps = (m // tm) * nf
        nseq = ck.shape[0] // steps
        assert nseq * steps == ck.shape[0]
        q_spec = pl.BlockSpec((nseq * XQ_ROWS, LANE), lambda i, f: (i * nf + f, 0))
        kv_spec = pl.BlockSpec((nseq,) + ck.shape[1:], lambda i, f: (i * nf + f, 0, 0))
        in_specs += [q_spec, kv_spec, kv_spec]
        args += [qhat, ck, cv]
        out_shape = (out_shape, jax.ShapeDtypeStruct(qhat.shape, BF16))
        out_specs = (row_spec, q_spec)
    return pl.pallas_call(
        functools.partial(_mlp_kernel, with_attn=attn is not None),
        out_shape=out_shape,
        grid=(m // tm, nf),
        in_specs=in_specs,
        out_specs=out_specs,
        scratch_shapes=[pltpu.VMEM((tm, d), BF16), pltpu.VMEM((tm, d), F32)],
        compiler_params=_params(("parallel", "arbitrary")),
        name="mlp_final",
    )(*args)


def _prep_weights_kernel(win_ref, *refs):
    n = (len(refs) - 1) // 2
    in_refs, win_out, out_refs = refs[:n], refs[n], refs[n + 1:]
    gz_lo = 2 * GLA_KEY_W + GLA_VAL_W
    gz_hi = gz_lo + GLA_GATE_RANK
    gz_out = IN_COLS["gz"][0]
    win_out[:gz_lo, :] = win_ref[:gz_lo, :].astype(BF16)
    win_out[gz_lo:gz_out, :] = win_ref[gz_hi:, :].astype(BF16)
    win_out[gz_out:gz_out + GLA_GATE_RANK, :] = win_ref[gz_lo:gz_hi, :].astype(BF16)
    win_out[gz_out + GLA_GATE_RANK:, :] = jnp.zeros(
        (PROJ_W - gz_out - GLA_GATE_RANK, win_out.shape[1]), BF16)
    for i_ref, o_ref in zip(in_refs, out_refs):
        o_ref[...] = i_ref[...].astype(BF16)


def prep_weights(w_in_t, others, nblk=8):
    def spec(shape):
        return pl.BlockSpec((None, shape[1] // nblk, shape[2]), lambda i: (0, i, 0))

    def out_spec(rows, cols):
        return pl.BlockSpec((rows // nblk, cols), lambda i: (i, 0))

    width, d = w_in_t.shape
    out_shapes = [jax.ShapeDtypeStruct((PROJ_W, d), BF16)]
    out_specs = [pl.BlockSpec((PROJ_W, d // nblk), lambda i: (0, i))]
    for w in others:
        out_shapes.append(jax.ShapeDtypeStruct(w.shape[1:], BF16))
        out_specs.append(out_spec(*w.shape[1:]))
    return pl.pallas_call(
        _prep_weights_kernel,
        out_shape=tuple(out_shapes),
        grid=(nblk,),
        in_specs=[pl.BlockSpec((width, d // nblk), lambda i: (0, i))] + [spec(w.shape) for w in others],
        out_specs=tuple(out_specs),
        compiler_params=_params(("parallel",)),
        name="prep_weights",
    )(w_in_t, *others)


def _cache_rows(c):
    b, m, h, dh = c.shape
    return c.reshape(b, m, h, dh // LANE, LANE).transpose(0, 1, 3, 2, 4).reshape(b, m * h * (dh // LANE), LANE)


def kernel(x_prompt, x_sample, mem_prompt, state_gla, state_pool, cache_mem_k, cache_mem_v,
           norm_mix_g, w_in, w_gk_up, b_gk, gla_norm_g, w_pool_mix, pool_scale,
           w_branch_a, w_branch_b, w_out, norm_x_g, norm_mem_g, w_xq, w_xk, w_xv, w_xo,
           norm_mlp_g, w_up, w_down, norm_final_g):
    depth = w_in.shape[0]
    assert depth == 1
    batch, seq, d = x_prompt.shape
    dec_batch, dec_seq, _ = x_sample.shape
    mp = batch * seq
    ms = dec_batch * dec_seq

    w_in_r, wxk, wxv = prep_weights(w_in[0].T, (w_xk, w_xv))
    wgk = jnp.concatenate(
        [w_gk_up[0], jnp.zeros((LANE - GLA_GATE_RANK, GLA_KEY_W), F32)], axis=0).astype(BF16)
    wmix = w_pool_mix[0].astype(BF16)

    xp = x_prompt.reshape(mp, d)
    xs = x_sample.reshape(ms, d)

    (qk_p, v_p, gate_p, u_p, sa_p, sb_p, lah_p, lal_p, wa, wb, wo, wxq, wxo, wu, wd) = in_proj(
        xp, norm_mix_g[0], w_in_r, wgk, b_gk[0], TM_PROJ,
        cast=(w_branch_a, w_branch_b, w_out, w_xq, w_xo, w_up, w_down))

    qk_s, v_s, gate_s, u_s, sa_s, sb_s, lah_s, lal_s = in_proj(xs, norm_mix_g[0], w_in_r, wgk, b_gk[0], ms)
    gla_rows = SAMPLE_SEQS_PER_GLA_STEP * dec_seq
    o_s, sg_s = gla(qk_s, v_s, lah_s, lal_s, gate_s, gla_norm_g[0], state_gla[0],
                    groups=ms // gla_rows, tt=gla_rows, rows=gla_rows, seg=dec_seq)
    d_tm, sp_tm = pool_sample(state_pool[0].transpose(1, 0, 2),
                              u_s.reshape(dec_batch, dec_seq, d).transpose(1, 0, 2),
                              PAST_LEN, SAMPLE_SEQS_PER_POOL_STEP)
    h_s = mix_out(o_s, d_tm.transpose(1, 0, 2).reshape(ms, d), sa_s, sb_s, xs, wmix, pool_scale[0],
                  wa, wb, wo, tm=ms, seq_len=dec_seq, fused_pool=False)
    q_s = norm_matmul(h_s, norm_x_g[0], wxq, tm=ms, tn=d)
    halves = X_HEAD_DIM // LANE
    qhat = q_s.reshape(dec_batch, dec_seq, X_HEADS, halves, LANE).transpose(0, 3, 2, 1, 4)
    qhat = qhat.reshape(dec_batch * XQ_ROWS, LANE)

    mem = mem_prompt.reshape(batch * MEM_LEN, d)
    mk_p = norm_matmul(mem, norm_mem_g[0], wxk, tm=1024, tn=1024)
    mv_p = norm_matmul(mem, norm_mem_g[0], wxv, tm=1024, tn=1024)
    o_p, sg_p = gla(qk_p, v_p, lah_p, lal_p, gate_p, gla_norm_g[0], None,
                    groups=batch, tt=TT_GLA, rows=GLA_CHUNK, seg=GLA_CHUNK)
    h_p = mix_out(o_p, u_p, sa_p, sb_p, xp, wmix, pool_scale[0], wa, wb, wo,
                  tm=TM_MIX, seq_len=seq, fused_pool=True)
    h_p = xattn_prompt(h_p, norm_x_g[0], wxq, mk_p.reshape(batch, MEM_LEN, d),
                       mv_p.reshape(batch, MEM_LEN, d), wxo, tm=TM_XATTN, seq_len=seq)
    y_p, a_s = mlp_final(h_p, norm_mlp_g[0], wu, wd, norm_final_g, tm=TM_MLP, tf=TF_MLP,
                         attn=(qhat, _cache_rows(cache_mem_k[0]), _cache_rows(cache_mem_v[0])))
    sp_p = u_p.reshape(batch, seq, d)[:, seq - POOL_BUF:]

    a_s = a_s.reshape(dec_batch, halves, X_HEADS, dec_seq, LANE).transpose(0, 3, 2, 1, 4).reshape(ms, d)
    h_s = matmul_residual(a_s, wxo, h_s, tm=ms)
    y_s = mlp_final(h_s, norm_mlp_g[0], wu, wd, norm_final_g, tm=ms, tf=TF_MLP)
    sp_s = sp_tm.transpose(1, 0, 2)

    return (y_p.reshape(batch, seq, d),
            y_s.reshape(dec_batch, dec_seq, d),
            mk_p.reshape(1, batch, MEM_LEN, X_HEADS, X_HEAD_DIM),
            mv_p.reshape(1, batch, MEM_LEN, X_HEADS, X_HEAD_DIM),
            sg_p[None],
            sg_s[None],
            sp_p[None],
            sp_s[None])
```

```python
import functools

import jax
import jax.numpy as jnp
from jax import lax
from jax.experimental import pallas as pl
from jax.experimental.pallas import tpu as pltpu

F32 = jnp.float32
BF16 = jnp.bfloat16

D_MODEL = 1024
GLA_HEADS = 4
GLA_DK = 128
GLA_DV = 256
GLA_KEY_W = GLA_HEADS * GLA_DK
GLA_VAL_W = GLA_HEADS * GLA_DV
GLA_GATE_RANK = 16
GLA_GATE_NORM = 16.0
POOL_WINDOWS = (2, 4, 8, 16)
POOL_G = 256
POOL_BUF = 15
MEM_LEN = 256
X_HEADS = 4
X_HEAD_DIM = 256
EPS = 1e-6
PAST_LEN = 16384

LANE = 128
HALO = 16
PAD = 8
VMEM_LIMIT = 52 * 1024 * 1024
XQ_ROWS = 32

IN_COLS = {
    "qk": (0, 2 * GLA_KEY_W),
    "v": (1024, GLA_VAL_W),
    "og": (2048, GLA_VAL_W),
    "u": (3072, D_MODEL),
    "ga": (4096, D_MODEL),
    "gb": (5120, D_MODEL),
    "gz": (6144, LANE),
}
PROJ_W = 6272

TM_PROJ = 512
TM_MIX = 512
TM_XATTN = 512
TM_MLP = 1024
TF_MLP = 1024
TT_GLA = 512
GLA_CHUNK = 128
SAMPLE_SEQS_PER_GLA_STEP = 16
SAMPLE_SEQS_PER_POOL_STEP = 32


def _params(sem):
    return pltpu.CompilerParams(dimension_semantics=sem, vmem_limit_bytes=VMEM_LIMIT)


def _rmsnorm(x, g):
    return x * lax.rsqrt(jnp.mean(x * x, axis=-1, keepdims=True) + EPS) * g


def _dot(a, b):
    return jnp.dot(a, b, preferred_element_type=F32)


def _dot_nt(a, b):
    return lax.dot_general(a, b, (((1,), (1,)), ((), ())), preferred_element_type=F32)


def _dot_tn(a, b):
    return lax.dot_general(a, b, (((0,), (0,)), ((), ())), preferred_element_type=F32)


def _norm_matmul_kernel(x_ref, g_ref, w_ref, o_ref, xn_ref):
    @pl.when(pl.program_id(1) == 0)
    def _():
        xn_ref[...] = _rmsnorm(x_ref[...], g_ref[...]).astype(BF16)

    o_ref[...] = _dot(xn_ref[...], w_ref[...]).astype(o_ref.dtype)


def norm_matmul(x, g, w, tm, tn, out_dtype=F32):
    m, k = x.shape
    n = w.shape[1]
    return pl.pallas_call(
        _norm_matmul_kernel,
        out_shape=jax.ShapeDtypeStruct((m, n), out_dtype),
        grid=(m // tm, n // tn),
        in_specs=[
            pl.BlockSpec((tm, k), lambda i, j: (i, 0)),
            pl.BlockSpec((1, k), lambda i, j: (0, 0)),
            pl.BlockSpec((k, tn), lambda i, j: (0, j)),
        ],
        out_specs=pl.BlockSpec((tm, tn), lambda i, j: (i, j)),
        scratch_shapes=[pltpu.VMEM((tm, k), BF16)],
        compiler_params=_params(("parallel", "arbitrary")),
        name="norm_matmul",
    )(x, g.reshape(1, k), w)


def _log_decay_split(z):
    la = (jnp.minimum(z, 0.0) - jnp.log(1.0 + jnp.exp(-jnp.abs(z)))) * (1.0 / GLA_GATE_NORM)
    la_hi = la.astype(BF16)
    return la_hi, (la - la_hi.astype(F32)).astype(BF16)


def _in_proj_kernel(*refs, ncast):
    (x_ref, g_ref, w_ref, wgk_ref, bgk_ref) = refs[:5]
    cast_in = refs[5:5 + ncast]
    (qk_ref, v_ref, gate_ref, u_ref, siga_ref, sigb_ref, lahi_ref, lalo_ref) = refs[5 + ncast:13 + ncast]
    cast_out = refs[13 + ncast:]
    xn = _rmsnorm(x_ref[...], g_ref[...]).astype(BF16)

    def piece(name):
        col, width = IN_COLS[name]
        return _dot_nt(xn, w_ref[col:col + width, :])

    z = _dot(piece("gz").astype(BF16), wgk_ref[...]) + bgk_ref[...]
    lahi_ref[...], lalo_ref[...] = _log_decay_split(z)
    og = piece("og")
    gate_ref[...] = (og * jax.nn.sigmoid(og)).astype(BF16)
    siga_ref[...] = jax.nn.sigmoid(piece("ga")).astype(BF16)
    sigb_ref[...] = jax.nn.sigmoid(piece("gb")).astype(BF16)
    v_ref[...] = piece("v").astype(BF16)
    qk_ref[...] = piece("qk")
    u_ref[...] = piece("u")
    for i_ref, o_ref in zip(cast_in, cast_out):
        o_ref[...] = i_ref[...].astype(BF16)


def in_proj(x, g, w, wgk, bgk, tm, cast=()):
    m, k = x.shape
    steps = m // tm
    outs = ((2 * GLA_KEY_W, F32), (GLA_VAL_W, BF16), (GLA_VAL_W, BF16), (D_MODEL, F32),
            (D_MODEL, BF16), (D_MODEL, BF16), (GLA_KEY_W, BF16), (GLA_KEY_W, BF16))
    const = lambda shape: pl.BlockSpec(shape, lambda i: (0, 0))
    out_shape = [jax.ShapeDtypeStruct((m, width), dtype) for width, dtype in outs]
    out_specs = [pl.BlockSpec((tm, width), lambda i: (i, 0)) for width, _ in outs]
    in_specs = [pl.BlockSpec((tm, k), lambda i: (i, 0)), const((1, k)), const((PROJ_W, k)),
                const((LANE, GLA_KEY_W)), const((1, GLA_KEY_W))]
    for wc in cast:
        _, rows, cols = wc.shape
        in_specs.append(pl.BlockSpec((None, rows // steps, cols), lambda i: (0, i, 0)))
        out_specs.append(pl.BlockSpec((rows // steps, cols), lambda i: (i, 0)))
        out_shape.append(jax.ShapeDtypeStruct((rows, cols), BF16))
    return pl.pallas_call(
        functools.partial(_in_proj_kernel, ncast=len(cast)),
        out_shape=tuple(out_shape),
        grid=(steps,),
        in_specs=in_specs,
        out_specs=tuple(out_specs),
        compiler_params=_params(("parallel",)),
        name="in_proj",
    )(x, g.reshape(1, k), w, wgk, bgk.reshape(1, GLA_KEY_W), *cast)


def _matmul_res_kernel(x_ref, w_ref, r_ref, o_ref):
    o_ref[...] = _dot(x_ref[...].astype(BF16), w_ref[...]) + r_ref[...]


def matmul_residual(x, w, res, tm):
    m, k = x.shape
    n = w.shape[1]
    return pl.pallas_call(
        _matmul_res_kernel,
        out_shape=jax.ShapeDtypeStruct((m, n), F32),
        grid=(m // tm,),
        in_specs=[
            pl.BlockSpec((tm, k), lambda i: (i, 0)),
            pl.BlockSpec((k, n), lambda i: (0, 0)),
            pl.BlockSpec((tm, n), lambda i: (i, 0)),
        ],
        out_specs=pl.BlockSpec((tm, n), lambda i: (i, 0)),
        compiler_params=_params(("parallel",)),
        name="matmul_residual",
    )(x, w, res)


def _gla_kernel(q_ref, k_ref, v_ref, lahi_ref, lalo_ref, gate_ref, gn_ref, s0_ref,
                o_ref, s_ref, *, rows, seg, zero_init):
    tt = q_ref.shape[0]
    nchunks = tt // rows
    nseg = rows // seg
    seg_shift = seg.bit_length() - 1

    @pl.when(pl.program_id(1) == 0)
    def _():
        if zero_init:
            s_ref[...] = jnp.zeros(s_ref.shape, F32)
        else:
            s_ref[...] = s0_ref[...]

    ri = lax.broadcasted_iota(jnp.int32, (rows, rows), 0)
    ci = lax.broadcasted_iota(jnp.int32, (rows, rows), 1)
    same_seg = (ri >> seg_shift) == (ci >> seg_shift)
    causal = jnp.logical_and(same_seg, ci <= ri)
    l_cum = jnp.where(causal, 1.0, 0.0).astype(BF16)
    l_seg = jnp.where(same_seg, 1.0, 0.0).astype(BF16)
    row_seg = lax.broadcasted_iota(jnp.int32, (rows, GLA_DK), 0) >> seg_shift
    row_seg_v = lax.broadcasted_iota(jnp.int32, (rows, GLA_DV), 0) >> seg_shift
    gn = gn_ref[...]
    qscale = GLA_DK ** -0.5

    def cumulative(c):
        rsl = slice(c * rows, (c + 1) * rows)
        lah = lahi_ref[rsl, :]
        lal = lalo_ref[rsl, :]
        b = _dot(l_cum, lah) + _dot(l_cum, lal)
        if nseg == 1:
            b_end = b[rows - 1:rows, :]
        else:
            b_end = _dot(l_seg, lah) + _dot(l_seg, lal)
        return lah, lal, b, b_end

    ahead = cumulative(0)
    for c in range(nchunks):
        rsl = slice(c * rows, (c + 1) * rows)
        lah, lal, b, b_end = ahead
        if c + 1 < nchunks:
            ahead = cumulative(c + 1)
        q = q_ref[rsl, :]
        k = k_ref[rsl, :]
        qt_b = ((q * qscale) * jnp.exp(b)).astype(BF16)
        kt_b = (k * jnp.exp(-b)).astype(BF16)
        kd = k * jnp.exp(b_end - b)
        heads = []
        for h in range(GLA_HEADS):
            ks = slice(h * GLA_DK, (h + 1) * GLA_DK)
            v_h = v_ref[rsl, h * GLA_DV:(h + 1) * GLA_DV]
            scores = _dot_nt(qt_b[:, ks], kt_b[:, ks])
            inter = None
            new_states = []
            for j in range(nseg):
                s_old = s_ref[j, h]
                inter_j = _dot(qt_b[:, ks], s_old.astype(BF16))
                if nseg == 1:
                    inter = inter_j
                    kd_j = kd[:, ks].astype(BF16)
                    dcol = jnp.broadcast_to(b_end[:, ks], (GLA_DK, GLA_DK)).T
                else:
                    inter_j = jnp.where(row_seg_v == j, inter_j, 0.0)
                    inter = inter_j if inter is None else inter + inter_j
                    kd_j = jnp.where(row_seg == j, kd[:, ks], 0.0).astype(BF16)
                    ones_j = jnp.where(row_seg == j, 1.0, 0.0).astype(BF16)
                    dcol = _dot_tn(lah[:, ks], ones_j) + _dot_tn(lal[:, ks], ones_j)
                new_states.append((s_old, dcol, _dot_tn(kd_j, v_h)))
            heads.append((v_h, scores, inter, new_states))
        outs = []
        for v_h, scores, inter, _ in heads:
            a = jnp.where(causal, scores, 0.0).astype(BF16)
            outs.append(_dot(a, v_h) + inter)
        for h, (_, _, _, new_states) in enumerate(heads):
            vs = slice(h * GLA_DV, (h + 1) * GLA_DV)
            for j, (s_old, dcol, upd) in enumerate(new_states):
                e = jnp.exp(dcol)
                s_ref[j, h] = s_old * jnp.concatenate([e, e], axis=1) + upd
            on = _rmsnorm(outs[h], gn)
            o_ref[rsl, vs] = (on * gate_ref[rsl, vs].astype(F32)).astype(o_ref.dtype)


def gla(qk, v, la_hi, la_lo, gate, gn, s0, *, groups, tt, rows, seg):
    m = qk.shape[0]
    steps = m // (groups * tt)
    nseg = rows // seg
    zero_init = s0 is None
    if zero_init:
        s0 = jnp.zeros((nseg, GLA_HEADS, 8, LANE), F32)
        s0_spec = pl.BlockSpec((nseg, GLA_HEADS, 8, LANE), lambda g, t: (0, 0, 0, 0))
    else:
        s0_spec = pl.BlockSpec((nseg, GLA_HEADS, GLA_DK, GLA_DV), lambda g, t: (g, 0, 0, 0))

    def row_spec(width, blk=0):
        return pl.BlockSpec((tt, width), lambda g, t: (g * steps + t, blk))

    kern = functools.partial(_gla_kernel, rows=rows, seg=seg, zero_init=zero_init)
    return pl.pallas_call(
        kern,
        out_shape=(jax.ShapeDtypeStruct((m, GLA_VAL_W), BF16),
                   jax.ShapeDtypeStruct((groups * nseg, GLA_HEADS, GLA_DK, GLA_DV), F32)),
        grid=(groups, steps),
        in_specs=[
            row_spec(GLA_KEY_W, 0),
            row_spec(GLA_KEY_W, 1),
            row_spec(GLA_VAL_W),
            row_spec(GLA_KEY_W),
            row_spec(GLA_KEY_W),
            row_spec(GLA_VAL_W),
            pl.BlockSpec((1, GLA_DV), lambda g, t: (0, 0)),
            s0_spec,
        ],
        out_specs=(row_spec(GLA_VAL_W),
                   pl.BlockSpec((nseg, GLA_HEADS, GLA_DK, GLA_DV), lambda g, t: (g, 0, 0, 0))),
        compiler_params=_params(("parallel", "arbitrary")),
        name="gla",
    )(qk, qk, v, la_hi, la_lo, gate, gn.reshape(1, GLA_DV), s0)


def _pool_sample_kernel(buf_ref, u_ref, d_ref, new_ref, *, pos0):
    t_new = u_ref.shape[0]

    def ext(i, cs=slice(None)):
        return buf_ref[i, :, cs] if i < POOL_BUF else u_ref[i - POOL_BUF, :, cs]

    for r in range(POOL_BUF):
        new_ref[r] = ext(r + t_new)
    for t in range(t_new):
        cur = POOL_BUF + t
        for g, w in enumerate(POOL_WINDOWS):
            cs = slice(g * POOL_G, (g + 1) * POOL_G)
            win = ext(cur, cs)
            for j in range(1, w):
                win = win + ext(cur - j, cs)
            cnt = float(min(pos0 + t + 1, w))
            d_ref[t, :, cs] = win * (1.0 / cnt) - ext(cur, cs)


def pool_sample(buf_tm, u_tm, pos0, nb):
    nbuf, nseq, width = buf_tm.shape
    t_new = u_tm.shape[0]
    buf_spec = pl.BlockSpec((nbuf, nb, width), lambda i: (0, i, 0))
    new_spec = pl.BlockSpec((t_new, nb, width), lambda i: (0, i, 0))
    return pl.pallas_call(
        functools.partial(_pool_sample_kernel, pos0=pos0),
        out_shape=(jax.ShapeDtypeStruct(u_tm.shape, F32), jax.ShapeDtypeStruct(buf_tm.shape, F32)),
        grid=(nseq // nb,),
        in_specs=[buf_spec, new_spec],
        out_specs=(new_spec, buf_spec),
        compiler_params=_params(("parallel",)),
        name="pool_sample",
    )(buf_tm, u_tm)


def _pool_diff_tile(u_ref, halo_ref, ext_ref, l2_ref, l4_ref, l8_ref, tiles_per_seq):
    tm = u_ref.shape[0]
    n = tm + HALO
    t_in_seq = pl.program_id(0) % tiles_per_seq
    zeros_pad = jnp.zeros((PAD, D_MODEL), F32)
    ext_ref[0:PAD, :] = zeros_pad
    halo = halo_ref[...]
    ext_ref[PAD:PAD + HALO, :] = jnp.where(t_in_seq == 0, 0.0, halo)
    ext_ref[PAD + HALO:, :] = u_ref[...]
    l2_ref[0:PAD, :] = zeros_pad
    l4_ref[0:PAD, :] = zeros_pad[:, :3 * POOL_G]
    l2_ref[PAD:PAD + n, :] = ext_ref[PAD:PAD + n, :] + ext_ref[PAD - 1:PAD - 1 + n, :]
    l4_ref[PAD:PAD + n, :] = l2_ref[PAD:PAD + n, POOL_G:] + l2_ref[PAD - 2:PAD - 2 + n, POOL_G:]
    l8_ref[PAD:PAD + n, :] = l4_ref[PAD:PAD + n, POOL_G:] + l4_ref[PAD - 4:PAD - 4 + n, POOL_G:]
    base = PAD + HALO
    wins = (
        l2_ref[base:base + tm, 0:POOL_G],
        l4_ref[base:base + tm, 0:POOL_G],
        l8_ref[base:base + tm, 0:POOL_G],
        l8_ref[base:base + tm, POOL_G:] + l8_ref[base - 8:base - 8 + tm, POOL_G:],
    )
    pos1 = (t_in_seq * tm + 1 + lax.broadcasted_iota(jnp.int32, (tm, 1), 0)).astype(F32)
    out = []
    for g, w in enumerate(POOL_WINDOWS):
        inv = 1.0 / jnp.minimum(pos1, float(w))
        out.append(wins[g] * inv - u_ref[:, g * POOL_G:(g + 1) * POOL_G])
    return out


def _mix_out_kernel(*refs, fused_pool, tiles_per_seq):
    if fused_pool:
        (o_ref, u_ref, halo_ref, siga_ref, sigb_ref, x_ref, wmix_ref, ps_ref, wa_ref, wb_ref,
         wo_ref, h_ref, ext_ref, l2_ref, l4_ref, l8_ref) = refs
        branch_a = _dot(o_ref[...], wa_ref[...])
        diffs = _pool_diff_tile(u_ref, halo_ref, ext_ref, l2_ref, l4_ref, l8_ref, tiles_per_seq)
    else:
        (o_ref, d_ref, siga_ref, sigb_ref, x_ref, wmix_ref, ps_ref, wa_ref, wb_ref, wo_ref,
         h_ref) = refs
        branch_a = _dot(o_ref[...], wa_ref[...])
        diffs = [d_ref[:, g * POOL_G:(g + 1) * POOL_G] for g in range(len(POOL_WINDOWS))]
    pooled = []
    for g in range(len(POOL_WINDOWS)):
        y = _dot(diffs[g].astype(BF16), wmix_ref[g]) * ps_ref[:, g * POOL_G:(g + 1) * POOL_G]
        pooled.append(y.astype(BF16))
    pooled = jnp.concatenate(pooled, axis=1)
    merged = (siga_ref[...].astype(F32) * branch_a
              + sigb_ref[...].astype(F32) * _dot(pooled, wb_ref[...]))
    h_ref[...] = x_ref[...] + _dot(merged.astype(BF16), wo_ref[...])


def mix_out(o, u_or_d, sig_a, sig_b, x, wmix, pscale, wa, wb, wo, *, tm, seq_len, fused_pool):
    m, wide = x.shape
    row_spec = pl.BlockSpec((tm, wide), lambda i: (i, 0))
    const2 = lambda shape: pl.BlockSpec(shape, lambda i: (0, 0))
    w_specs = [
        pl.BlockSpec((len(POOL_WINDOWS), POOL_G, POOL_G), lambda i: (0, 0, 0)),
        const2((1, wide)),
        const2((GLA_VAL_W, wide)),
        const2((wide, wide)),
        const2((wide, wide)),
    ]
    if fused_pool:
        halo_blk = tm // HALO
        in_specs = [row_spec, row_spec,
                    pl.BlockSpec((HALO, wide), lambda i: (jnp.maximum(i * halo_blk - 1, 0), 0)),
                    row_spec, row_spec, row_spec] + w_specs
        args = (o, u_or_d, u_or_d, sig_a, sig_b, x)
        scratch = [pltpu.VMEM((tm + PAD + HALO, wide), F32),
                   pltpu.VMEM((tm + PAD + HALO, wide), F32),
                   pltpu.VMEM((tm + PAD + HALO, 3 * POOL_G), F32),
                   pltpu.VMEM((tm + PAD + HALO, 2 * POOL_G), F32)]
        tiles_per_seq = seq_len // tm
    else:
        in_specs = [row_spec] * 5 + w_specs
        args = (o, u_or_d, sig_a, sig_b, x)
        scratch = []
        tiles_per_seq = 1
    kern = functools.partial(_mix_out_kernel, fused_pool=fused_pool, tiles_per_seq=tiles_per_seq)
    return pl.pallas_call(
        kern,
        out_shape=jax.ShapeDtypeStruct((m, wide), F32),
        grid=(m // tm,),
        in_specs=in_specs,
        out_specs=row_spec,
        scratch_shapes=scratch,
        compiler_params=_params(("parallel",)),
        name="mix_out",
    )(*args, wmix, pscale.reshape(1, wide), wa, wb, wo)


def _softmax_rows(s):
    p = jnp.exp(s - jnp.max(s, axis=-1, keepdims=True))
    return p, 1.0 / jnp.sum(p, axis=-1, keepdims=True)


def _xattn_prompt_kernel(h_ref, g_ref, wq_ref, mem_ref, gm_ref, wk_ref, wv_ref, wo_ref,
                         o_ref, mk_ref, mv_ref, kb_ref, vb_ref, *, tiles):
    groups = 2 * X_HEADS

    @pl.when(pl.program_id(0) % tiles == 0)
    def _():
        mn = _rmsnorm(mem_ref[0], gm_ref[...]).astype(BF16)
        k = _dot(mn, wk_ref[...])
        v = _dot(mn, wv_ref[...])
        kb_ref[...] = k.astype(BF16)
        vb_ref[...] = v.astype(BF16)
        mk_ref[0] = k.reshape(MEM_LEN, groups, LANE)
        mv_ref[0] = v.reshape(MEM_LEN, groups, LANE)

    def head(ref, hd):
        lo = ref[:, hd * LANE:(hd + 1) * LANE]
        hi = ref[:, (X_HEADS + hd) * LANE:(X_HEADS + hd + 1) * LANE]
        return jnp.concatenate([lo, hi], axis=1)

    h = h_ref[...]
    hn = _rmsnorm(h, g_ref[...]).astype(BF16)
    q = (_dot(hn, wq_ref[...]) * (X_HEAD_DIM ** -0.5)).astype(BF16)

    def scores(hd):
        return _dot_nt(q[:, hd * X_HEAD_DIM:(hd + 1) * X_HEAD_DIM], head(kb_ref, hd))

    outs = []
    s_next = scores(0)
    for hd in range(X_HEADS):
        s_cur = s_next
        if hd + 1 < X_HEADS:
            s_next = scores(hd + 1)
        p, inv = _softmax_rows(s_cur)
        outs.append((_dot(p.astype(BF16), head(vb_ref, hd)) * inv).astype(BF16))
    o = jnp.concatenate(outs, axis=1)
    o_ref[...] = h + _dot(o, wo_ref[...])


def xattn_prompt(h, g, wq, mem, gm, wk, wv, wo, *, tm):
    m, d = h.shape
    batch = mem.shape[0]
    tiles = m // batch // tm
    groups = 2 * X_HEADS
    row_spec = pl.BlockSpec((tm, d), lambda i: (i, 0))
    vec_spec = pl.BlockSpec((1, d), lambda i: (0, 0))
    w_spec = pl.BlockSpec((d, d), lambda i: (0, 0))
    kv_spec = pl.BlockSpec((1, MEM_LEN, groups, LANE), lambda i: (i // tiles, 0, 0, 0))
    kv_shape = jax.ShapeDtypeStruct((batch, MEM_LEN, groups, LANE), F32)
    return pl.pallas_call(
        functools.partial(_xattn_prompt_kernel, tiles=tiles),
        out_shape=(jax.ShapeDtypeStruct((m, d), F32), kv_shape, kv_shape),
        grid=(m // tm,),
        in_specs=[row_spec, vec_spec, w_spec,
                  pl.BlockSpec((1, MEM_LEN, d), lambda i: (i // tiles, 0, 0)), vec_spec,
                  w_spec, w_spec, w_spec],
        out_specs=(row_spec, kv_spec, kv_spec),
        scratch_shapes=[pltpu.VMEM((MEM_LEN, d), BF16), pltpu.VMEM((MEM_LEN, d), BF16)],
        compiler_params=_params(("arbitrary",)),
        name="xattn_prompt",
    )(h, g.reshape(1, d), wq, mem, gm.reshape(1, d), wk, wv, wo)


def _sample_attention_probs(q_ref, k_ref):
    nseq, krows, _ = k_ref.shape
    half = XQ_ROWS // 2
    lane = lax.broadcasted_iota(jnp.int32, (half, krows), 1)
    row = lax.broadcasted_iota(jnp.int32, (half, krows), 0)
    valid = (lane & 7) == (row >> 2)
    out = []
    for j in range(nseq):
        q = (q_ref[j * XQ_ROWS:(j + 1) * XQ_ROWS, :] * (X_HEAD_DIM ** -0.5)).astype(BF16)
        g = _dot_nt(q, k_ref[j].astype(BF16))
        s = g[:half] + pltpu.roll(g[half:], krows - 4, axis=1)
        s = jnp.where(valid, s, -1e30)
        p = jnp.exp(s - jnp.max(s, axis=-1, keepdims=True))
        inv = 1.0 / jnp.sum(p, axis=-1, keepdims=True)
        pe = jnp.concatenate([p, pltpu.roll(p, 4, axis=1)], axis=0).astype(BF16)
        out.append((pe, jnp.concatenate([inv, inv], axis=0)))
    return out


def _sample_attention_values(probs, v_ref, o_ref):
    for j, (pe, inv) in enumerate(probs):
        o = _dot(pe, v_ref[j].astype(BF16)) * inv
        o_ref[j * XQ_ROWS:(j + 1) * XQ_ROWS, :] = o.astype(o_ref.dtype)


def _mlp_kernel(*refs, with_attn):
    if with_attn:
        (h_ref, g_ref, wu_ref, wd_ref, gf_ref, q_ref, k_ref, v_ref,
         y_ref, a_ref, hn_ref, acc_ref) = refs
    else:
        h_ref, g_ref, wu_ref, wd_ref, gf_ref, y_ref, hn_ref, acc_ref = refs
    f = pl.program_id(1)

    @pl.when(f == 0)
    def _():
        h = h_ref[...]
        hn_ref[...] = _rmsnorm(h, g_ref[...]).astype(BF16)
        acc_ref[...] = h

    if with_attn:
        probs = _sample_attention_probs(q_ref, k_ref)
    a = jnp.maximum(_dot(hn_ref[...], wu_ref[...]), 0.0)
    if with_attn:
        _sample_attention_values(probs, v_ref, a_ref)
    acc_ref[...] += _dot((a * a).astype(BF16), wd_ref[...])

    @pl.when(f == pl.num_programs(1) - 1)
    def _():
        y_ref[...] = _rmsnorm(acc_ref[...], gf_ref[...])


def mlp_final(h, g, wu, wd, gf, *, tm, tf, attn=None):
    m, d = h.shape
    ff = wu.shape[1]
    nf = ff // tf
    row_spec = pl.BlockSpec((tm, d), lambda i, f: (i, 0))
    vec_spec = pl.BlockSpec((1, d), lambda i, f: (0, 0))
    in_specs = [row_spec, vec_spec,
                pl.BlockSpec((d, tf), lambda i, f: (0, f)),
                pl.BlockSpec((tf, d), lambda i, f: (f, 0)),
                vec_spec]
    args = [h, g.reshape(1, d), wu, wd, gf.reshape(1, d)]
    out_shape = jax.ShapeDtypeStruct((m, d), F32)
    out_specs = row_spec
    if attn is not None:
        qhat, ck, cv = attn
        steps = (m // tm) * nf
        nseq = ck.shape[0] // steps
        assert nseq * steps == ck.shape[0]
        q_spec = pl.BlockSpec((nseq * XQ_ROWS, LANE), lambda i, f: (i * nf + f, 0))
        kv_spec = pl.BlockSpec((nseq,) + ck.shape[1:], lambda i, f: (i * nf + f, 0, 0))
        in_specs += [q_spec, kv_spec, kv_spec]
        args += [qhat, ck, cv]
        out_shape = (out_shape, jax.ShapeDtypeStruct(qhat.shape, BF16))
        out_specs = (row_spec, q_spec)
    return pl.pallas_call(
        functools.partial(_mlp_kernel, with_attn=attn is not None),
        out_shape=out_shape,
        grid=(m // tm, nf),
        in_specs=in_specs,
        out_specs=out_specs,
        scratch_shapes=[pltpu.VMEM((tm, d), BF16), pltpu.VMEM((tm, d), F32)],
        compiler_params=_params(("parallel", "arbitrary")),
        name="mlp_final",
    )(*args)


def _prep_weights_kernel(win_ref, *refs):
    n = (len(refs) - 1) // 2
    in_refs, win_out, out_refs = refs[:n], refs[n], refs[n + 1:]
    gz_lo = 2 * GLA_KEY_W + GLA_VAL_W
    gz_hi = gz_lo + GLA_GATE_RANK
    gz_out = IN_COLS["gz"][0]
    win_out[:gz_lo, :] = win_ref[:gz_lo, :].astype(BF16)
    win_out[gz_lo:gz_out, :] = win_ref[gz_hi:, :].astype(BF16)
    win_out[gz_out:gz_out + GLA_GATE_RANK, :] = win_ref[gz_lo:gz_hi, :].astype(BF16)
    win_out[gz_out + GLA_GATE_RANK:, :] = jnp.zeros(
        (PROJ_W - gz_out - GLA_GATE_RANK, win_out.shape[1]), BF16)
    halves = X_HEAD_DIM // LANE
    for i_ref, o_ref in zip(in_refs, out_refs):
        for hd in range(X_HEADS):
            for c in range(halves):
                src = (hd * halves + c) * LANE
                dst = (c * X_HEADS + hd) * LANE
                o_ref[:, dst:dst + LANE] = i_ref[:, src:src + LANE].astype(BF16)


def prep_weights(w_in_t, others, nblk=8):
    def spec(shape):
        return pl.BlockSpec((None, shape[1] // nblk, shape[2]), lambda i: (0, i, 0))

    def out_spec(rows, cols):
        return pl.BlockSpec((rows // nblk, cols), lambda i: (i, 0))

    width, d = w_in_t.shape
    out_shapes = [jax.ShapeDtypeStruct((PROJ_W, d), BF16)]
    out_specs = [pl.BlockSpec((PROJ_W, d // nblk), lambda i: (0, i))]
    for w in others:
        out_shapes.append(jax.ShapeDtypeStruct(w.shape[1:], BF16))
        out_specs.append(out_spec(*w.shape[1:]))
    return pl.pallas_call(
        _prep_weights_kernel,
        out_shape=tuple(out_shapes),
        grid=(nblk,),
        in_specs=[pl.BlockSpec((width, d // nblk), lambda i: (0, i))] + [spec(w.shape) for w in others],
        out_specs=tuple(out_specs),
        compiler_params=_params(("parallel",)),
        name="prep_weights",
    )(w_in_t, *others)


def _cache_rows(c):
    b, m, h, dh = c.shape
    return c.reshape(b, m, h, dh // LANE, LANE).transpose(0, 1, 3, 2, 4).reshape(b, m * h * (dh // LANE), LANE)


def _mem_kv_output(kv):
    b, m, _, _ = kv.shape
    kv = kv.reshape(b, m, X_HEAD_DIM // LANE, X_HEADS, LANE).transpose(0, 1, 3, 2, 4)
    return kv.reshape(1, b, m, X_HEADS, X_HEAD_DIM)


def kernel(x_prompt, x_sample, mem_prompt, state_gla, state_pool, cache_mem_k, cache_mem_v,
           norm_mix_g, w_in, w_gk_up, b_gk, gla_norm_g, w_pool_mix, pool_scale,
           w_branch_a, w_branch_b, w_out, norm_x_g, norm_mem_g, w_xq, w_xk, w_xv, w_xo,
           norm_mlp_g, w_up, w_down, norm_final_g):
    depth = w_in.shape[0]
    assert depth == 1
    batch, seq, d = x_prompt.shape
    dec_batch, dec_seq, _ = x_sample.shape
    mp = batch * seq
    ms = dec_batch * dec_seq

    w_in_r, wxk, wxv = prep_weights(w_in[0].T, (w_xk, w_xv))
    wgk = jnp.concatenate(
        [w_gk_up[0], jnp.zeros((LANE - GLA_GATE_RANK, GLA_KEY_W), F32)], axis=0).astype(BF16)
    wmix = w_pool_mix[0].astype(BF16)

    xp = x_prompt.reshape(mp, d)
    xs = x_sample.reshape(ms, d)

    (qk_p, v_p, gate_p, u_p, sa_p, sb_p, lah_p, lal_p, wa, wb, wo, wxq, wxo, wu, wd) = in_proj(
        xp, norm_mix_g[0], w_in_r, wgk, b_gk[0], TM_PROJ,
        cast=(w_branch_a, w_branch_b, w_out, w_xq, w_xo, w_up, w_down))

    qk_s, v_s, gate_s, u_s, sa_s, sb_s, lah_s, lal_s = in_proj(xs, norm_mix_g[0], w_in_r, wgk, b_gk[0], ms)
    gla_rows = SAMPLE_SEQS_PER_GLA_STEP * dec_seq
    o_s, sg_s = gla(qk_s, v_s, lah_s, lal_s, gate_s, gla_norm_g[0], state_gla[0],
                    groups=ms // gla_rows, tt=gla_rows, rows=gla_rows, seg=dec_seq)
    d_tm, sp_tm = pool_sample(state_pool[0].transpose(1, 0, 2),
                              u_s.reshape(dec_batch, dec_seq, d).transpose(1, 0, 2),
                              PAST_LEN, SAMPLE_SEQS_PER_POOL_STEP)
    h_s = mix_out(o_s, d_tm.transpose(1, 0, 2).reshape(ms, d), sa_s, sb_s, xs, wmix, pool_scale[0],
                  wa, wb, wo, tm=ms, seq_len=dec_seq, fused_pool=False)
    q_s = norm_matmul(h_s, norm_x_g[0], wxq, tm=ms, tn=d)
    halves = X_HEAD_DIM // LANE
    qhat = q_s.reshape(dec_batch, dec_seq, X_HEADS, halves, LANE).transpose(0, 3, 2, 1, 4)
    qhat = qhat.reshape(dec_batch * XQ_ROWS, LANE)

    o_p, sg_p = gla(qk_p, v_p, lah_p, lal_p, gate_p, gla_norm_g[0], None,
                    groups=batch, tt=TT_GLA, rows=GLA_CHUNK, seg=GLA_CHUNK)
    h_p = mix_out(o_p, u_p, sa_p, sb_p, xp, wmix, pool_scale[0], wa, wb, wo,
                  tm=TM_MIX, seq_len=seq, fused_pool=True)
    h_p, mk_p, mv_p = xattn_prompt(h_p, norm_x_g[0], wxq, mem_prompt, norm_mem_g[0], wxk, wxv, wxo,
                                   tm=TM_XATTN)
    y_p, a_s = mlp_final(h_p, norm_mlp_g[0], wu, wd, norm_final_g, tm=TM_MLP, tf=TF_MLP,
                         attn=(qhat, _cache_rows(cache_mem_k[0]), _cache_rows(cache_mem_v[0])))
    sp_p = u_p.reshape(batch, seq, d)[:, seq - POOL_BUF:]

    a_s = a_s.reshape(dec_batch, halves, X_HEADS, dec_seq, LANE).transpose(0, 3, 2, 1, 4).reshape(ms, d)
    h_s = matmul_residual(a_s, wxo, h_s, tm=ms)
    y_s = mlp_final(h_s, norm_mlp_g[0], wu, wd, norm_final_g, tm=ms, tf=TF_MLP)
    sp_s = sp_tm.transpose(1, 0, 2)

    return (y_p.reshape(batch, seq, d),
            y_s.reshape(dec_batch, dec_seq, d),
            _mem_kv_output(mk_p),
            _mem_kv_output(mv_p),
            sg_p[None],
            sg_s[None],
            sp_p[None],
            sp_s[None])
```

```python
import functools

import jax
import jax.numpy as jnp
from jax import lax
from jax.experimental import pallas as pl
from jax.experimental.pallas import tpu as pltpu

F32 = jnp.float32
BF16 = jnp.bfloat16

D_MODEL = 1024
GLA_HEADS = 4
GLA_DK = 128
GLA_DV = 256
GLA_KEY_W = GLA_HEADS * GLA_DK
GLA_VAL_W = GLA_HEADS * GLA_DV
GLA_GATE_RANK = 16
GLA_GATE_NORM = 16.0
POOL_WINDOWS = (2, 4, 8, 16)
POOL_G = 256
POOL_BUF = 15
MEM_LEN = 256
X_HEADS = 4
X_HEAD_DIM = 256
EPS = 1e-6
PAST_LEN = 16384

LANE = 128
HALO = 16
VMEM_LIMIT = 52 * 1024 * 1024
XQ_ROWS = 32

IN_COLS = {
    "qk": (0, 2 * GLA_KEY_W),
    "v": (1024, GLA_VAL_W),
    "og": (2048, GLA_VAL_W),
    "u": (3072, D_MODEL),
    "ga": (4096, D_MODEL),
    "gb": (5120, D_MODEL),
    "gz": (6144, LANE),
}
PROJ_W = 6272

TM_PROJ = 512
TM_MIX = 512
TM_XATTN = 1024
TM_MLP = 1024
TF_MLP = 1024
TT_GLA = 1024
GLA_CHUNK = 128
SAMPLE_SEQS_PER_GLA_STEP = 16
SAMPLE_SEQS_PER_POOL_STEP = 32


def _params(sem):
    return pltpu.CompilerParams(dimension_semantics=sem, vmem_limit_bytes=VMEM_LIMIT)


def _rmsnorm(x, g):
    return x * lax.rsqrt(jnp.mean(x * x, axis=-1, keepdims=True) + EPS) * g


def _dot(a, b):
    return jnp.dot(a, b, preferred_element_type=F32)


def _dot_nt(a, b):
    return lax.dot_general(a, b, (((1,), (1,)), ((), ())), preferred_element_type=F32)


def _dot_tn(a, b):
    return lax.dot_general(a, b, (((0,), (0,)), ((), ())), preferred_element_type=F32)


def _norm_matmul_kernel(x_ref, g_ref, w_ref, o_ref, xn_ref):
    @pl.when(pl.program_id(1) == 0)
    def _():
        xn_ref[...] = _rmsnorm(x_ref[...], g_ref[...]).astype(BF16)

    o_ref[...] = _dot(xn_ref[...], w_ref[...]).astype(o_ref.dtype)


def norm_matmul(x, g, w, tm, tn, out_dtype=F32):
    m, k = x.shape
    n = w.shape[1]
    return pl.pallas_call(
        _norm_matmul_kernel,
        out_shape=jax.ShapeDtypeStruct((m, n), out_dtype),
        grid=(m // tm, n // tn),
        in_specs=[
            pl.BlockSpec((tm, k), lambda i, j: (i, 0)),
            pl.BlockSpec((1, k), lambda i, j: (0, 0)),
            pl.BlockSpec((k, tn), lambda i, j: (0, j)),
        ],
        out_specs=pl.BlockSpec((tm, tn), lambda i, j: (i, j)),
        scratch_shapes=[pltpu.VMEM((tm, k), BF16)],
        compiler_params=_params(("parallel", "arbitrary")),
        name="norm_matmul",
    )(x, g.reshape(1, k), w)


def _log_decay_split(z):
    la = (jnp.minimum(z, 0.0) - jnp.log(1.0 + jnp.exp(-jnp.abs(z)))) * (1.0 / GLA_GATE_NORM)
    la_hi = la.astype(BF16)
    return la_hi, (la - la_hi.astype(F32)).astype(BF16)


def _in_proj_kernel(*refs, ncast):
    (x_ref, g_ref, w_ref, wgk_ref, bgk_ref) = refs[:5]
    cast_in = refs[5:5 + ncast]
    (qk_ref, v_ref, gate_ref, u_ref, siga_ref, sigb_ref, lahi_ref, lalo_ref) = refs[5 + ncast:13 + ncast]
    cast_out = refs[13 + ncast:]
    xn = _rmsnorm(x_ref[...], g_ref[...]).astype(BF16)

    def piece(name):
        col, width = IN_COLS[name]
        return _dot_nt(xn, w_ref[col:col + width, :])

    og = piece("og")
    gate_ref[...] = (og * jax.nn.sigmoid(og)).astype(BF16)
    siga_ref[...] = jax.nn.sigmoid(piece("ga")).astype(BF16)
    gz = piece("gz").astype(BF16)
    sigb_ref[...] = jax.nn.sigmoid(piece("gb")).astype(BF16)
    z = _dot(gz, wgk_ref[...]) + bgk_ref[...]
    lahi_ref[...], lalo_ref[...] = _log_decay_split(z)
    v_ref[...] = piece("v").astype(BF16)
    qk_ref[...] = piece("qk")
    u_ref[...] = piece("u")
    for i_ref, o_ref in zip(cast_in, cast_out):
        o_ref[...] = i_ref[...].astype(BF16)


def in_proj(x, g, w, wgk, bgk, tm, cast=()):
    m, k = x.shape
    steps = m // tm
    outs = ((2 * GLA_KEY_W, F32), (GLA_VAL_W, BF16), (GLA_VAL_W, BF16), (D_MODEL, F32),
            (D_MODEL, BF16), (D_MODEL, BF16), (GLA_KEY_W, BF16), (GLA_KEY_W, BF16))
    const = lambda shape: pl.BlockSpec(shape, lambda i: (0, 0))
    out_shape = [jax.ShapeDtypeStruct((m, width), dtype) for width, dtype in outs]
    out_specs = [pl.BlockSpec((tm, width), lambda i: (i, 0)) for width, _ in outs]
    in_specs = [pl.BlockSpec((tm, k), lambda i: (i, 0)), const((1, k)), const((PROJ_W, k)),
                const((LANE, GLA_KEY_W)), const((1, GLA_KEY_W))]
    for wc in cast:
        _, rows, cols = wc.shape
        in_specs.append(pl.BlockSpec((None, rows // steps, cols), lambda i: (0, i, 0)))
        out_specs.append(pl.BlockSpec((rows // steps, cols), lambda i: (i, 0)))
        out_shape.append(jax.ShapeDtypeStruct((rows, cols), BF16))
    return pl.pallas_call(
        functools.partial(_in_proj_kernel, ncast=len(cast)),
        out_shape=tuple(out_shape),
        grid=(steps,),
        in_specs=in_specs,
        out_specs=tuple(out_specs),
        compiler_params=_params(("parallel",)),
        name="in_proj",
    )(x, g.reshape(1, k), w, wgk, bgk.reshape(1, GLA_KEY_W), *cast)


def _matmul_res_kernel(x_ref, w_ref, r_ref, o_ref):
    o_ref[...] = _dot(x_ref[...].astype(BF16), w_ref[...]) + r_ref[...]


def matmul_residual(x, w, res, tm):
    m, k = x.shape
    n = w.shape[1]
    return pl.pallas_call(
        _matmul_res_kernel,
        out_shape=jax.ShapeDtypeStruct((m, n), F32),
        grid=(m // tm,),
        in_specs=[
            pl.BlockSpec((tm, k), lambda i: (i, 0)),
            pl.BlockSpec((k, n), lambda i: (0, 0)),
            pl.BlockSpec((tm, n), lambda i: (i, 0)),
        ],
        out_specs=pl.BlockSpec((tm, n), lambda i: (i, 0)),
        compiler_params=_params(("parallel",)),
        name="matmul_residual",
    )(x, w, res)


def _gla_kernel(q_ref, k_ref, v_ref, lahi_ref, lalo_ref, gate_ref, gn_ref, s0_ref,
                o_ref, s_ref, *, rows, seg, zero_init):
    tt = q_ref.shape[0]
    nchunks = tt // rows
    nseg = rows // seg
    seg_shift = seg.bit_length() - 1

    @pl.when(pl.program_id(1) == 0)
    def _():
        if zero_init:
            s_ref[...] = jnp.zeros(s_ref.shape, F32)
        else:
            s_ref[...] = s0_ref[...]

    ri = lax.broadcasted_iota(jnp.int32, (rows, rows), 0)
    ci = lax.broadcasted_iota(jnp.int32, (rows, rows), 1)
    same_seg = (ri >> seg_shift) == (ci >> seg_shift)
    causal = jnp.logical_and(same_seg, ci <= ri)
    l_cum = jnp.where(causal, 1.0, 0.0).astype(BF16)
    l_seg = jnp.where(same_seg, 1.0, 0.0).astype(BF16)
    row_seg = lax.broadcasted_iota(jnp.int32, (rows, GLA_DK), 0) >> seg_shift
    row_seg_v = lax.broadcasted_iota(jnp.int32, (rows, GLA_DV), 0) >> seg_shift
    gn = gn_ref[...]
    qscale = GLA_DK ** -0.5

    def cumulative(c):
        rsl = slice(c * rows, (c + 1) * rows)
        lah = lahi_ref[rsl, :]
        lal = lalo_ref[rsl, :]
        b = _dot(l_cum, lah) + _dot(l_cum, lal)
        if nseg == 1:
            b_end = b[rows - 1:rows, :]
        else:
            b_end = _dot(l_seg, lah) + _dot(l_seg, lal)
        return lah, lal, b, b_end

    ahead = cumulative(0)
    for c in range(nchunks):
        rsl = slice(c * rows, (c + 1) * rows)
        lah, lal, b, b_end = ahead
        if c + 1 < nchunks:
            ahead = cumulative(c + 1)
        q = q_ref[rsl, :]
        k = k_ref[rsl, :]
        qt_b = ((q * qscale) * jnp.exp(b)).astype(BF16)
        kt_b = (k * jnp.exp(-b)).astype(BF16)
        kd = k * jnp.exp(b_end - b)
        heads = []
        for h in range(GLA_HEADS):
            ks = slice(h * GLA_DK, (h + 1) * GLA_DK)
            v_h = v_ref[rsl, h * GLA_DV:(h + 1) * GLA_DV]
            scores = _dot_nt(qt_b[:, ks], kt_b[:, ks])
            inter = None
            new_states = []
            for j in range(nseg):
                s_old = s_ref[j, h]
                inter_j = _dot(qt_b[:, ks], s_old.astype(BF16))
                if nseg == 1:
                    inter = inter_j
                    kd_j = kd[:, ks].astype(BF16)
                    dcol = jnp.broadcast_to(b_end[:, ks], (GLA_DK, GLA_DK)).T
                else:
                    inter_j = jnp.where(row_seg_v == j, inter_j, 0.0)
                    inter = inter_j if inter is None else inter + inter_j
                    kd_j = jnp.where(row_seg == j, kd[:, ks], 0.0).astype(BF16)
                    ones_j = jnp.where(row_seg == j, 1.0, 0.0).astype(BF16)
                    dcol = _dot_tn(lah[:, ks], ones_j) + _dot_tn(lal[:, ks], ones_j)
                new_states.append((s_old, dcol, _dot_tn(kd_j, v_h)))
            heads.append((v_h, scores, inter, new_states))
        outs = []
        for v_h, scores, inter, _ in heads:
            a = jnp.where(causal, scores, 0.0).astype(BF16)
            outs.append(_dot(a, v_h) + inter)
        for h, (_, _, _, new_states) in enumerate(heads):
            vs = slice(h * GLA_DV, (h + 1) * GLA_DV)
            for j, (s_old, dcol, upd) in enumerate(new_states):
                e = jnp.exp(dcol)
                s_ref[j, h] = s_old * jnp.concatenate([e, e], axis=1) + upd
            on = _rmsnorm(outs[h], gn)
            o_ref[rsl, vs] = (on * gate_ref[rsl, vs].astype(F32)).astype(o_ref.dtype)


def gla(qk, v, la_hi, la_lo, gate, gn, s0, *, groups, tt, rows, seg):
    m = qk.shape[0]
    steps = m // (groups * tt)
    nseg = rows // seg
    zero_init = s0 is None
    if zero_init:
        s0 = jnp.zeros((nseg, GLA_HEADS, 8, LANE), F32)
        s0_spec = pl.BlockSpec((nseg, GLA_HEADS, 8, LANE), lambda g, t: (0, 0, 0, 0))
    else:
        s0_spec = pl.BlockSpec((nseg, GLA_HEADS, GLA_DK, GLA_DV), lambda g, t: (g, 0, 0, 0))

    def row_spec(width, blk=0):
        return pl.BlockSpec((tt, width), lambda g, t: (g * steps + t, blk))

    kern = functools.partial(_gla_kernel, rows=rows, seg=seg, zero_init=zero_init)
    return pl.pallas_call(
        kern,
        out_shape=(jax.ShapeDtypeStruct((m, GLA_VAL_W), BF16),
                   jax.ShapeDtypeStruct((groups * nseg, GLA_HEADS, GLA_DK, GLA_DV), F32)),
        grid=(groups, steps),
        in_specs=[
            row_spec(GLA_KEY_W, 0),
            row_spec(GLA_KEY_W, 1),
            row_spec(GLA_VAL_W),
            row_spec(GLA_KEY_W),
            row_spec(GLA_KEY_W),
            row_spec(GLA_VAL_W),
            pl.BlockSpec((1, GLA_DV), lambda g, t: (0, 0)),
            s0_spec,
        ],
        out_specs=(row_spec(GLA_VAL_W),
                   pl.BlockSpec((nseg, GLA_HEADS, GLA_DK, GLA_DV), lambda g, t: (g, 0, 0, 0))),
        compiler_params=_params(("parallel", "arbitrary")),
        name="gla",
    )(qk, qk, v, la_hi, la_lo, gate, gn.reshape(1, GLA_DV), s0)


def _pool_sample_kernel(buf_ref, u_ref, d_ref, new_ref, *, pos0):
    t_new = u_ref.shape[0]

    def ext(i, cs=slice(None)):
        return buf_ref[i, :, cs] if i < POOL_BUF else u_ref[i - POOL_BUF, :, cs]

    for r in range(POOL_BUF):
        new_ref[r] = ext(r + t_new)
    for t in range(t_new):
        cur = POOL_BUF + t
        for g, w in enumerate(POOL_WINDOWS):
            cs = slice(g * POOL_G, (g + 1) * POOL_G)
            win = ext(cur, cs)
            for j in range(1, w):
                win = win + ext(cur - j, cs)
            cnt = float(min(pos0 + t + 1, w))
            d_ref[t, :, cs] = win * (1.0 / cnt) - ext(cur, cs)


def pool_sample(buf_tm, u_tm, pos0, nb):
    nbuf, nseq, width = buf_tm.shape
    t_new = u_tm.shape[0]
    buf_spec = pl.BlockSpec((nbuf, nb, width), lambda i: (0, i, 0))
    new_spec = pl.BlockSpec((t_new, nb, width), lambda i: (0, i, 0))
    return pl.pallas_call(
        functools.partial(_pool_sample_kernel, pos0=pos0),
        out_shape=(jax.ShapeDtypeStruct(u_tm.shape, F32), jax.ShapeDtypeStruct(buf_tm.shape, F32)),
        grid=(nseq // nb,),
        in_specs=[buf_spec, new_spec],
        out_specs=(new_spec, buf_spec),
        compiler_params=_params(("parallel",)),
        name="pool_sample",
    )(buf_tm, u_tm)


def _pool_diff_group(u_ref, halo_ref, g, tiles_per_seq):
    tm = u_ref.shape[0]
    cs = slice(g * POOL_G, (g + 1) * POOL_G)
    t_in_seq = pl.program_id(0) % tiles_per_seq
    halo = jnp.where(t_in_seq == 0, 0.0, halo_ref[:, cs])
    u = u_ref[:, cs]
    s = jnp.concatenate([halo, u], axis=0)
    shift = 1
    while shift < POOL_WINDOWS[g]:
        s = s + pltpu.roll(s, shift, axis=0)
        shift *= 2
    pos1 = (t_in_seq * tm + 1 + lax.broadcasted_iota(jnp.int32, (tm, 1), 0)).astype(F32)
    inv = 1.0 / jnp.minimum(pos1, float(POOL_WINDOWS[g]))
    return s[HALO:, :] * inv - u


def _mix_out_kernel(*refs, fused_pool, tiles_per_seq):
    ngroups = len(POOL_WINDOWS)
    if fused_pool:
        (o_ref, u_ref, halo_ref, siga_ref, sigb_ref, x_ref, wmix_ref, ps_ref, wa_ref, wb_ref,
         wo_ref, h_ref) = refs
    else:
        (o_ref, d_ref, siga_ref, sigb_ref, x_ref, wmix_ref, ps_ref, wa_ref, wb_ref, wo_ref,
         h_ref) = refs
    o = o_ref[...]
    branch_a, diffs = [], []
    for g in range(ngroups):
        cs = slice(g * POOL_G, (g + 1) * POOL_G)
        branch_a.append(_dot(o, wa_ref[:, cs]))
        if fused_pool:
            diffs.append(_pool_diff_group(u_ref, halo_ref, g, tiles_per_seq))
        else:
            diffs.append(d_ref[:, cs])
    pooled = []
    for g in range(ngroups):
        y = _dot(diffs[g].astype(BF16), wmix_ref[g]) * ps_ref[:, g * POOL_G:(g + 1) * POOL_G]
        pooled.append(y.astype(BF16))
    pooled = jnp.concatenate(pooled, axis=1)
    merged = (siga_ref[...].astype(F32) * jnp.concatenate(branch_a, axis=1)
              + sigb_ref[...].astype(F32) * _dot(pooled, wb_ref[...]))
    h_ref[...] = x_ref[...] + _dot(merged.astype(BF16), wo_ref[...])


def mix_out(o, u_or_d, sig_a, sig_b, x, wmix, pscale, wa, wb, wo, *, tm, seq_len, fused_pool):
    m, wide = x.shape
    row_spec = pl.BlockSpec((tm, wide), lambda i: (i, 0))
    const2 = lambda shape: pl.BlockSpec(shape, lambda i: (0, 0))
    w_specs = [
        pl.BlockSpec((len(POOL_WINDOWS), POOL_G, POOL_G), lambda i: (0, 0, 0)),
        const2((1, wide)),
        const2((GLA_VAL_W, wide)),
        const2((wide, wide)),
        const2((wide, wide)),
    ]
    if fused_pool:
        halo_blk = tm // HALO
        in_specs = [row_spec, row_spec,
                    pl.BlockSpec((HALO, wide), lambda i: (jnp.maximum(i * halo_blk - 1, 0), 0)),
                    row_spec, row_spec, row_spec] + w_specs
        args = (o, u_or_d, u_or_d, sig_a, sig_b, x)
        tiles_per_seq = seq_len // tm
    else:
        in_specs = [row_spec] * 5 + w_specs
        args = (o, u_or_d, sig_a, sig_b, x)
        tiles_per_seq = 1
    kern = functools.partial(_mix_out_kernel, fused_pool=fused_pool, tiles_per_seq=tiles_per_seq)
    return pl.pallas_call(
        kern,
        out_shape=jax.ShapeDtypeStruct((m, wide), F32),
        grid=(m // tm,),
        in_specs=in_specs,
        out_specs=row_spec,
        compiler_params=_params(("parallel",)),
        name="mix_out",
    )(*args, wmix, pscale.reshape(1, wide), wa, wb, wo)


def _softmax_rows(s):
    p = jnp.exp(s - jnp.max(s, axis=-1, keepdims=True))
    return p, 1.0 / jnp.sum(p, axis=-1, keepdims=True)


def _xattn_prompt_kernel(h_ref, g_ref, wq_ref, mem_ref, gm_ref, wk_ref, wv_ref, wo_ref,
                         o_ref, mk_ref, mv_ref, kb_ref, vb_ref, *, tiles):
    groups = 2 * X_HEADS

    @pl.when(pl.program_id(0) % tiles == 0)
    def _():
        mn = _rmsnorm(mem_ref[0], gm_ref[...]).astype(BF16)
        k = _dot(mn, wk_ref[...])
        v = _dot(mn, wv_ref[...])
        kb_ref[...] = k.astype(BF16)
        vb_ref[...] = v.astype(BF16)
        mk_ref[0] = k.reshape(MEM_LEN, groups, LANE)
        mv_ref[0] = v.reshape(MEM_LEN, groups, LANE)

    def head(ref, hd):
        lo = ref[:, hd * LANE:(hd + 1) * LANE]
        hi = ref[:, (X_HEADS + hd) * LANE:(X_HEADS + hd + 1) * LANE]
        return jnp.concatenate([lo, hi], axis=1)

    h = h_ref[...]
    hn = _rmsnorm(h, g_ref[...]).astype(BF16)
    q = (_dot(hn, wq_ref[...]) * (X_HEAD_DIM ** -0.5)).astype(BF16)

    def scores(hd):
        return _dot_nt(q[:, hd * X_HEAD_DIM:(hd + 1) * X_HEAD_DIM], head(kb_ref, hd))

    outs = []
    s_next = scores(0)
    for hd in range(X_HEADS):
        s_cur = s_next
        if hd + 1 < X_HEADS:
            s_next = scores(hd + 1)
        p, inv = _softmax_rows(s_cur)
        outs.append((_dot(p.astype(BF16), head(vb_ref, hd)) * inv).astype(BF16))
    o = jnp.concatenate(outs, axis=1)
    o_ref[...] = h + _dot(o, wo_ref[...])


def xattn_prompt(h, g, wq, mem, gm, wk, wv, wo, *, tm):
    m, d = h.shape
    batch = mem.shape[0]
    tiles = m // batch // tm
    groups = 2 * X_HEADS
    row_spec = pl.BlockSpec((tm, d), lambda i: (i, 0))
    vec_spec = pl.BlockSpec((1, d), lambda i: (0, 0))
    w_spec = pl.BlockSpec((d, d), lambda i: (0, 0))
    kv_spec = pl.BlockSpec((1, MEM_LEN, groups, LANE), lambda i: (i // tiles, 0, 0, 0))
    kv_shape = jax.ShapeDtypeStruct((batch, MEM_LEN, groups, LANE), F32)
    return pl.pallas_call(
        functools.partial(_xattn_prompt_kernel, tiles=tiles),
        out_shape=(jax.ShapeDtypeStruct((m, d), F32), kv_shape, kv_shape),
        grid=(m // tm,),
        in_specs=[row_spec, vec_spec, w_spec,
                  pl.BlockSpec((1, MEM_LEN, d), lambda i: (i // tiles, 0, 0)), vec_spec,
                  w_spec, w_spec, w_spec],
        out_specs=(row_spec, kv_spec, kv_spec),
        scratch_shapes=[pltpu.VMEM((MEM_LEN, d), BF16), pltpu.VMEM((MEM_LEN, d), BF16)],
        compiler_params=_params(("arbitrary",)),
        name="xattn_prompt",
    )(h, g.reshape(1, d), wq, mem, gm.reshape(1, d), wk, wv, wo)


def _sample_attention_probs(q_ref, k_ref):
    nseq, krows, _ = k_ref.shape
    half = XQ_ROWS // 2
    lane = lax.broadcasted_iota(jnp.int32, (half, krows), 1)
    row = lax.broadcasted_iota(jnp.int32, (half, krows), 0)
    valid = (lane & 7) == (row >> 2)
    out = []
    for j in range(nseq):
        q = (q_ref[j * XQ_ROWS:(j + 1) * XQ_ROWS, :] * (X_HEAD_DIM ** -0.5)).astype(BF16)
        g = _dot_nt(q, k_ref[j].astype(BF16))
        s = g[:half] + pltpu.roll(g[half:], krows - 4, axis=1)
        s = jnp.where(valid, s, -1e30)
        p = jnp.exp(s - jnp.max(s, axis=-1, keepdims=True))
        inv = 1.0 / jnp.sum(p, axis=-1, keepdims=True)
        pe = jnp.concatenate([p, pltpu.roll(p, 4, axis=1)], axis=0).astype(BF16)
        out.append((pe, jnp.concatenate([inv, inv], axis=0)))
    return out


def _sample_attention_values(probs, v_ref, o_ref):
    for j, (pe, inv) in enumerate(probs):
        o = _dot(pe, v_ref[j].astype(BF16)) * inv
        o_ref[j * XQ_ROWS:(j + 1) * XQ_ROWS, :] = o.astype(o_ref.dtype)


def _mlp_kernel(*refs, with_attn):
    if with_attn:
        (h_ref, g_ref, wu_ref, wd_ref, gf_ref, q_ref, k_ref, v_ref,
         y_ref, a_ref, hn_ref, acc_ref) = refs
    else:
        h_ref, g_ref, wu_ref, wd_ref, gf_ref, y_ref, hn_ref, acc_ref = refs
    f = pl.program_id(1)

    @pl.when(f == 0)
    def _():
        h = h_ref[...]
        hn_ref[...] = _rmsnorm(h, g_ref[...]).astype(BF16)
        acc_ref[...] = h

    if with_attn:
        probs = _sample_attention_probs(q_ref, k_ref)
    a = jnp.maximum(_dot(hn_ref[...], wu_ref[...]), 0.0)
    if with_attn:
        _sample_attention_values(probs, v_ref, a_ref)
    acc_ref[...] += _dot((a * a).astype(BF16), wd_ref[...])

    @pl.when(f == pl.num_programs(1) - 1)
    def _():
        y_ref[...] = _rmsnorm(acc_ref[...], gf_ref[...])


def mlp_final(h, g, wu, wd, gf, *, tm, tf, attn=None):
    m, d = h.shape
    ff = wu.shape[1]
    nf = ff // tf
    row_spec = pl.BlockSpec((tm, d), lambda i, f: (i, 0))
    vec_spec = pl.BlockSpec((1, d), lambda i, f: (0, 0))
    in_specs = [row_spec, vec_spec,
                pl.BlockSpec((d, tf), lambda i, f: (0, f)),
                pl.BlockSpec((tf, d), lambda i, f: (f, 0)),
                vec_spec]
    args = [h, g.reshape(1, d), wu, wd, gf.reshape(1, d)]
    out_shape = jax.ShapeDtypeStruct((m, d), F32)
    out_specs = row_spec
    if attn is not None:
        qhat, ck, cv = attn
        steps = (m // tm) * nf
        nseq = ck.shape[0] // steps
        assert nseq * steps == ck.shape[0]
        q_spec = pl.BlockSpec((nseq * XQ_ROWS, LANE), lambda i, f: (i * nf + f, 0))
        kv_spec = pl.BlockSpec((nseq,) + ck.shape[1:], lambda i, f: (i * nf + f, 0, 0))
        in_specs += [q_spec, kv_spec, kv_spec]
        args += [qhat, ck, cv]
        out_shape = (out_shape, jax.ShapeDtypeStruct(qhat.shape, BF16))
        out_specs = (row_spec, q_spec)
    return pl.pallas_call(
        functools.partial(_mlp_kernel, with_attn=attn is not None),
        out_shape=out_shape,
        grid=(m // tm, nf),
        in_specs=in_specs,
        out_specs=out_specs,
        scratch_shapes=[pltpu.VMEM((tm, d), BF16), pltpu.VMEM((tm, d), F32)],
        compiler_params=_params(("parallel", "arbitrary")),
        name="mlp_final",
    )(*args)


def _prep_weights_kernel(win_ref, *refs):
    n = (len(refs) - 1) // 2
    in_refs, win_out, out_refs = refs[:n], refs[n], refs[n + 1:]
    gz_lo = 2 * GLA_KEY_W + GLA_VAL_W
    gz_hi = gz_lo + GLA_GATE_RANK
    gz_out = IN_COLS["gz"][0]
    win_out[:gz_lo, :] = win_ref[:gz_lo, :].astype(BF16)
    win_out[gz_lo:gz_out, :] = win_ref[gz_hi:, :].astype(BF16)
    win_out[gz_out:gz_out + GLA_GATE_RANK, :] = win_ref[gz_lo:gz_hi, :].astype(BF16)
    win_out[gz_out + GLA_GATE_RANK:, :] = jnp.zeros(
        (PROJ_W - gz_out - GLA_GATE_RANK, win_out.shape[1]), BF16)
    halves = X_HEAD_DIM // LANE
    for i_ref, o_ref in zip(in_refs, out_refs):
        for hd in range(X_HEADS):
            for c in range(halves):
                src = (hd * halves + c) * LANE
                dst = (c * X_HEADS + hd) * LANE
                o_ref[:, dst:dst + LANE] = i_ref[:, src:src + LANE].astype(BF16)


def prep_weights(w_in_t, others, nblk=8):
    def spec(shape):
        return pl.BlockSpec((None, shape[1] // nblk, shape[2]), lambda i: (0, i, 0))

    def out_spec(rows, cols):
        return pl.BlockSpec((rows // nblk, cols), lambda i: (i, 0))

    width, d = w_in_t.shape
    out_shapes = [jax.ShapeDtypeStruct((PROJ_W, d), BF16)]
    out_specs = [pl.BlockSpec((PROJ_W, d // nblk), lambda i: (0, i))]
    for w in others:
        out_shapes.append(jax.ShapeDtypeStruct(w.shape[1:], BF16))
        out_specs.append(out_spec(*w.shape[1:]))
    return pl.pallas_call(
        _prep_weights_kernel,
        out_shape=tuple(out_shapes),
        grid=(nblk,),
        in_specs=[pl.BlockSpec((width, d // nblk), lambda i: (0, i))] + [spec(w.shape) for w in others],
        out_specs=tuple(out_specs),
        compiler_params=_params(("parallel",)),
        name="prep_weights",
    )(w_in_t, *others)


def _cache_rows(c):
    b, m, h, dh = c.shape
    return c.reshape(b, m, h, dh // LANE, LANE).transpose(0, 1, 3, 2, 4).reshape(b, m * h * (dh // LANE), LANE)


def _mem_kv_output(kv):
    b, m, _, _ = kv.shape
    kv = kv.reshape(b, m, X_HEAD_DIM // LANE, X_HEADS, LANE).transpose(0, 1, 3, 2, 4)
    return kv.reshape(1, b, m, X_HEADS, X_HEAD_DIM)


def kernel(x_prompt, x_sample, mem_prompt, state_gla, state_pool, cache_mem_k, cache_mem_v,
           norm_mix_g, w_in, w_gk_up, b_gk, gla_norm_g, w_pool_mix, pool_scale,
           w_branch_a, w_branch_b, w_out, norm_x_g, norm_mem_g, w_xq, w_xk, w_xv, w_xo,
           norm_mlp_g, w_up, w_down, norm_final_g):
    depth = w_in.shape[0]
    assert depth == 1
    batch, seq, d = x_prompt.shape
    dec_batch, dec_seq, _ = x_sample.shape
    mp = batch * seq
    ms = dec_batch * dec_seq

    w_in_r, wxk, wxv = prep_weights(w_in[0].T, (w_xk, w_xv))
    wgk = jnp.concatenate(
        [w_gk_up[0], jnp.zeros((LANE - GLA_GATE_RANK, GLA_KEY_W), F32)], axis=0).astype(BF16)
    wmix = w_pool_mix[0].astype(BF16)

    xp = x_prompt.reshape(mp, d)
    xs = x_sample.reshape(ms, d)

    (qk_p, v_p, gate_p, u_p, sa_p, sb_p, lah_p, lal_p, wa, wb, wo, wxq, wxo, wu, wd) = in_proj(
        xp, norm_mix_g[0], w_in_r, wgk, b_gk[0], TM_PROJ,
        cast=(w_branch_a, w_branch_b, w_out, w_xq, w_xo, w_up, w_down))

    qk_s, v_s, gate_s, u_s, sa_s, sb_s, lah_s, lal_s = in_proj(xs, norm_mix_g[0], w_in_r, wgk, b_gk[0], ms)
    gla_rows = SAMPLE_SEQS_PER_GLA_STEP * dec_seq
    o_s, sg_s = gla(qk_s, v_s, lah_s, lal_s, gate_s, gla_norm_g[0], state_gla[0],
                    groups=ms // gla_rows, tt=gla_rows, rows=gla_rows, seg=dec_seq)
    d_tm, sp_tm = pool_sample(state_pool[0].transpose(1, 0, 2),
                              u_s.reshape(dec_batch, dec_seq, d).transpose(1, 0, 2),
                              PAST_LEN, SAMPLE_SEQS_PER_POOL_STEP)
    h_s = mix_out(o_s, d_tm.transpose(1, 0, 2).reshape(ms, d), sa_s, sb_s, xs, wmix, pool_scale[0],
                  wa, wb, wo, tm=ms, seq_len=dec_seq, fused_pool=False)
    q_s = norm_matmul(h_s, norm_x_g[0], wxq, tm=ms, tn=d)
    halves = X_HEAD_DIM // LANE
    qhat = q_s.reshape(dec_batch, dec_seq, X_HEADS, halves, LANE).transpose(0, 3, 2, 1, 4)
    qhat = qhat.reshape(dec_batch * XQ_ROWS, LANE)

    o_p, sg_p = gla(qk_p, v_p, lah_p, lal_p, gate_p, gla_norm_g[0], None,
                    groups=batch, tt=TT_GLA, rows=GLA_CHUNK, seg=GLA_CHUNK)
    h_p = mix_out(o_p, u_p, sa_p, sb_p, xp, wmix, pool_scale[0], wa, wb, wo,
                  tm=TM_MIX, seq_len=seq, fused_pool=True)
    h_p, mk_p, mv_p = xattn_prompt(h_p, norm_x_g[0], wxq, mem_prompt, norm_mem_g[0], wxk, wxv, wxo,
                                   tm=TM_XATTN)
    y_p, a_s = mlp_final(h_p, norm_mlp_g[0], wu, wd, norm_final_g, tm=TM_MLP, tf=TF_MLP,
                         attn=(qhat, _cache_rows(cache_mem_k[0]), _cache_rows(cache_mem_v[0])))
    sp_p = u_p.reshape(batch, seq, d)[:, seq - POOL_BUF:]

    a_s = a_s.reshape(dec_batch, halves, X_HEADS, dec_seq, LANE).transpose(0, 3, 2, 1, 4).reshape(ms, d)
    h_s = matmul_residual(a_s, wxo, h_s, tm=ms)
    y_s = mlp_final(h_s, norm_mlp_g[0], wu, wd, norm_final_g, tm=ms, tf=TF_MLP)
    sp_s = sp_tm.transpose(1, 0, 2)

    return (y_p.reshape(batch, seq, d),
            y_s.reshape(dec_batch, dec_seq, d),
            _mem_kv_output(mk_p),
            _mem_kv_output(mv_p),
            sg_p[None],
            sg_s[None],
            sp_p[None],
            sp_s[None])
```

```python
import functools

import jax
import jax.numpy as jnp
from jax import lax
from jax.experimental import pallas as pl
from jax.experimental.pallas import tpu as pltpu

F32 = jnp.float32
BF16 = jnp.bfloat16

D_MODEL = 1024
GLA_HEADS = 4
GLA_DK = 128
GLA_DV = 256
GLA_KEY_W = GLA_HEADS * GLA_DK
GLA_VAL_W = GLA_HEADS * GLA_DV
GLA_GATE_RANK = 16
GLA_GATE_NORM = 16.0
POOL_WINDOWS = (2, 4, 8, 16)
POOL_G = 256
POOL_BUF = 15
MEM_LEN = 256
X_HEADS = 4
X_HEAD_DIM = 256
EPS = 1e-6
PAST_LEN = 16384

LANE = 128
HALO = 16
VMEM_LIMIT = 52 * 1024 * 1024
XQ_ROWS = 32

IN_COLS = {
    "qk": (0, 2 * GLA_KEY_W),
    "v": (1024, GLA_VAL_W),
    "og": (2048, GLA_VAL_W),
    "u": (3072, D_MODEL),
    "ga": (4096, D_MODEL),
    "gb": (5120, D_MODEL),
    "gz": (6144, LANE),
}
PROJ_W = 6272

TM_PROJ = 512
TM_MIX = 512
TM_XATTN = 1024
TM_MLP = 1024
TF_MLP = 1024
TT_GLA = 1024
GLA_CHUNK = 128
GLA_FAST_RANGE = 60.0
SAMPLE_SEQS_PER_GLA_STEP = 8
SAMPLE_SEQS_PER_POOL_STEP = 32


def _params(sem):
    return pltpu.CompilerParams(dimension_semantics=sem, vmem_limit_bytes=VMEM_LIMIT)


def _rmsnorm(x, g):
    return x * lax.rsqrt(jnp.mean(x * x, axis=-1, keepdims=True) + EPS) * g


def _dot(a, b):
    return jnp.dot(a, b, preferred_element_type=F32)


def _dot_nt(a, b):
    return lax.dot_general(a, b, (((1,), (1,)), ((), ())), preferred_element_type=F32)


def _dot_tn(a, b):
    return lax.dot_general(a, b, (((0,), (0,)), ((), ())), preferred_element_type=F32)


def _norm_matmul_kernel(x_ref, g_ref, w_ref, o_ref, xn_ref):
    @pl.when(pl.program_id(1) == 0)
    def _():
        xn_ref[...] = _rmsnorm(x_ref[...], g_ref[...]).astype(BF16)

    o_ref[...] = _dot(xn_ref[...], w_ref[...]).astype(o_ref.dtype)


def norm_matmul(x, g, w, tm, tn, out_dtype=F32):
    m, k = x.shape
    n = w.shape[1]
    return pl.pallas_call(
        _norm_matmul_kernel,
        out_shape=jax.ShapeDtypeStruct((m, n), out_dtype),
        grid=(m // tm, n // tn),
        in_specs=[
            pl.BlockSpec((tm, k), lambda i, j: (i, 0)),
            pl.BlockSpec((1, k), lambda i, j: (0, 0)),
            pl.BlockSpec((k, tn), lambda i, j: (0, j)),
        ],
        out_specs=pl.BlockSpec((tm, tn), lambda i, j: (i, j)),
        scratch_shapes=[pltpu.VMEM((tm, k), BF16)],
        compiler_params=_params(("parallel", "arbitrary")),
        name="norm_matmul",
    )(x, g.reshape(1, k), w)


def _log_decay_split(z):
    la = (jnp.minimum(z, 0.0) - jnp.log(1.0 + jnp.exp(-jnp.abs(z)))) * (1.0 / GLA_GATE_NORM)
    la_hi = la.astype(BF16)
    return la_hi, (la - la_hi.astype(F32)).astype(BF16)


def _in_proj_kernel(*refs, ncast):
    (x_ref, g_ref, w_ref, wgk_ref, bgk_ref) = refs[:5]
    cast_in = refs[5:5 + ncast]
    (qk_ref, v_ref, gate_ref, u_ref, siga_ref, sigb_ref, lahi_ref, lalo_ref) = refs[5 + ncast:13 + ncast]
    cast_out = refs[13 + ncast:]
    xn = _rmsnorm(x_ref[...], g_ref[...]).astype(BF16)

    def piece(name):
        col, width = IN_COLS[name]
        return _dot_nt(xn, w_ref[col:col + width, :])

    og = piece("og")
    gate_ref[...] = (og * jax.nn.sigmoid(og)).astype(BF16)
    siga_ref[...] = jax.nn.sigmoid(piece("ga")).astype(BF16)
    gz = piece("gz").astype(BF16)
    sigb_ref[...] = jax.nn.sigmoid(piece("gb")).astype(BF16)
    z = _dot(gz, wgk_ref[...]) + bgk_ref[...]
    lahi_ref[...], lalo_ref[...] = _log_decay_split(z)
    v_ref[...] = piece("v").astype(BF16)
    qk_ref[...] = piece("qk")
    u_ref[...] = piece("u")
    for i_ref, o_ref in zip(cast_in, cast_out):
        o_ref[...] = i_ref[...].astype(BF16)


def in_proj(x, g, w, wgk, bgk, tm, cast=()):
    m, k = x.shape
    steps = m // tm
    outs = ((2 * GLA_KEY_W, F32), (GLA_VAL_W, BF16), (GLA_VAL_W, BF16), (D_MODEL, F32),
            (D_MODEL, BF16), (D_MODEL, BF16), (GLA_KEY_W, BF16), (GLA_KEY_W, BF16))
    const = lambda shape: pl.BlockSpec(shape, lambda i: (0, 0))
    out_shape = [jax.ShapeDtypeStruct((m, width), dtype) for width, dtype in outs]
    out_specs = [pl.BlockSpec((tm, width), lambda i: (i, 0)) for width, _ in outs]
    in_specs = [pl.BlockSpec((tm, k), lambda i: (i, 0)), const((1, k)), const((PROJ_W, k)),
                const((LANE, GLA_KEY_W)), const((1, GLA_KEY_W))]
    for wc in cast:
        _, rows, cols = wc.shape
        in_specs.append(pl.BlockSpec((None, rows // steps, cols), lambda i: (0, i, 0)))
        out_specs.append(pl.BlockSpec((rows // steps, cols), lambda i: (i, 0)))
        out_shape.append(jax.ShapeDtypeStruct((rows, cols), BF16))
    return pl.pallas_call(
        functools.partial(_in_proj_kernel, ncast=len(cast)),
        out_shape=tuple(out_shape),
        grid=(steps,),
        in_specs=in_specs,
        out_specs=tuple(out_specs),
        compiler_params=_params(("parallel",)),
        name="in_proj",
    )(x, g.reshape(1, k), w, wgk, bgk.reshape(1, GLA_KEY_W), *cast)


def _matmul_res_kernel(x_ref, w_ref, r_ref, o_ref):
    o_ref[...] = _dot(x_ref[...].astype(BF16), w_ref[...]) + r_ref[...]


def matmul_residual(x, w, res, tm):
    m, k = x.shape
    n = w.shape[1]
    return pl.pallas_call(
        _matmul_res_kernel,
        out_shape=jax.ShapeDtypeStruct((m, n), F32),
        grid=(m // tm,),
        in_specs=[
            pl.BlockSpec((tm, k), lambda i: (i, 0)),
            pl.BlockSpec((k, n), lambda i: (0, 0)),
            pl.BlockSpec((tm, n), lambda i: (i, 0)),
        ],
        out_specs=pl.BlockSpec((tm, n), lambda i: (i, 0)),
        compiler_params=_params(("parallel",)),
        name="matmul_residual",
    )(x, w, res)


def _gla_kernel(q_ref, k_ref, v_ref, lahi_ref, lalo_ref, gate_ref, gn_ref, s0_ref,
                o_ref, s_ref, s_in_ref, b_ref, *, rows, seg, zero_init):
    tt = q_ref.shape[0]
    nchunks = tt // rows
    nseg = rows // seg
    seg_shift = seg.bit_length() - 1

    @pl.when(pl.program_id(1) == 0)
    def _():
        if zero_init:
            s_ref[...] = jnp.zeros(s_ref.shape, F32)
        else:
            s_ref[...] = s0_ref[...]

    s_in_ref[...] = s_ref[...]

    ri = lax.broadcasted_iota(jnp.int32, (rows, rows), 0)
    ci = lax.broadcasted_iota(jnp.int32, (rows, rows), 1)
    same_seg = (ri >> seg_shift) == (ci >> seg_shift)
    causal = jnp.logical_and(same_seg, ci <= ri)
    l_cum = jnp.where(causal, 1.0, 0.0).astype(BF16)
    l_seg = jnp.where(same_seg, 1.0, 0.0).astype(BF16)
    row_seg = lax.broadcasted_iota(jnp.int32, (rows, GLA_DK), 0) >> seg_shift
    row_seg_v = lax.broadcasted_iota(jnp.int32, (rows, GLA_DV), 0) >> seg_shift
    gn = gn_ref[...]
    qscale = GLA_DK ** -0.5

    def cumulative(c):
        rsl = slice(c * rows, (c + 1) * rows)
        lah = lahi_ref[rsl, :]
        lal = lalo_ref[rsl, :]
        b = _dot(l_cum, lah) + _dot(l_cum, lal)
        if nseg == 1:
            b_end = b[rows - 1:rows, :]
        else:
            b_end = _dot(l_seg, lah) + _dot(l_seg, lal)
        return lah, lal, b, b_end

    def pairwise_scores(c, ks, qs_h, b_h):
        sub = 8

        def columns(blk, acc):
            r0 = pl.multiple_of(blk * sub, sub)
            b_blk = b_ref[pl.ds(r0, sub), ks]
            k_blk = k_ref[pl.ds(c * rows + r0, sub), ks]
            for j in range(sub):
                e = jnp.exp(jnp.minimum(b_h - b_blk[j:j + 1, :], 0.0))
                col = jnp.sum(qs_h * e * k_blk[j:j + 1, :], axis=1, keepdims=True)
                acc = acc + jnp.where(ci == r0 + j, col, 0.0)
            return acc

        return lax.fori_loop(0, rows // sub, columns, jnp.zeros((rows, rows), F32))

    def run(exact):
        worst = jnp.float32(0.0)
        ahead = cumulative(0)
        for c in range(nchunks):
            rsl = slice(c * rows, (c + 1) * rows)
            lah, lal, b, b_end = ahead
            if c + 1 < nchunks:
                ahead = cumulative(c + 1)
            qs = q_ref[rsl, :] * qscale
            k = k_ref[rsl, :]
            qt_b = (qs * jnp.exp(b)).astype(BF16)
            kd = k * jnp.exp(b_end - b)
            if exact:
                b_ref[...] = b
            else:
                kt_b = (k * jnp.exp(-b)).astype(BF16)
                worst = jnp.maximum(worst, jnp.max(-b_end))
            heads = []
            for h in range(GLA_HEADS):
                ks = slice(h * GLA_DK, (h + 1) * GLA_DK)
                v_h = v_ref[rsl, h * GLA_DV:(h + 1) * GLA_DV]
                if exact:
                    scores = pairwise_scores(c, ks, qs[:, ks], b[:, ks])
                else:
                    scores = _dot_nt(qt_b[:, ks], kt_b[:, ks])
                inter = None
                new_states = []
                for j in range(nseg):
                    s_old = s_ref[j, h]
                    inter_j = _dot(qt_b[:, ks], s_old.astype(BF16))
                    if nseg == 1:
                        inter = inter_j
                        kd_j = kd[:, ks].astype(BF16)
                        dcol = jnp.broadcast_to(b_end[:, ks], (GLA_DK, GLA_DK)).T
                    else:
                        inter_j = jnp.where(row_seg_v == j, inter_j, 0.0)
                        inter = inter_j if inter is None else inter + inter_j
                        kd_j = jnp.where(row_seg == j, kd[:, ks], 0.0).astype(BF16)
                        ones_j = jnp.where(row_seg == j, 1.0, 0.0).astype(BF16)
                        dcol = _dot_tn(lah[:, ks], ones_j) + _dot_tn(lal[:, ks], ones_j)
                    new_states.append((s_old, dcol, _dot_tn(kd_j, v_h)))
                heads.append((v_h, scores, inter, new_states))
            outs = []
            for v_h, scores, inter, _ in heads:
                a = jnp.where(causal, scores, 0.0).astype(BF16)
                outs.append(_dot(a, v_h) + inter)
            for h, (_, _, _, new_states) in enumerate(heads):
                vs = slice(h * GLA_DV, (h + 1) * GLA_DV)
                for j, (s_old, dcol, upd) in enumerate(new_states):
                    e = jnp.exp(dcol)
                    s_ref[j, h] = s_old * jnp.concatenate([e, e], axis=1) + upd
                on = _rmsnorm(outs[h], gn)
                o_ref[rsl, vs] = (on * gate_ref[rsl, vs].astype(F32)).astype(o_ref.dtype)
        return worst

    worst = run(exact=False)

    @pl.when(worst > GLA_FAST_RANGE)
    def _():
        s_ref[...] = s_in_ref[...]
        run(exact=True)


def gla(qk, v, la_hi, la_lo, gate, gn, s0, *, groups, tt, rows, seg):
    m = qk.shape[0]
    steps = m // (groups * tt)
    nseg = rows // seg
    zero_init = s0 is None
    if zero_init:
        s0 = jnp.zeros((nseg, GLA_HEADS, 8, LANE), F32)
        s0_spec = pl.BlockSpec((nseg, GLA_HEADS, 8, LANE), lambda g, t: (0, 0, 0, 0))
    else:
        s0_spec = pl.BlockSpec((nseg, GLA_HEADS, GLA_DK, GLA_DV), lambda g, t: (g, 0, 0, 0))

    def row_spec(width, blk=0):
        return pl.BlockSpec((tt, width), lambda g, t: (g * steps + t, blk))

    kern = functools.partial(_gla_kernel, rows=rows, seg=seg, zero_init=zero_init)
    return pl.pallas_call(
        kern,
        out_shape=(jax.ShapeDtypeStruct((m, GLA_VAL_W), BF16),
                   jax.ShapeDtypeStruct((groups * nseg, GLA_HEADS, GLA_DK, GLA_DV), F32)),
        grid=(groups, steps),
        in_specs=[
            row_spec(GLA_KEY_W, 0),
            row_spec(GLA_KEY_W, 1),
            row_spec(GLA_VAL_W),
            row_spec(GLA_KEY_W),
            row_spec(GLA_KEY_W),
            row_spec(GLA_VAL_W),
            pl.BlockSpec((1, GLA_DV), lambda g, t: (0, 0)),
            s0_spec,
        ],
        out_specs=(row_spec(GLA_VAL_W),
                   pl.BlockSpec((nseg, GLA_HEADS, GLA_DK, GLA_DV), lambda g, t: (g, 0, 0, 0))),
        scratch_shapes=[pltpu.VMEM((nseg, GLA_HEADS, GLA_DK, GLA_DV), F32),
                        pltpu.VMEM((rows, GLA_KEY_W), F32)],
        compiler_params=_params(("parallel", "arbitrary")),
        name="gla",
    )(qk, qk, v, la_hi, la_lo, gate, gn.reshape(1, GLA_DV), s0)


def _pool_sample_kernel(buf_ref, u_ref, d_ref, new_ref, *, pos0):
    t_new = u_ref.shape[0]

    def ext(i, cs=slice(None)):
        return buf_ref[i, :, cs] if i < POOL_BUF else u_ref[i - POOL_BUF, :, cs]

    for r in range(POOL_BUF):
        new_ref[r] = ext(r + t_new)
    for t in range(t_new):
        cur = POOL_BUF + t
        for g, w in enumerate(POOL_WINDOWS):
            cs = slice(g * POOL_G, (g + 1) * POOL_G)
            win = ext(cur, cs)
            for j in range(1, w):
                win = win + ext(cur - j, cs)
            cnt = float(min(pos0 + t + 1, w))
            d_ref[t, :, cs] = win * (1.0 / cnt) - ext(cur, cs)


def pool_sample(buf_tm, u_tm, pos0, nb):
    nbuf, nseq, width = buf_tm.shape
    t_new = u_tm.shape[0]
    buf_spec = pl.BlockSpec((nbuf, nb, width), lambda i: (0, i, 0))
    new_spec = pl.BlockSpec((t_new, nb, width), lambda i: (0, i, 0))
    return pl.pallas_call(
        functools.partial(_pool_sample_kernel, pos0=pos0),
        out_shape=(jax.ShapeDtypeStruct(u_tm.shape, F32), jax.ShapeDtypeStruct(buf_tm.shape, F32)),
        grid=(nseq // nb,),
        in_specs=[buf_spec, new_spec],
        out_specs=(new_spec, buf_spec),
        compiler_params=_params(("parallel",)),
        name="pool_sample",
    )(buf_tm, u_tm)


def _pool_diff_group(u_ref, halo_ref, g, tiles_per_seq):
    tm = u_ref.shape[0]
    cs = slice(g * POOL_G, (g + 1) * POOL_G)
    t_in_seq = pl.program_id(0) % tiles_per_seq
    halo = jnp.where(t_in_seq == 0, 0.0, halo_ref[:, cs])
    u = u_ref[:, cs]
    s = jnp.concatenate([halo, u], axis=0)
    shift = 1
    while shift < POOL_WINDOWS[g]:
        s = s + pltpu.roll(s, shift, axis=0)
        shift *= 2
    pos1 = (t_in_seq * tm + 1 + lax.broadcasted_iota(jnp.int32, (tm, 1), 0)).astype(F32)
    inv = 1.0 / jnp.minimum(pos1, float(POOL_WINDOWS[g]))
    return s[HALO:, :] * inv - u


def _mix_out_kernel(*refs, fused_pool, tiles_per_seq):
    ngroups = len(POOL_WINDOWS)
    if fused_pool:
        (o_ref, u_ref, halo_ref, siga_ref, sigb_ref, x_ref, wmix_ref, ps_ref, wa_ref, wb_ref,
         wo_ref, h_ref) = refs
    else:
        (o_ref, d_ref, siga_ref, sigb_ref, x_ref, wmix_ref, ps_ref, wa_ref, wb_ref, wo_ref,
         h_ref) = refs
    o = o_ref[...]
    branch_a, diffs = [], []
    for g in range(ngroups):
        cs = slice(g * POOL_G, (g + 1) * POOL_G)
        branch_a.append(_dot(o, wa_ref[:, cs]))
        if fused_pool:
            diffs.append(_pool_diff_group(u_ref, halo_ref, g, tiles_per_seq))
        else:
            diffs.append(d_ref[:, cs])
    pooled = []
    for g in range(ngroups):
        y = _dot(diffs[g].astype(BF16), wmix_ref[g]) * ps_ref[:, g * POOL_G:(g + 1) * POOL_G]
        pooled.append(y.astype(BF16))
    pooled = jnp.concatenate(pooled, axis=1)
    merged = (siga_ref[...].astype(F32) * jnp.concatenate(branch_a, axis=1)
              + sigb_ref[...].astype(F32) * _dot(pooled, wb_ref[...]))
    h_ref[...] = x_ref[...] + _dot(merged.astype(BF16), wo_ref[...])


def mix_out(o, u_or_d, sig_a, sig_b, x, wmix, pscale, wa, wb, wo, *, tm, seq_len, fused_pool):
    m, wide = x.shape
    row_spec = pl.BlockSpec((tm, wide), lambda i: (i, 0))
    const2 = lambda shape: pl.BlockSpec(shape, lambda i: (0, 0))
    w_specs = [
        pl.BlockSpec((len(POOL_WINDOWS), POOL_G, POOL_G), lambda i: (0, 0, 0)),
        const2((1, wide)),
        const2((GLA_VAL_W, wide)),
        const2((wide, wide)),
        const2((wide, wide)),
    ]
    if fused_pool:
        halo_blk = tm // HALO
        in_specs = [row_spec, row_spec,
                    pl.BlockSpec((HALO, wide), lambda i: (jnp.maximum(i * halo_blk - 1, 0), 0)),
                    row_spec, row_spec, row_spec] + w_specs
        args = (o, u_or_d, u_or_d, sig_a, sig_b, x)
        tiles_per_seq = seq_len // tm
    else:
        in_specs = [row_spec] * 5 + w_specs
        args = (o, u_or_d, sig_a, sig_b, x)
        tiles_per_seq = 1
    kern = functools.partial(_mix_out_kernel, fused_pool=fused_pool, tiles_per_seq=tiles_per_seq)
    return pl.pallas_call(
        kern,
        out_shape=jax.ShapeDtypeStruct((m, wide), F32),
        grid=(m // tm,),
        in_specs=in_specs,
        out_specs=row_spec,
        compiler_params=_params(("parallel",)),
        name="mix_out",
    )(*args, wmix, pscale.reshape(1, wide), wa, wb, wo)


def _softmax_rows(s):
    p = jnp.exp(s - jnp.max(s, axis=-1, keepdims=True))
    return p, 1.0 / jnp.sum(p, axis=-1, keepdims=True)


def _xattn_prompt_kernel(h_ref, g_ref, wq_ref, mem_ref, gm_ref, wk_ref, wv_ref, wo_ref,
                         o_ref, mk_ref, mv_ref, kb_ref, vb_ref, *, tiles):
    groups = 2 * X_HEADS

    @pl.when(pl.program_id(0) % tiles == 0)
    def _():
        mn = _rmsnorm(mem_ref[0], gm_ref[...]).astype(BF16)
        k = _dot(mn, wk_ref[...])
        v = _dot(mn, wv_ref[...])
        kb_ref[...] = k.astype(BF16)
        vb_ref[...] = v.astype(BF16)
        mk_ref[0] = k.reshape(MEM_LEN, groups, LANE)
        mv_ref[0] = v.reshape(MEM_LEN, groups, LANE)

    def head(ref, hd):
        lo = ref[:, hd * LANE:(hd + 1) * LANE]
        hi = ref[:, (X_HEADS + hd) * LANE:(X_HEADS + hd + 1) * LANE]
        return jnp.concatenate([lo, hi], axis=1)

    h = h_ref[...]
    hn = _rmsnorm(h, g_ref[...]).astype(BF16)
    q = (_dot(hn, wq_ref[...]) * (X_HEAD_DIM ** -0.5)).astype(BF16)

    def scores(hd):
        return _dot_nt(q[:, hd * X_HEAD_DIM:(hd + 1) * X_HEAD_DIM], head(kb_ref, hd))

    outs = []
    s_next = scores(0)
    for hd in range(X_HEADS):
        s_cur = s_next
        if hd + 1 < X_HEADS:
            s_next = scores(hd + 1)
        p, inv = _softmax_rows(s_cur)
        outs.append((_dot(p.astype(BF16), head(vb_ref, hd)) * inv).astype(BF16))
    o = jnp.concatenate(outs, axis=1)
    o_ref[...] = h + _dot(o, wo_ref[...])


def xattn_prompt(h, g, wq, mem, gm, wk, wv, wo, *, tm):
    m, d = h.shape
    batch = mem.shape[0]
    tiles = m // batch // tm
    groups = 2 * X_HEADS
    row_spec = pl.BlockSpec((tm, d), lambda i: (i, 0))
    vec_spec = pl.BlockSpec((1, d), lambda i: (0, 0))
    w_spec = pl.BlockSpec((d, d), lambda i: (0, 0))
    kv_spec = pl.BlockSpec((1, MEM_LEN, groups, LANE), lambda i: (i // tiles, 0, 0, 0))
    kv_shape = jax.ShapeDtypeStruct((batch, MEM_LEN, groups, LANE), F32)
    return pl.pallas_call(
        functools.partial(_xattn_prompt_kernel, tiles=tiles),
        out_shape=(jax.ShapeDtypeStruct((m, d), F32), kv_shape, kv_shape),
        grid=(m // tm,),
        in_specs=[row_spec, vec_spec, w_spec,
                  pl.BlockSpec((1, MEM_LEN, d), lambda i: (i // tiles, 0, 0)), vec_spec,
                  w_spec, w_spec, w_spec],
        out_specs=(row_spec, kv_spec, kv_spec),
        scratch_shapes=[pltpu.VMEM((MEM_LEN, d), BF16), pltpu.VMEM((MEM_LEN, d), BF16)],
        compiler_params=_params(("arbitrary",)),
        name="xattn_prompt",
    )(h, g.reshape(1, d), wq, mem, gm.reshape(1, d), wk, wv, wo)


def _sample_attention_probs(q_ref, k_ref):
    nseq, krows, _ = k_ref.shape
    half = XQ_ROWS // 2
    lane = lax.broadcasted_iota(jnp.int32, (half, krows), 1)
    row = lax.broadcasted_iota(jnp.int32, (half, krows), 0)
    valid = (lane & 7) == (row >> 2)
    out = []
    for j in range(nseq):
        q = (q_ref[j * XQ_ROWS:(j + 1) * XQ_ROWS, :] * (X_HEAD_DIM ** -0.5)).astype(BF16)
        g = _dot_nt(q, k_ref[j].astype(BF16))
        s = g[:half] + pltpu.roll(g[half:], krows - 4, axis=1)
        s = jnp.where(valid, s, -1e30)
        p = jnp.exp(s - jnp.max(s, axis=-1, keepdims=True))
        inv = 1.0 / jnp.sum(p, axis=-1, keepdims=True)
        pe = jnp.concatenate([p, pltpu.roll(p, 4, axis=1)], axis=0).astype(BF16)
        out.append((pe, jnp.concatenate([inv, inv], axis=0)))
    return out


def _sample_attention_values(probs, v_ref, o_ref):
    for j, (pe, inv) in enumerate(probs):
        o = _dot(pe, v_ref[j].astype(BF16)) * inv
        o_ref[j * XQ_ROWS:(j + 1) * XQ_ROWS, :] = o.astype(o_ref.dtype)


def _mlp_kernel(*refs, with_attn):
    if with_attn:
        (h_ref, g_ref, wu_ref, wd_ref, gf_ref, q_ref, k_ref, v_ref,
         y_ref, a_ref, hn_ref, acc_ref) = refs
    else:
        h_ref, g_ref, wu_ref, wd_ref, gf_ref, y_ref, hn_ref, acc_ref = refs
    f = pl.program_id(1)

    @pl.when(f == 0)
    def _():
        h = h_ref[...]
        hn_ref[...] = _rmsnorm(h, g_ref[...]).astype(BF16)
        acc_ref[...] = h

    if with_attn:
        probs = _sample_attention_probs(q_ref, k_ref)
    a = jnp.maximum(_dot(hn_ref[...], wu_ref[...]), 0.0)
    if with_attn:
        _sample_attention_values(probs, v_ref, a_ref)
    acc_ref[...] += _dot((a * a).astype(BF16), wd_ref[...])

    @pl.when(f == pl.num_programs(1) - 1)
    def _():
        y_ref[...] = _rmsnorm(acc_ref[...], gf_ref[...])


def mlp_final(h, g, wu, wd, gf, *, tm, tf, attn=None):
    m, d = h.shape
    ff = wu.shape[1]
    nf = ff // tf
    row_spec = pl.BlockSpec((tm, d), lambda i, f: (i, 0))
    vec_spec = pl.BlockSpec((1, d), lambda i, f: (0, 0))
    in_specs = [row_spec, vec_spec,
                pl.BlockSpec((d, tf), lambda i, f: (0, f)),
                pl.BlockSpec((tf, d), lambda i, f: (f, 0)),
                vec_spec]
    args = [h, g.reshape(1, d), wu, wd, gf.reshape(1, d)]
    out_shape = jax.ShapeDtypeStruct((m, d), F32)
    out_specs = row_spec
    if attn is not None:
        qhat, ck, cv = attn
        steps = (m // tm) * nf
        nseq = ck.shape[0] // steps
        assert nseq * steps == ck.shape[0]
        q_spec = pl.BlockSpec((nseq * XQ_ROWS, LANE), lambda i, f: (i * nf + f, 0))
        kv_spec = pl.BlockSpec((nseq,) + ck.shape[1:], lambda i, f: (i * nf + f, 0, 0))
        in_specs += [q_spec, kv_spec, kv_spec]
        args += [qhat, ck, cv]
        out_shape = (out_shape, jax.ShapeDtypeStruct(qhat.shape, BF16))
        out_specs = (row_spec, q_spec)
    return pl.pallas_call(
        functools.partial(_mlp_kernel, with_attn=attn is not None),
        out_shape=out_shape,
        grid=(m // tm, nf),
        in_specs=in_specs,
        out_specs=out_specs,
        scratch_shapes=[pltpu.VMEM((tm, d), BF16), pltpu.VMEM((tm, d), F32)],
        compiler_params=_params(("parallel", "arbitrary")),
        name="mlp_final",
    )(*args)


def _prep_weights_kernel(win_ref, *refs):
    n = (len(refs) - 1) // 2
    in_refs, win_out, out_refs = refs[:n], refs[n], refs[n + 1:]
    gz_lo = 2 * GLA_KEY_W + GLA_VAL_W
    gz_hi = gz_lo + GLA_GATE_RANK
    gz_out = IN_COLS["gz"][0]
    win_out[:gz_lo, :] = win_ref[:gz_lo, :].astype(BF16)
    win_out[gz_lo:gz_out, :] = win_ref[gz_hi:, :].astype(BF16)
    win_out[gz_out:gz_out + GLA_GATE_RANK, :] = win_ref[gz_lo:gz_hi, :].astype(BF16)
    win_out[gz_out + GLA_GATE_RANK:, :] = jnp.zeros(
        (PROJ_W - gz_out - GLA_GATE_RANK, win_out.shape[1]), BF16)
    halves = X_HEAD_DIM // LANE
    for i_ref, o_ref in zip(in_refs, out_refs):
        for hd in range(X_HEADS):
            for c in range(halves):
                src = (hd * halves + c) * LANE
                dst = (c * X_HEADS + hd) * LANE
                o_ref[:, dst:dst + LANE] = i_ref[:, src:src + LANE].astype(BF16)


def prep_weights(w_in_t, others, nblk=8):
    def spec(shape):
        return pl.BlockSpec((None, shape[1] // nblk, shape[2]), lambda i: (0, i, 0))

    def out_spec(rows, cols):
        return pl.BlockSpec((rows // nblk, cols), lambda i: (i, 0))

    width, d = w_in_t.shape
    out_shapes = [jax.ShapeDtypeStruct((PROJ_W, d), BF16)]
    out_specs = [pl.BlockSpec((PROJ_W, d // nblk), lambda i: (0, i))]
    for w in others:
        out_shapes.append(jax.ShapeDtypeStruct(w.shape[1:], BF16))
        out_specs.append(out_spec(*w.shape[1:]))
    return pl.pallas_call(
        _prep_weights_kernel,
        out_shape=tuple(out_shapes),
        grid=(nblk,),
        in_specs=[pl.BlockSpec((width, d // nblk), lambda i: (0, i))] + [spec(w.shape) for w in others],
        out_specs=tuple(out_specs),
        compiler_params=_params(("parallel",)),
        name="prep_weights",
    )(w_in_t, *others)


def _cache_rows(c):
    b, m, h, dh = c.shape
    return c.reshape(b, m, h, dh // LANE, LANE).transpose(0, 1, 3, 2, 4).reshape(b, m * h * (dh // LANE), LANE)


def _mem_kv_output(kv):
    b, m, _, _ = kv.shape
    kv = kv.reshape(b, m, X_HEAD_DIM // LANE, X_HEADS, LANE).transpose(0, 1, 3, 2, 4)
    return kv.reshape(1, b, m, X_HEADS, X_HEAD_DIM)


def kernel(x_prompt, x_sample, mem_prompt, state_gla, state_pool, cache_mem_k, cache_mem_v,
           norm_mix_g, w_in, w_gk_up, b_gk, gla_norm_g, w_pool_mix, pool_scale,
           w_branch_a, w_branch_b, w_out, norm_x_g, norm_mem_g, w_xq, w_xk, w_xv, w_xo,
           norm_mlp_g, w_up, w_down, norm_final_g):
    depth = w_in.shape[0]
    assert depth == 1
    batch, seq, d = x_prompt.shape
    dec_batch, dec_seq, _ = x_sample.shape
    mp = batch * seq
    ms = dec_batch * dec_seq

    w_in_r, wxk, wxv = prep_weights(w_in[0].T, (w_xk, w_xv))
    wgk = jnp.concatenate(
        [w_gk_up[0], jnp.zeros((LANE - GLA_GATE_RANK, GLA_KEY_W), F32)], axis=0).astype(BF16)
    wmix = w_pool_mix[0].astype(BF16)

    xp = x_prompt.reshape(mp, d)
    xs = x_sample.reshape(ms, d)

    (qk_p, v_p, gate_p, u_p, sa_p, sb_p, lah_p, lal_p, wa, wb, wo, wxq, wxo, wu, wd) = in_proj(
        xp, norm_mix_g[0], w_in_r, wgk, b_gk[0], TM_PROJ,
        cast=(w_branch_a, w_branch_b, w_out, w_xq, w_xo, w_up, w_down))

    qk_s, v_s, gate_s, u_s, sa_s, sb_s, lah_s, lal_s = in_proj(xs, norm_mix_g[0], w_in_r, wgk, b_gk[0], ms)
    gla_rows = SAMPLE_SEQS_PER_GLA_STEP * dec_seq
    o_s, sg_s = gla(qk_s, v_s, lah_s, lal_s, gate_s, gla_norm_g[0], state_gla[0],
                    groups=ms // gla_rows, tt=gla_rows, rows=gla_rows, seg=dec_seq)
    d_tm, sp_tm = pool_sample(state_pool[0].transpose(1, 0, 2),
                              u_s.reshape(dec_batch, dec_seq, d).transpose(1, 0, 2),
                              PAST_LEN, SAMPLE_SEQS_PER_POOL_STEP)
    h_s = mix_out(o_s, d_tm.transpose(1, 0, 2).reshape(ms, d), sa_s, sb_s, xs, wmix, pool_scale[0],
                  wa, wb, wo, tm=ms, seq_len=dec_seq, fused_pool=False)
    q_s = norm_matmul(h_s, norm_x_g[0], wxq, tm=ms, tn=d)
    halves = X_HEAD_DIM // LANE
    qhat = q_s.reshape(dec_batch, dec_seq, X_HEADS, halves, LANE).transpose(0, 3, 2, 1, 4)
    qhat = qhat.reshape(dec_batch * XQ_ROWS, LANE)

    o_p, sg_p = gla(qk_p, v_p, lah_p, lal_p, gate_p, gla_norm_g[0], None,
                    groups=batch, tt=TT_GLA, rows=GLA_CHUNK, seg=GLA_CHUNK)
    h_p = mix_out(o_p, u_p, sa_p, sb_p, xp, wmix, pool_scale[0], wa, wb, wo,
                  tm=TM_MIX, seq_len=seq, fused_pool=True)
    h_p, mk_p, mv_p = xattn_prompt(h_p, norm_x_g[0], wxq, mem_prompt, norm_mem_g[0], wxk, wxv, wxo,
                                   tm=TM_XATTN)
    y_p, a_s = mlp_final(h_p, norm_mlp_g[0], wu, wd, norm_final_g, tm=TM_MLP, tf=TF_MLP,
                         attn=(qhat, _cache_rows(cache_mem_k[0]), _cache_rows(cache_mem_v[0])))
    sp_p = u_p.reshape(batch, seq, d)[:, seq - POOL_BUF:]

    a_s = a_s.reshape(dec_batch, halves, X_HEADS, dec_seq, LANE).transpose(0, 3, 2, 1, 4).reshape(ms, d)
    h_s = matmul_residual(a_s, wxo, h_s, tm=ms)
    y_s = mlp_final(h_s, norm_mlp_g[0], wu, wd, norm_final_g, tm=ms, tf=TF_MLP)
    sp_s = sp_tm.transpose(1, 0, 2)

    return (y_p.reshape(batch, seq, d),
            y_s.reshape(dec_batch, dec_seq, d),
            _mem_kv_output(mk_p),
            _mem_kv_output(mv_p),
            sg_p[None],
            sg_s[None],
            sp_p[None],
            sp_s[None])
```

```python
import functools

import jax
import jax.numpy as jnp
from jax import lax
from jax.experimental import pallas as pl
from jax.experimental.pallas import tpu as pltpu

F32 = jnp.float32
BF16 = jnp.bfloat16

D_MODEL = 1024
GLA_HEADS = 4
GLA_DK = 128
GLA_DV = 256
GLA_KEY_W = GLA_HEADS * GLA_DK
GLA_VAL_W = GLA_HEADS * GLA_DV
GLA_GATE_RANK = 16
GLA_GATE_NORM = 16.0
POOL_WINDOWS = (2, 4, 8, 16)
POOL_G = 256
POOL_BUF = 15
MEM_LEN = 256
X_HEADS = 4
X_HEAD_DIM = 256
EPS = 1e-6
PAST_LEN = 16384

LANE = 128
HALO = 16
VMEM_LIMIT = 52 * 1024 * 1024
XQ_ROWS = 32

IN_COLS = {
    "qk": (0, 2 * GLA_KEY_W),
    "v": (1024, GLA_VAL_W),
    "og": (2048, GLA_VAL_W),
    "u": (3072, D_MODEL),
    "ga": (4096, D_MODEL),
    "gb": (5120, D_MODEL),
    "gz": (6144, LANE),
}
PROJ_W = 6272

TM_PROJ = 512
TM_MIX = 512
TM_XATTN = 1024
TM_MLP = 1024
TF_MLP = 1024
TT_GLA = 1024
GLA_CHUNK = 128
GLA_FAST_RANGE = 60.0
SAMPLE_SEQS_PER_GLA_STEP = 8
SAMPLE_SEQS_PER_POOL_STEP = 32


def _params(sem):
    return pltpu.CompilerParams(dimension_semantics=sem, vmem_limit_bytes=VMEM_LIMIT)


def _rmsnorm(x, g):
    return x * lax.rsqrt(jnp.mean(x * x, axis=-1, keepdims=True) + EPS) * g


def _dot(a, b):
    return jnp.dot(a, b, preferred_element_type=F32)


def _dot_nt(a, b):
    return lax.dot_general(a, b, (((1,), (1,)), ((), ())), preferred_element_type=F32)


def _dot_tn(a, b):
    return lax.dot_general(a, b, (((0,), (0,)), ((), ())), preferred_element_type=F32)


def _norm_matmul_kernel(x_ref, g_ref, w_ref, o_ref, xn_ref):
    @pl.when(pl.program_id(1) == 0)
    def _():
        xn_ref[...] = _rmsnorm(x_ref[...], g_ref[...]).astype(BF16)

    o_ref[...] = _dot(xn_ref[...], w_ref[...]).astype(o_ref.dtype)


def norm_matmul(x, g, w, tm, tn, out_dtype=F32):
    m, k = x.shape
    n = w.shape[1]
    return pl.pallas_call(
        _norm_matmul_kernel,
        out_shape=jax.ShapeDtypeStruct((m, n), out_dtype),
        grid=(m // tm, n // tn),
        in_specs=[
            pl.BlockSpec((tm, k), lambda i, j: (i, 0)),
            pl.BlockSpec((1, k), lambda i, j: (0, 0)),
            pl.BlockSpec((k, tn), lambda i, j: (0, j)),
        ],
        out_specs=pl.BlockSpec((tm, tn), lambda i, j: (i, j)),
        scratch_shapes=[pltpu.VMEM((tm, k), BF16)],
        compiler_params=_params(("parallel", "arbitrary")),
        name="norm_matmul",
    )(x, g.reshape(1, k), w)


def _log_decay_split(z):
    la = (jnp.minimum(z, 0.0) - jnp.log(1.0 + jnp.exp(-jnp.abs(z)))) * (1.0 / GLA_GATE_NORM)
    la_hi = la.astype(BF16)
    return la_hi, (la - la_hi.astype(F32)).astype(BF16)


def _pool_diff(u, halo, t_in_seq):
    tm = u.shape[0]
    halo = jnp.where(t_in_seq == 0, 0.0, halo)
    pos1 = (t_in_seq * tm + 1 + lax.broadcasted_iota(jnp.int32, (tm, 1), 0)).astype(F32)
    out = []
    for g, w in enumerate(POOL_WINDOWS):
        cs = slice(g * POOL_G, (g + 1) * POOL_G)
        s = jnp.concatenate([halo[:, cs], u[:, cs]], axis=0)
        shift = 1
        while shift < w:
            s = s + pltpu.roll(s, shift, axis=0)
            shift *= 2
        out.append(s[HALO:, :] * (1.0 / jnp.minimum(pos1, float(w))) - u[:, cs])
    return jnp.concatenate(out, axis=1)


def _in_proj_kernel(*refs, ncast, tiles_per_seq):
    pool = tiles_per_seq is not None
    nout = 9 if pool else 8
    (x_ref, g_ref, w_ref, wgk_ref, bgk_ref) = refs[:5]
    cast_in = refs[5:5 + ncast]
    outs = refs[5 + ncast:5 + ncast + nout]
    (qk_ref, v_ref, gate_ref, u_ref, siga_ref, sigb_ref, lahi_ref, lalo_ref) = outs[:8]
    cast_out = refs[5 + ncast + nout:5 + 2 * ncast + nout]
    xn = _rmsnorm(x_ref[...], g_ref[...]).astype(BF16)

    def piece(name):
        col, width = IN_COLS[name]
        return _dot_nt(xn, w_ref[col:col + width, :])

    u = piece("u")
    u_ref[...] = u
    if pool:
        d_ref, tail_ref = outs[8], refs[-1]
        tm = u.shape[0]
        d_ref[...] = _pool_diff(u, tail_ref[...], pl.program_id(0) % tiles_per_seq).astype(BF16)
        tail_ref[...] = u[tm - HALO:, :]
    og = piece("og")
    gate_ref[...] = (og * jax.nn.sigmoid(og)).astype(BF16)
    siga_ref[...] = jax.nn.sigmoid(piece("ga")).astype(BF16)
    gz = piece("gz").astype(BF16)
    sigb_ref[...] = jax.nn.sigmoid(piece("gb")).astype(BF16)
    z = _dot(gz, wgk_ref[...]) + bgk_ref[...]
    lahi_ref[...], lalo_ref[...] = _log_decay_split(z)
    v_ref[...] = piece("v").astype(BF16)
    qk_ref[...] = piece("qk")
    for i_ref, o_ref in zip(cast_in, cast_out):
        o_ref[...] = i_ref[...].astype(BF16)


def in_proj(x, g, w, wgk, bgk, tm, cast=(), pool_seq_len=None):
    m, k = x.shape
    steps = m // tm
    outs = [(2 * GLA_KEY_W, F32), (GLA_VAL_W, BF16), (GLA_VAL_W, BF16), (D_MODEL, F32),
            (D_MODEL, BF16), (D_MODEL, BF16), (GLA_KEY_W, BF16), (GLA_KEY_W, BF16)]
    scratch = []
    tiles_per_seq = None
    if pool_seq_len is not None:
        outs.append((D_MODEL, BF16))
        scratch.append(pltpu.VMEM((HALO, D_MODEL), F32))
        tiles_per_seq = pool_seq_len // tm
    const = lambda shape: pl.BlockSpec(shape, lambda i: (0, 0))
    out_shape = [jax.ShapeDtypeStruct((m, width), dtype) for width, dtype in outs]
    out_specs = [pl.BlockSpec((tm, width), lambda i: (i, 0)) for width, _ in outs]
    in_specs = [pl.BlockSpec((tm, k), lambda i: (i, 0)), const((1, k)), const((PROJ_W, k)),
                const((LANE, GLA_KEY_W)), const((1, GLA_KEY_W))]
    for wc in cast:
        _, rows, cols = wc.shape
        in_specs.append(pl.BlockSpec((None, rows // steps, cols), lambda i: (0, i, 0)))
        out_specs.append(pl.BlockSpec((rows // steps, cols), lambda i: (i, 0)))
        out_shape.append(jax.ShapeDtypeStruct((rows, cols), BF16))
    return pl.pallas_call(
        functools.partial(_in_proj_kernel, ncast=len(cast), tiles_per_seq=tiles_per_seq),
        out_shape=tuple(out_shape),
        grid=(steps,),
        in_specs=in_specs,
        out_specs=tuple(out_specs),
        scratch_shapes=scratch,
        compiler_params=_params(("arbitrary",)),
        name="in_proj",
    )(x, g.reshape(1, k), w, wgk, bgk.reshape(1, GLA_KEY_W), *cast)


def _matmul_res_kernel(x_ref, w_ref, r_ref, o_ref):
    o_ref[...] = _dot(x_ref[...].astype(BF16), w_ref[...]) + r_ref[...]


def matmul_residual(x, w, res, tm):
    m, k = x.shape
    n = w.shape[1]
    return pl.pallas_call(
        _matmul_res_kernel,
        out_shape=jax.ShapeDtypeStruct((m, n), F32),
        grid=(m // tm,),
        in_specs=[
            pl.BlockSpec((tm, k), lambda i: (i, 0)),
            pl.BlockSpec((k, n), lambda i: (0, 0)),
            pl.BlockSpec((tm, n), lambda i: (i, 0)),
        ],
        out_specs=pl.BlockSpec((tm, n), lambda i: (i, 0)),
        compiler_params=_params(("parallel",)),
        name="matmul_residual",
    )(x, w, res)


def _gla_kernel(q_ref, k_ref, v_ref, lahi_ref, lalo_ref, gate_ref, gn_ref, s0_ref,
                o_ref, s_ref, s_in_ref, b_ref, *, rows, seg, zero_init):
    tt = q_ref.shape[0]
    nchunks = tt // rows
    nseg = rows // seg
    seg_shift = seg.bit_length() - 1

    @pl.when(pl.program_id(1) == 0)
    def _():
        if zero_init:
            s_ref[...] = jnp.zeros(s_ref.shape, F32)
        else:
            s_ref[...] = s0_ref[...]

    s_in_ref[...] = s_ref[...]

    ri = lax.broadcasted_iota(jnp.int32, (rows, rows), 0)
    ci = lax.broadcasted_iota(jnp.int32, (rows, rows), 1)
    same_seg = (ri >> seg_shift) == (ci >> seg_shift)
    causal = jnp.logical_and(same_seg, ci <= ri)
    l_cum = jnp.where(causal, 1.0, 0.0).astype(BF16)
    l_seg = jnp.where(same_seg, 1.0, 0.0).astype(BF16)
    row_seg = lax.broadcasted_iota(jnp.int32, (rows, GLA_DK), 0) >> seg_shift
    row_seg_v = lax.broadcasted_iota(jnp.int32, (rows, GLA_DV), 0) >> seg_shift
    gn = gn_ref[...]
    qscale = GLA_DK ** -0.5

    def cumulative(c):
        rsl = slice(c * rows, (c + 1) * rows)
        lah = lahi_ref[rsl, :]
        lal = lalo_ref[rsl, :]
        b = _dot(l_cum, lah) + _dot(l_cum, lal)
        if nseg == 1:
            b_end = b[rows - 1:rows, :]
        else:
            b_end = _dot(l_seg, lah) + _dot(l_seg, lal)
        return lah, lal, b, b_end

    def pairwise_scores(c, ks, qs_h, b_h):
        sub = 8

        def columns(blk, acc):
            r0 = pl.multiple_of(blk * sub, sub)
            b_blk = b_ref[pl.ds(r0, sub), ks]
            k_blk = k_ref[pl.ds(c * rows + r0, sub), ks]
            for j in range(sub):
                e = jnp.exp(jnp.minimum(b_h - b_blk[j:j + 1, :], 0.0))
                col = jnp.sum(qs_h * e * k_blk[j:j + 1, :], axis=1, keepdims=True)
                acc = acc + jnp.where(ci == r0 + j, col, 0.0)
            return acc

        return lax.fori_loop(0, rows // sub, columns, jnp.zeros((rows, rows), F32))

    def run(exact):
        worst = jnp.float32(0.0)
        ahead = cumulative(0)
        for c in range(nchunks):
            rsl = slice(c * rows, (c + 1) * rows)
            lah, lal, b, b_end = ahead
            if c + 1 < nchunks:
                ahead = cumulative(c + 1)
            qs = q_ref[rsl, :] * qscale
            k = k_ref[rsl, :]
            qt_b = (qs * jnp.exp(b)).astype(BF16)
            kd = k * jnp.exp(b_end - b)
            if exact:
                b_ref[...] = b
            else:
                kt_b = (k * jnp.exp(-b)).astype(BF16)
                worst = jnp.maximum(worst, jnp.max(-b_end))
            heads = []
            for h in range(GLA_HEADS):
                ks = slice(h * GLA_DK, (h + 1) * GLA_DK)
                v_h = v_ref[rsl, h * GLA_DV:(h + 1) * GLA_DV]
                if exact:
                    scores = pairwise_scores(c, ks, qs[:, ks], b[:, ks])
                else:
                    scores = _dot_nt(qt_b[:, ks], kt_b[:, ks])
                inter = None
                new_states = []
                for j in range(nseg):
                    s_old = s_ref[j, h]
                    inter_j = _dot(qt_b[:, ks], s_old.astype(BF16))
                    if nseg == 1:
                        inter = inter_j
                        kd_j = kd[:, ks].astype(BF16)
                        dcol = jnp.broadcast_to(b_end[:, ks], (GLA_DK, GLA_DK)).T
                    else:
                        inter_j = jnp.where(row_seg_v == j, inter_j, 0.0)
                        inter = inter_j if inter is None else inter + inter_j
                        kd_j = jnp.where(row_seg == j, kd[:, ks], 0.0).astype(BF16)
                        ones_j = jnp.where(row_seg == j, 1.0, 0.0).astype(BF16)
                        dcol = _dot_tn(lah[:, ks], ones_j) + _dot_tn(lal[:, ks], ones_j)
                    new_states.append((s_old, dcol, _dot_tn(kd_j, v_h)))
                heads.append((v_h, scores, inter, new_states))
            outs = []
            for v_h, scores, inter, _ in heads:
                a = jnp.where(causal, scores, 0.0).astype(BF16)
                outs.append(_dot(a, v_h) + inter)
            for h, (_, _, _, new_states) in enumerate(heads):
                vs = slice(h * GLA_DV, (h + 1) * GLA_DV)
                for j, (s_old, dcol, upd) in enumerate(new_states):
                    e = jnp.exp(dcol)
                    s_ref[j, h] = s_old * jnp.concatenate([e, e], axis=1) + upd
                on = _rmsnorm(outs[h], gn)
                o_ref[rsl, vs] = (on * gate_ref[rsl, vs].astype(F32)).astype(o_ref.dtype)
        return worst

    worst = run(exact=False)

    @pl.when(worst > GLA_FAST_RANGE)
    def _():
        s_ref[...] = s_in_ref[...]
        run(exact=True)


def gla(qk, v, la_hi, la_lo, gate, gn, s0, *, groups, tt, rows, seg):
    m = qk.shape[0]
    steps = m // (groups * tt)
    nseg = rows // seg
    zero_init = s0 is None
    if zero_init:
        s0 = jnp.zeros((nseg, GLA_HEADS, 8, LANE), F32)
        s0_spec = pl.BlockSpec((nseg, GLA_HEADS, 8, LANE), lambda g, t: (0, 0, 0, 0))
    else:
        s0_spec = pl.BlockSpec((nseg, GLA_HEADS, GLA_DK, GLA_DV), lambda g, t: (g, 0, 0, 0))

    def row_spec(width, blk=0):
        return pl.BlockSpec((tt, width), lambda g, t: (g * steps + t, blk))

    kern = functools.partial(_gla_kernel, rows=rows, seg=seg, zero_init=zero_init)
    return pl.pallas_call(
        kern,
        out_shape=(jax.ShapeDtypeStruct((m, GLA_VAL_W), BF16),
                   jax.ShapeDtypeStruct((groups * nseg, GLA_HEADS, GLA_DK, GLA_DV), F32)),
        grid=(groups, steps),
        in_specs=[
            row_spec(GLA_KEY_W, 0),
            row_spec(GLA_KEY_W, 1),
            row_spec(GLA_VAL_W),
            row_spec(GLA_KEY_W),
            row_spec(GLA_KEY_W),
            row_spec(GLA_VAL_W),
            pl.BlockSpec((1, GLA_DV), lambda g, t: (0, 0)),
            s0_spec,
        ],
        out_specs=(row_spec(GLA_VAL_W),
                   pl.BlockSpec((nseg, GLA_HEADS, GLA_DK, GLA_DV), lambda g, t: (g, 0, 0, 0))),
        scratch_shapes=[pltpu.VMEM((nseg, GLA_HEADS, GLA_DK, GLA_DV), F32),
                        pltpu.VMEM((rows, GLA_KEY_W), F32)],
        compiler_params=_params(("parallel", "arbitrary")),
        name="gla",
    )(qk, qk, v, la_hi, la_lo, gate, gn.reshape(1, GLA_DV), s0)


def _pool_sample_kernel(buf_ref, u_ref, d_ref, new_ref, *, pos0):
    t_new = u_ref.shape[0]

    def ext(i, cs=slice(None)):
        return buf_ref[i, :, cs] if i < POOL_BUF else u_ref[i - POOL_BUF, :, cs]

    for r in range(POOL_BUF):
        new_ref[r] = ext(r + t_new)
    for t in range(t_new):
        cur = POOL_BUF + t
        for g, w in enumerate(POOL_WINDOWS):
            cs = slice(g * POOL_G, (g + 1) * POOL_G)
            win = ext(cur, cs)
            for j in range(1, w):
                win = win + ext(cur - j, cs)
            cnt = float(min(pos0 + t + 1, w))
            d_ref[t, :, cs] = win * (1.0 / cnt) - ext(cur, cs)


def pool_sample(buf_tm, u_tm, pos0, nb):
    nbuf, nseq, width = buf_tm.shape
    t_new = u_tm.shape[0]
    buf_spec = pl.BlockSpec((nbuf, nb, width), lambda i: (0, i, 0))
    new_spec = pl.BlockSpec((t_new, nb, width), lambda i: (0, i, 0))
    return pl.pallas_call(
        functools.partial(_pool_sample_kernel, pos0=pos0),
        out_shape=(jax.ShapeDtypeStruct(u_tm.shape, F32), jax.ShapeDtypeStruct(buf_tm.shape, F32)),
        grid=(nseq // nb,),
        in_specs=[buf_spec, new_spec],
        out_specs=(new_spec, buf_spec),
        compiler_params=_params(("parallel",)),
        name="pool_sample",
    )(buf_tm, u_tm)


def _mix_out_kernel(o_ref, d_ref, siga_ref, sigb_ref, x_ref, wmix_ref, ps_ref, wa_ref, wb_ref,
                    wo_ref, h_ref):
    branch_a = _dot(o_ref[...], wa_ref[...])
    pooled = []
    for g in range(len(POOL_WINDOWS)):
        cs = slice(g * POOL_G, (g + 1) * POOL_G)
        y = _dot(d_ref[:, cs].astype(BF16), wmix_ref[g]) * ps_ref[:, cs]
        pooled.append(y.astype(BF16))
    pooled = jnp.concatenate(pooled, axis=1)
    merged = (siga_ref[...].astype(F32) * branch_a
              + sigb_ref[...].astype(F32) * _dot(pooled, wb_ref[...]))
    h_ref[...] = x_ref[...] + _dot(merged.astype(BF16), wo_ref[...])


def mix_out(o, d, sig_a, sig_b, x, wmix, pscale, wa, wb, wo, *, tm):
    m, wide = x.shape
    row_spec = pl.BlockSpec((tm, wide), lambda i: (i, 0))
    const2 = lambda shape: pl.BlockSpec(shape, lambda i: (0, 0))
    w_specs = [
        pl.BlockSpec((len(POOL_WINDOWS), POOL_G, POOL_G), lambda i: (0, 0, 0)),
        const2((1, wide)),
        const2((GLA_VAL_W, wide)),
        const2((wide, wide)),
        const2((wide, wide)),
    ]
    return pl.pallas_call(
        _mix_out_kernel,
        out_shape=jax.ShapeDtypeStruct((m, wide), F32),
        grid=(m // tm,),
        in_specs=[row_spec] * 5 + w_specs,
        out_specs=row_spec,
        compiler_params=_params(("parallel",)),
        name="mix_out",
    )(o, d, sig_a, sig_b, x, wmix, pscale.reshape(1, wide), wa, wb, wo)


def _softmax_rows(s):
    p = jnp.exp(s - jnp.max(s, axis=-1, keepdims=True))
    return p, 1.0 / jnp.sum(p, axis=-1, keepdims=True)


def _xattn_prompt_kernel(h_ref, g_ref, wq_ref, mem_ref, gm_ref, wk_ref, wv_ref, wo_ref,
                         o_ref, mk_ref, mv_ref, kb_ref, vb_ref, *, tiles):
    groups = 2 * X_HEADS

    @pl.when(pl.program_id(0) % tiles == 0)
    def _():
        mn = _rmsnorm(mem_ref[0], gm_ref[...]).astype(BF16)
        k = _dot(mn, wk_ref[...])
        v = _dot(mn, wv_ref[...])
        kb_ref[...] = k.astype(BF16)
        vb_ref[...] = v.astype(BF16)
        mk_ref[0] = k.reshape(MEM_LEN, groups, LANE)
        mv_ref[0] = v.reshape(MEM_LEN, groups, LANE)

    def head(ref, hd):
        lo = ref[:, hd * LANE:(hd + 1) * LANE]
        hi = ref[:, (X_HEADS + hd) * LANE:(X_HEADS + hd + 1) * LANE]
        return jnp.concatenate([lo, hi], axis=1)

    h = h_ref[...]
    hn = _rmsnorm(h, g_ref[...]).astype(BF16)
    q = (_dot(hn, wq_ref[...]) * (X_HEAD_DIM ** -0.5)).astype(BF16)

    def scores(hd):
        return _dot_nt(q[:, hd * X_HEAD_DIM:(hd + 1) * X_HEAD_DIM], head(kb_ref, hd))

    outs = []
    s_next = scores(0)
    for hd in range(X_HEADS):
        s_cur = s_next
        if hd + 1 < X_HEADS:
            s_next = scores(hd + 1)
        p, inv = _softmax_rows(s_cur)
        outs.append((_dot(p.astype(BF16), head(vb_ref, hd)) * inv).astype(BF16))
    o = jnp.concatenate(outs, axis=1)
    o_ref[...] = h + _dot(o, wo_ref[...])


def xattn_prompt(h, g, wq, mem, gm, wk, wv, wo, *, tm):
    m, d = h.shape
    batch = mem.shape[0]
    tiles = m // batch // tm
    groups = 2 * X_HEADS
    row_spec = pl.BlockSpec((tm, d), lambda i: (i, 0))
    vec_spec = pl.BlockSpec((1, d), lambda i: (0, 0))
    w_spec = pl.BlockSpec((d, d), lambda i: (0, 0))
    kv_spec = pl.BlockSpec((1, MEM_LEN, groups, LANE), lambda i: (i // tiles, 0, 0, 0))
    kv_shape = jax.ShapeDtypeStruct((batch, MEM_LEN, groups, LANE), F32)
    return pl.pallas_call(
        functools.partial(_xattn_prompt_kernel, tiles=tiles),
        out_shape=(jax.ShapeDtypeStruct((m, d), F32), kv_shape, kv_shape),
        grid=(m // tm,),
        in_specs=[row_spec, vec_spec, w_spec,
                  pl.BlockSpec((1, MEM_LEN, d), lambda i: (i // tiles, 0, 0)), vec_spec,
                  w_spec, w_spec, w_spec],
        out_specs=(row_spec, kv_spec, kv_spec),
        scratch_shapes=[pltpu.VMEM((MEM_LEN, d), BF16), pltpu.VMEM((MEM_LEN, d), BF16)],
        compiler_params=_params(("arbitrary",)),
        name="xattn_prompt",
    )(h, g.reshape(1, d), wq, mem, gm.reshape(1, d), wk, wv, wo)


def _sample_attention_probs(q_ref, k_ref):
    nseq, krows, _ = k_ref.shape
    half = XQ_ROWS // 2
    lane = lax.broadcasted_iota(jnp.int32, (half, krows), 1)
    row = lax.broadcasted_iota(jnp.int32, (half, krows), 0)
    valid = (lane & 7) == (row >> 2)
    out = []
    for j in range(nseq):
        q = (q_ref[j * XQ_ROWS:(j + 1) * XQ_ROWS, :] * (X_HEAD_DIM ** -0.5)).astype(BF16)
        g = _dot_nt(q, k_ref[j].astype(BF16))
        s = g[:half] + pltpu.roll(g[half:], krows - 4, axis=1)
        s = jnp.where(valid, s, -1e30)
        p = jnp.exp(s - jnp.max(s, axis=-1, keepdims=True))
        inv = 1.0 / jnp.sum(p, axis=-1, keepdims=True)
        pe = jnp.concatenate([p, pltpu.roll(p, 4, axis=1)], axis=0).astype(BF16)
        out.append((pe, jnp.concatenate([inv, inv], axis=0)))
    return out


def _sample_attention_values(probs, v_ref, o_ref):
    for j, (pe, inv) in enumerate(probs):
        o = _dot(pe, v_ref[j].astype(BF16)) * inv
        o_ref[j * XQ_ROWS:(j + 1) * XQ_ROWS, :] = o.astype(o_ref.dtype)


def _mlp_kernel(*refs, with_attn):
    if with_attn:
        (h_ref, g_ref, wu_ref, wd_ref, gf_ref, q_ref, k_ref, v_ref,
         y_ref, a_ref, hn_ref, acc_ref) = refs
    else:
        h_ref, g_ref, wu_ref, wd_ref, gf_ref, y_ref, hn_ref, acc_ref = refs
    f = pl.program_id(1)

    @pl.when(f == 0)
    def _():
        h = h_ref[...]
        hn_ref[...] = _rmsnorm(h, g_ref[...]).astype(BF16)
        acc_ref[...] = h

    if with_attn:
        probs = _sample_attention_probs(q_ref, k_ref)
    a = jnp.maximum(_dot(hn_ref[...], wu_ref[...]), 0.0)
    if with_attn:
        _sample_attention_values(probs, v_ref, a_ref)
    acc_ref[...] += _dot((a * a).astype(BF16), wd_ref[...])

    @pl.when(f == pl.num_programs(1) - 1)
    def _():
        y_ref[...] = _rmsnorm(acc_ref[...], gf_ref[...])


def mlp_final(h, g, wu, wd, gf, *, tm, tf, attn=None):
    m, d = h.shape
    ff = wu.shape[1]
    nf = ff // tf
    row_spec = pl.BlockSpec((tm, d), lambda i, f: (i, 0))
    vec_spec = pl.BlockSpec((1, d), lambda i, f: (0, 0))
    in_specs = [row_spec, vec_spec,
                pl.BlockSpec((d, tf), lambda i, f: (0, f)),
                pl.BlockSpec((tf, d), lambda i, f: (f, 0)),
                vec_spec]
    args = [h, g.reshape(1, d), wu, wd, gf.reshape(1, d)]
    out_shape = jax.ShapeDtypeStruct((m, d), F32)
    out_specs = row_spec
    if attn is not None:
        qhat, ck, cv = attn
        steps = (m // tm) * nf
        nseq = ck.shape[0] // steps
        assert nseq * steps == ck.shape[0]
        q_spec = pl.BlockSpec((nseq * XQ_ROWS, LANE), lambda i, f: (i * nf + f, 0))
        kv_spec = pl.BlockSpec((nseq,) + ck.shape[1:], lambda i, f: (i * nf + f, 0, 0))
        in_specs += [q_spec, kv_spec, kv_spec]
        args += [qhat, ck, cv]
        out_shape = (out_shape, jax.ShapeDtypeStruct(qhat.shape, BF16))
        out_specs = (row_spec, q_spec)
    return pl.pallas_call(
        functools.partial(_mlp_kernel, with_attn=attn is not None),
        out_shape=out_shape,
        grid=(m // tm, nf),
        in_specs=in_specs,
        out_specs=out_specs,
        scratch_shapes=[pltpu.VMEM((tm, d), BF16), pltpu.VMEM((tm, d), F32)],
        compiler_params=_params(("parallel", "arbitrary")),
        name="mlp_final",
    )(*args)


def _prep_weights_kernel(win_ref, *refs):
    n = (len(refs) - 1) // 2
    in_refs, win_out, out_refs = refs[:n], refs[n], refs[n + 1:]
    gz_lo = 2 * GLA_KEY_W + GLA_VAL_W
    gz_hi = gz_lo + GLA_GATE_RANK
    gz_out = IN_COLS["gz"][0]
    win_out[:gz_lo, :] = win_ref[:gz_lo, :].astype(BF16)
    win_out[gz_lo:gz_out, :] = win_ref[gz_hi:, :].astype(BF16)
    win_out[gz_out:gz_out + GLA_GATE_RANK, :] = win_ref[gz_lo:gz_hi, :].astype(BF16)
    win_out[gz_out + GLA_GATE_RANK:, :] = jnp.zeros(
        (PROJ_W - gz_out - GLA_GATE_RANK, win_out.shape[1]), BF16)
    halves = X_HEAD_DIM // LANE
    for i_ref, o_ref in zip(in_refs, out_refs):
        for hd in range(X_HEADS):
            for c in range(halves):
                src = (hd * halves + c) * LANE
                dst = (c * X_HEADS + hd) * LANE
                o_ref[:, dst:dst + LANE] = i_ref[:, src:src + LANE].astype(BF16)


def prep_weights(w_in_t, others, nblk=8):
    def spec(shape):
        return pl.BlockSpec((None, shape[1] // nblk, shape[2]), lambda i: (0, i, 0))

    def out_spec(rows, cols):
        return pl.BlockSpec((rows // nblk, cols), lambda i: (i, 0))

    width, d = w_in_t.shape
    out_shapes = [jax.ShapeDtypeStruct((PROJ_W, d), BF16)]
    out_specs = [pl.BlockSpec((PROJ_W, d // nblk), lambda i: (0, i))]
    for w in others:
        out_shapes.append(jax.ShapeDtypeStruct(w.shape[1:], BF16))
        out_specs.append(out_spec(*w.shape[1:]))
    return pl.pallas_call(
        _prep_weights_kernel,
        out_shape=tuple(out_shapes),
        grid=(nblk,),
        in_specs=[pl.BlockSpec((width, d // nblk), lambda i: (0, i))] + [spec(w.shape) for w in others],
        out_specs=tuple(out_specs),
        compiler_params=_params(("parallel",)),
        name="prep_weights",
    )(w_in_t, *others)


def _cache_rows(c):
    b, m, h, dh = c.shape
    return c.reshape(b, m, h, dh // LANE, LANE).transpose(0, 1, 3, 2, 4).reshape(b, m * h * (dh // LANE), LANE)


def _mem_kv_output(kv):
    b, m, _, _ = kv.shape
    kv = kv.reshape(b, m, X_HEAD_DIM // LANE, X_HEADS, LANE).transpose(0, 1, 3, 2, 4)
    return kv.reshape(1, b, m, X_HEADS, X_HEAD_DIM)


def kernel(x_prompt, x_sample, mem_prompt, state_gla, state_pool, cache_mem_k, cache_mem_v,
           norm_mix_g, w_in, w_gk_up, b_gk, gla_norm_g, w_pool_mix, pool_scale,
           w_branch_a, w_branch_b, w_out, norm_x_g, norm_mem_g, w_xq, w_xk, w_xv, w_xo,
           norm_mlp_g, w_up, w_down, norm_final_g):
    depth = w_in.shape[0]
    assert depth == 1
    batch, seq, d = x_prompt.shape
    dec_batch, dec_seq, _ = x_sample.shape
    mp = batch * seq
    ms = dec_batch * dec_seq

    w_in_r, wxk, wxv = prep_weights(w_in[0].T, (w_xk, w_xv))
    wgk = jnp.concatenate(
        [w_gk_up[0], jnp.zeros((LANE - GLA_GATE_RANK, GLA_KEY_W), F32)], axis=0).astype(BF16)
    wmix = w_pool_mix[0].astype(BF16)

    xp = x_prompt.reshape(mp, d)
    xs = x_sample.reshape(ms, d)

    (qk_p, v_p, gate_p, u_p, sa_p, sb_p, lah_p, lal_p, d_p, wa, wb, wo, wxq, wxo, wu, wd) = in_proj(
        xp, norm_mix_g[0], w_in_r, wgk, b_gk[0], TM_PROJ,
        cast=(w_branch_a, w_branch_b, w_out, w_xq, w_xo, w_up, w_down), pool_seq_len=seq)

    qk_s, v_s, gate_s, u_s, sa_s, sb_s, lah_s, lal_s = in_proj(xs, norm_mix_g[0], w_in_r, wgk, b_gk[0], ms)
    gla_rows = SAMPLE_SEQS_PER_GLA_STEP * dec_seq
    o_s, sg_s = gla(qk_s, v_s, lah_s, lal_s, gate_s, gla_norm_g[0], state_gla[0],
                    groups=ms // gla_rows, tt=gla_rows, rows=gla_rows, seg=dec_seq)
    d_tm, sp_tm = pool_sample(state_pool[0].transpose(1, 0, 2),
                              u_s.reshape(dec_batch, dec_seq, d).transpose(1, 0, 2),
                              PAST_LEN, SAMPLE_SEQS_PER_POOL_STEP)
    h_s = mix_out(o_s, d_tm.transpose(1, 0, 2).reshape(ms, d), sa_s, sb_s, xs, wmix, pool_scale[0],
                  wa, wb, wo, tm=ms)
    q_s = norm_matmul(h_s, norm_x_g[0], wxq, tm=ms, tn=d)
    halves = X_HEAD_DIM // LANE
    qhat = q_s.reshape(dec_batch, dec_seq, X_HEADS, halves, LANE).transpose(0, 3, 2, 1, 4)
    qhat = qhat.reshape(dec_batch * XQ_ROWS, LANE)

    o_p, sg_p = gla(qk_p, v_p, lah_p, lal_p, gate_p, gla_norm_g[0], None,
                    groups=batch, tt=TT_GLA, rows=GLA_CHUNK, seg=GLA_CHUNK)
    h_p = mix_out(o_p, d_p, sa_p, sb_p, xp, wmix, pool_scale[0], wa, wb, wo, tm=TM_MIX)
    h_p, mk_p, mv_p = xattn_prompt(h_p, norm_x_g[0], wxq, mem_prompt, norm_mem_g[0], wxk, wxv, wxo,
                                   tm=TM_XATTN)
    y_p, a_s = mlp_final(h_p, norm_mlp_g[0], wu, wd, norm_final_g, tm=TM_MLP, tf=TF_MLP,
                         attn=(qhat, _cache_rows(cache_mem_k[0]), _cache_rows(cache_mem_v[0])))
    sp_p = u_p.reshape(batch, seq, d)[:, seq - POOL_BUF:]

    a_s = a_s.reshape(dec_batch, halves, X_HEADS, dec_seq, LANE).transpose(0, 3, 2, 1, 4).reshape(ms, d)
    h_s = matmul_residual(a_s, wxo, h_s, tm=ms)
    y_s = mlp_final(h_s, norm_mlp_g[0], wu, wd, norm_final_g, tm=ms, tf=TF_MLP)
    sp_s = sp_tm.transpose(1, 0, 2)

    return (y_p.reshape(batch, seq, d),
            y_s.reshape(dec_batch, dec_seq, d),
            _mem_kv_output(mk_p),
            _mem_kv_output(mv_p),
            sg_p[None],
            sg_s[None],
            sp_p[None],
            sp_s[None])
```

```python
import functools

import jax
import jax.numpy as jnp
from jax import lax
from jax.experimental import pallas as pl
from jax.experimental.pallas import tpu as pltpu

F32 = jnp.float32
BF16 = jnp.bfloat16

D_MODEL = 1024
GLA_HEADS = 4
GLA_DK = 128
GLA_DV = 256
GLA_KEY_W = GLA_HEADS * GLA_DK
GLA_VAL_W = GLA_HEADS * GLA_DV
GLA_GATE_RANK = 16
GLA_GATE_NORM = 16.0
POOL_WINDOWS = (2, 4, 8, 16)
POOL_G = 256
POOL_BUF = 15
MEM_LEN = 256
X_HEADS = 4
X_HEAD_DIM = 256
EPS = 1e-6
PAST_LEN = 16384

LANE = 128
HALO = 16
VMEM_LIMIT = 52 * 1024 * 1024
XQ_ROWS = 32

IN_COLS = {
    "qk": (0, 2 * GLA_KEY_W),
    "v": (1024, GLA_VAL_W),
    "og": (2048, GLA_VAL_W),
    "u": (3072, D_MODEL),
    "ga": (4096, D_MODEL),
    "gb": (5120, D_MODEL),
    "gz": (6144, LANE),
}
PROJ_W = 6272

TM_PROJ = 512
TM_MIX = 512
TM_XATTN = 1024
TM_MLP = 1024
TF_MLP = 1024
TT_GLA = 1024
GLA_CHUNK = 128
GLA_FAST_RANGE = 60.0
SAMPLE_SEQS_PER_GLA_STEP = 8
SAMPLE_SEQS_PER_POOL_STEP = 32


def _params(sem):
    return pltpu.CompilerParams(dimension_semantics=sem, vmem_limit_bytes=VMEM_LIMIT)


def _rmsnorm(x, g):
    return x * lax.rsqrt(jnp.mean(x * x, axis=-1, keepdims=True) + EPS) * g


def _dot(a, b):
    return jnp.dot(a, b, preferred_element_type=F32)


def _dot_nt(a, b):
    return lax.dot_general(a, b, (((1,), (1,)), ((), ())), preferred_element_type=F32)


def _dot_tn(a, b):
    return lax.dot_general(a, b, (((0,), (0,)), ((), ())), preferred_element_type=F32)


def _log_decay_split(z):
    la = (jnp.minimum(z, 0.0) - jnp.log(1.0 + jnp.exp(-jnp.abs(z)))) * (1.0 / GLA_GATE_NORM)
    la_hi = la.astype(BF16)
    return la_hi, (la - la_hi.astype(F32)).astype(BF16)


def _pool_diff(u, halo, t_in_seq):
    tm = u.shape[0]
    halo = jnp.where(t_in_seq == 0, 0.0, halo)
    pos1 = (t_in_seq * tm + 1 + lax.broadcasted_iota(jnp.int32, (tm, 1), 0)).astype(F32)
    out = []
    for g, w in enumerate(POOL_WINDOWS):
        cs = slice(g * POOL_G, (g + 1) * POOL_G)
        s = jnp.concatenate([halo[:, cs], u[:, cs]], axis=0)
        shift = 1
        while shift < w:
            s = s + pltpu.roll(s, shift, axis=0)
            shift *= 2
        out.append(s[HALO:, :] * (1.0 / jnp.minimum(pos1, float(w))) - u[:, cs])
    return jnp.concatenate(out, axis=1)


def _in_proj_kernel(*refs, ncast, tiles_per_seq):
    pool = tiles_per_seq is not None
    nout = 9 if pool else 8
    (x_ref, g_ref, w_ref, wgk_ref, bgk_ref) = refs[:5]
    cast_in = refs[5:5 + ncast]
    outs = refs[5 + ncast:5 + ncast + nout]
    (qk_ref, v_ref, gate_ref, u_ref, siga_ref, sigb_ref, lahi_ref, lalo_ref) = outs[:8]
    cast_out = refs[5 + ncast + nout:5 + 2 * ncast + nout]
    xn = _rmsnorm(x_ref[...], g_ref[...]).astype(BF16)

    def piece(name):
        col, width = IN_COLS[name]
        return _dot_nt(xn, w_ref[col:col + width, :])

    u = piece("u")
    if pool:
        d_ref, tail_ref = outs[8], refs[-1]
        tm = u.shape[0]
        d_ref[...] = _pool_diff(u, tail_ref[...], pl.program_id(0) % tiles_per_seq).astype(BF16)
        tail_ref[...] = u[tm - HALO:, :]
        u_ref[...] = u[tm - HALO:, :]
    else:
        u_ref[...] = u
    og = piece("og")
    gate_ref[...] = (og * jax.nn.sigmoid(og)).astype(BF16)
    siga_ref[...] = jax.nn.sigmoid(piece("ga")).astype(BF16)
    gz = piece("gz").astype(BF16)
    sigb_ref[...] = jax.nn.sigmoid(piece("gb")).astype(BF16)
    z = _dot(gz, wgk_ref[...]) + bgk_ref[...]
    lahi_ref[...], lalo_ref[...] = _log_decay_split(z)
    v_ref[...] = piece("v").astype(BF16)
    qk_ref[...] = piece("qk")
    for i_ref, o_ref in zip(cast_in, cast_out):
        o_ref[...] = i_ref[...].astype(BF16)


def in_proj(x, g, w, wgk, bgk, tm, cast=(), pool_seq_len=None):
    m, k = x.shape
    steps = m // tm
    outs = [(2 * GLA_KEY_W, F32), (GLA_VAL_W, BF16), (GLA_VAL_W, BF16), (D_MODEL, F32),
            (D_MODEL, BF16), (D_MODEL, BF16), (GLA_KEY_W, BF16), (GLA_KEY_W, BF16)]
    scratch = []
    tiles_per_seq = None
    if pool_seq_len is not None:
        outs.append((D_MODEL, BF16))
        scratch.append(pltpu.VMEM((HALO, D_MODEL), F32))
        tiles_per_seq = pool_seq_len // tm
    const = lambda shape: pl.BlockSpec(shape, lambda i: (0, 0))
    out_shape = [jax.ShapeDtypeStruct((m, width), dtype) for width, dtype in outs]
    out_specs = [pl.BlockSpec((tm, width), lambda i: (i, 0)) for width, _ in outs]
    if pool_seq_len is not None:
        u_index = 3
        out_shape[u_index] = jax.ShapeDtypeStruct((m // pool_seq_len * HALO, D_MODEL), F32)
        out_specs[u_index] = pl.BlockSpec((HALO, D_MODEL), lambda i: (i // tiles_per_seq, 0))
    in_specs = [pl.BlockSpec((tm, k), lambda i: (i, 0)), const((1, k)), const((PROJ_W, k)),
                const((LANE, GLA_KEY_W)), const((1, GLA_KEY_W))]
    for wc in cast:
        _, rows, cols = wc.shape
        in_specs.append(pl.BlockSpec((None, rows // steps, cols), lambda i: (0, i, 0)))
        out_specs.append(pl.BlockSpec((rows // steps, cols), lambda i: (i, 0)))
        out_shape.append(jax.ShapeDtypeStruct((rows, cols), BF16))
    return pl.pallas_call(
        functools.partial(_in_proj_kernel, ncast=len(cast), tiles_per_seq=tiles_per_seq),
        out_shape=tuple(out_shape),
        grid=(steps,),
        in_specs=in_specs,
        out_specs=tuple(out_specs),
        scratch_shapes=scratch,
        compiler_params=_params(("arbitrary",)),
        name="in_proj",
    )(x, g.reshape(1, k), w, wgk, bgk.reshape(1, GLA_KEY_W), *cast)


def _gla_kernel(q_ref, k_ref, v_ref, lahi_ref, lalo_ref, gate_ref, gn_ref, s0_ref,
                o_ref, s_ref, s_in_ref, b_ref, *, rows, seg, zero_init):
    tt = q_ref.shape[0]
    nchunks = tt // rows
    nseg = rows // seg
    seg_shift = seg.bit_length() - 1

    @pl.when(pl.program_id(1) == 0)
    def _():
        if zero_init:
            s_ref[...] = jnp.zeros(s_ref.shape, F32)
        else:
            s_ref[...] = s0_ref[...]

    s_in_ref[...] = s_ref[...]

    ri = lax.broadcasted_iota(jnp.int32, (rows, rows), 0)
    ci = lax.broadcasted_iota(jnp.int32, (rows, rows), 1)
    same_seg = (ri >> seg_shift) == (ci >> seg_shift)
    causal = jnp.logical_and(same_seg, ci <= ri)
    l_cum = jnp.where(causal, 1.0, 0.0).astype(BF16)
    l_seg = jnp.where(same_seg, 1.0, 0.0).astype(BF16)
    row_seg = lax.broadcasted_iota(jnp.int32, (rows, GLA_DK), 0) >> seg_shift
    row_seg_v = lax.broadcasted_iota(jnp.int32, (rows, GLA_DV), 0) >> seg_shift
    gn = gn_ref[...]
    qscale = GLA_DK ** -0.5

    def cumulative(c):
        rsl = slice(c * rows, (c + 1) * rows)
        lah = lahi_ref[rsl, :]
        lal = lalo_ref[rsl, :]
        b = _dot(l_cum, lah) + _dot(l_cum, lal)
        if nseg == 1:
            b_end = b[rows - 1:rows, :]
        else:
            b_end = _dot(l_seg, lah) + _dot(l_seg, lal)
        return lah, lal, b, b_end

    def pairwise_scores(c, ks, qs_h, b_h):
        sub = 8

        def columns(blk, acc):
            r0 = pl.multiple_of(blk * sub, sub)
            b_blk = b_ref[pl.ds(r0, sub), ks]
            k_blk = k_ref[pl.ds(c * rows + r0, sub), ks]
            for j in range(sub):
                e = jnp.exp(jnp.minimum(b_h - b_blk[j:j + 1, :], 0.0))
                col = jnp.sum(qs_h * e * k_blk[j:j + 1, :], axis=1, keepdims=True)
                acc = acc + jnp.where(ci == r0 + j, col, 0.0)
            return acc

        return lax.fori_loop(0, rows // sub, columns, jnp.zeros((rows, rows), F32))

    def run(exact):
        worst = jnp.float32(0.0)
        ahead = cumulative(0)
        for c in range(nchunks):
            rsl = slice(c * rows, (c + 1) * rows)
            lah, lal, b, b_end = ahead
            if c + 1 < nchunks:
                ahead = cumulative(c + 1)
            qs = q_ref[rsl, :] * qscale
            k = k_ref[rsl, :]
            qt_b = (qs * jnp.exp(b)).astype(BF16)
            kd = k * jnp.exp(b_end - b)
            if exact:
                b_ref[...] = b
            else:
                kt_b = (k * jnp.exp(-b)).astype(BF16)
                worst = jnp.maximum(worst, jnp.max(-b_end))
            heads = []
            for h in range(GLA_HEADS):
                ks = slice(h * GLA_DK, (h + 1) * GLA_DK)
                v_h = v_ref[rsl, h * GLA_DV:(h + 1) * GLA_DV]
                if exact:
                    scores = pairwise_scores(c, ks, qs[:, ks], b[:, ks])
                else:
                    scores = _dot_nt(qt_b[:, ks], kt_b[:, ks])
                inter = None
                new_states = []
                for j in range(nseg):
                    s_old = s_ref[j, h]
                    inter_j = _dot(qt_b[:, ks], s_old.astype(BF16))
                    if nseg == 1:
                        inter = inter_j
                        kd_j = kd[:, ks].astype(BF16)
                        dcol = jnp.broadcast_to(b_end[:, ks], (GLA_DK, GLA_DK)).T
                    else:
                        inter_j = jnp.where(row_seg_v == j, inter_j, 0.0)
                        inter = inter_j if inter is None else inter + inter_j
                        kd_j = jnp.where(row_seg == j, kd[:, ks], 0.0).astype(BF16)
                        ones_j = jnp.where(row_seg == j, 1.0, 0.0).astype(BF16)
                        dcol = _dot_tn(lah[:, ks], ones_j) + _dot_tn(lal[:, ks], ones_j)
                    new_states.append((s_old, dcol, _dot_tn(kd_j, v_h)))
                heads.append((v_h, scores, inter, new_states))
            outs = []
            for v_h, scores, inter, _ in heads:
                a = jnp.where(causal, scores, 0.0).astype(BF16)
                outs.append(_dot(a, v_h) + inter)
            for h, (_, _, _, new_states) in enumerate(heads):
                vs = slice(h * GLA_DV, (h + 1) * GLA_DV)
                for j, (s_old, dcol, upd) in enumerate(new_states):
                    e = jnp.exp(dcol)
                    s_ref[j, h] = s_old * jnp.concatenate([e, e], axis=1) + upd
                on = _rmsnorm(outs[h], gn)
                o_ref[rsl, vs] = (on * gate_ref[rsl, vs].astype(F32)).astype(o_ref.dtype)
        return worst

    worst = run(exact=False)

    @pl.when(worst > GLA_FAST_RANGE)
    def _():
        s_ref[...] = s_in_ref[...]
        run(exact=True)


def gla(qk, v, la_hi, la_lo, gate, gn, s0, *, groups, tt, rows, seg):
    m = qk.shape[0]
    steps = m // (groups * tt)
    nseg = rows // seg
    zero_init = s0 is None
    if zero_init:
        s0 = jnp.zeros((nseg, GLA_HEADS, 8, LANE), F32)
        s0_spec = pl.BlockSpec((nseg, GLA_HEADS, 8, LANE), lambda g, t: (0, 0, 0, 0))
    else:
        s0_spec = pl.BlockSpec((nseg, GLA_HEADS, GLA_DK, GLA_DV), lambda g, t: (g, 0, 0, 0))

    def row_spec(width, blk=0):
        return pl.BlockSpec((tt, width), lambda g, t: (g * steps + t, blk))

    kern = functools.partial(_gla_kernel, rows=rows, seg=seg, zero_init=zero_init)
    return pl.pallas_call(
        kern,
        out_shape=(jax.ShapeDtypeStruct((m, GLA_VAL_W), BF16),
                   jax.ShapeDtypeStruct((groups * nseg, GLA_HEADS, GLA_DK, GLA_DV), F32)),
        grid=(groups, steps),
        in_specs=[
            row_spec(GLA_KEY_W, 0),
            row_spec(GLA_KEY_W, 1),
            row_spec(GLA_VAL_W),
            row_spec(GLA_KEY_W),
            row_spec(GLA_KEY_W),
            row_spec(GLA_VAL_W),
            pl.BlockSpec((1, GLA_DV), lambda g, t: (0, 0)),
            s0_spec,
        ],
        out_specs=(row_spec(GLA_VAL_W),
                   pl.BlockSpec((nseg, GLA_HEADS, GLA_DK, GLA_DV), lambda g, t: (g, 0, 0, 0))),
        scratch_shapes=[pltpu.VMEM((nseg, GLA_HEADS, GLA_DK, GLA_DV), F32),
                        pltpu.VMEM((rows, GLA_KEY_W), F32)],
        compiler_params=_params(("parallel", "arbitrary")),
        name="gla",
    )(qk, qk, v, la_hi, la_lo, gate, gn.reshape(1, GLA_DV), s0)


def _pool_sample_kernel(buf_ref, u_ref, d_ref, new_ref, *, pos0):
    t_new = u_ref.shape[0]

    def ext(i, cs=slice(None)):
        return buf_ref[i, :, cs] if i < POOL_BUF else u_ref[i - POOL_BUF, :, cs]

    for r in range(POOL_BUF):
        new_ref[r] = ext(r + t_new)
    for t in range(t_new):
        cur = POOL_BUF + t
        for g, w in enumerate(POOL_WINDOWS):
            cs = slice(g * POOL_G, (g + 1) * POOL_G)
            win = ext(cur, cs)
            for j in range(1, w):
                win = win + ext(cur - j, cs)
            cnt = float(min(pos0 + t + 1, w))
            d_ref[t, :, cs] = win * (1.0 / cnt) - ext(cur, cs)


def pool_sample(buf_tm, u_tm, pos0, nb):
    nbuf, nseq, width = buf_tm.shape
    t_new = u_tm.shape[0]
    buf_spec = pl.BlockSpec((nbuf, nb, width), lambda i: (0, i, 0))
    new_spec = pl.BlockSpec((t_new, nb, width), lambda i: (0, i, 0))
    return pl.pallas_call(
        functools.partial(_pool_sample_kernel, pos0=pos0),
        out_shape=(jax.ShapeDtypeStruct(u_tm.shape, F32), jax.ShapeDtypeStruct(buf_tm.shape, F32)),
        grid=(nseq // nb,),
        in_specs=[buf_spec, new_spec],
        out_specs=(new_spec, buf_spec),
        compiler_params=_params(("parallel",)),
        name="pool_sample",
    )(buf_tm, u_tm)


def _mix_out_kernel(*refs, with_query):
    (o_ref, d_ref, siga_ref, sigb_ref, x_ref, wmix_ref, ps_ref, wa_ref, wb_ref, wo_ref) = refs[:10]
    h_ref = refs[12] if with_query else refs[10]
    branch_a = _dot(o_ref[...], wa_ref[...])
    pooled = []
    for g in range(len(POOL_WINDOWS)):
        cs = slice(g * POOL_G, (g + 1) * POOL_G)
        y = _dot(d_ref[:, cs].astype(BF16), wmix_ref[g]) * ps_ref[:, cs]
        pooled.append(y.astype(BF16))
    pooled = jnp.concatenate(pooled, axis=1)
    merged = (siga_ref[...].astype(F32) * branch_a
              + sigb_ref[...].astype(F32) * _dot(pooled, wb_ref[...]))
    h = x_ref[...] + _dot(merged.astype(BF16), wo_ref[...])
    h_ref[...] = h
    if with_query:
        gq_ref, wq_ref, q_ref = refs[10], refs[11], refs[13]
        q_ref[...] = _dot(_rmsnorm(h, gq_ref[...]).astype(BF16), wq_ref[...])


def mix_out(o, d, sig_a, sig_b, x, wmix, pscale, wa, wb, wo, *, tm, query=None):
    m, wide = x.shape
    row_spec = pl.BlockSpec((tm, wide), lambda i: (i, 0))
    const2 = lambda shape: pl.BlockSpec(shape, lambda i: (0, 0))
    in_specs = [row_spec] * 5 + [
        pl.BlockSpec((len(POOL_WINDOWS), POOL_G, POOL_G), lambda i: (0, 0, 0)),
        const2((1, wide)),
        const2((GLA_VAL_W, wide)),
        const2((wide, wide)),
        const2((wide, wide)),
    ]
    args = [o, d, sig_a, sig_b, x, wmix, pscale.reshape(1, wide), wa, wb, wo]
    out_shape = jax.ShapeDtypeStruct((m, wide), F32)
    out_specs = row_spec
    if query is not None:
        gq, wq = query
        in_specs += [const2((1, wide)), const2((wide, wide))]
        args += [gq.reshape(1, wide), wq]
        out_shape = (out_shape, out_shape)
        out_specs = (row_spec, row_spec)
    return pl.pallas_call(
        functools.partial(_mix_out_kernel, with_query=query is not None),
        out_shape=out_shape,
        grid=(m // tm,),
        in_specs=in_specs,
        out_specs=out_specs,
        compiler_params=_params(("parallel",)),
        name="mix_out",
    )(*args)


def _softmax_rows(s):
    p = jnp.exp(s - jnp.max(s, axis=-1, keepdims=True))
    return p, 1.0 / jnp.sum(p, axis=-1, keepdims=True)


def _xattn_prompt_kernel(h_ref, g_ref, wq_ref, mem_ref, gm_ref, wk_ref, wv_ref, wo_ref,
                         o_ref, mk_ref, mv_ref, kb_ref, vb_ref, *, tiles):
    groups = 2 * X_HEADS

    @pl.when(pl.program_id(0) % tiles == 0)
    def _():
        mn = _rmsnorm(mem_ref[0], gm_ref[...]).astype(BF16)
        k = _dot(mn, wk_ref[...])
        v = _dot(mn, wv_ref[...])
        kb_ref[...] = k.astype(BF16)
        vb_ref[...] = v.astype(BF16)
        mk_ref[0] = k.reshape(MEM_LEN, groups, LANE)
        mv_ref[0] = v.reshape(MEM_LEN, groups, LANE)

    def head(ref, hd):
        lo = ref[:, hd * LANE:(hd + 1) * LANE]
        hi = ref[:, (X_HEADS + hd) * LANE:(X_HEADS + hd + 1) * LANE]
        return jnp.concatenate([lo, hi], axis=1)

    h = h_ref[...]
    hn = _rmsnorm(h, g_ref[...]).astype(BF16)
    q = (_dot(hn, wq_ref[...]) * (X_HEAD_DIM ** -0.5)).astype(BF16)

    def scores(hd):
        return _dot_nt(q[:, hd * X_HEAD_DIM:(hd + 1) * X_HEAD_DIM], head(kb_ref, hd))

    outs = []
    s_next = scores(0)
    for hd in range(X_HEADS):
        s_cur = s_next
        if hd + 1 < X_HEADS:
            s_next = scores(hd + 1)
        p, inv = _softmax_rows(s_cur)
        outs.append((_dot(p.astype(BF16), head(vb_ref, hd)) * inv).astype(BF16))
    o = jnp.concatenate(outs, axis=1)
    o_ref[...] = h + _dot(o, wo_ref[...])


def xattn_prompt(h, g, wq, mem, gm, wk, wv, wo, *, tm):
    m, d = h.shape
    batch = mem.shape[0]
    tiles = m // batch // tm
    groups = 2 * X_HEADS
    row_spec = pl.BlockSpec((tm, d), lambda i: (i, 0))
    vec_spec = pl.BlockSpec((1, d), lambda i: (0, 0))
    w_spec = pl.BlockSpec((d, d), lambda i: (0, 0))
    kv_spec = pl.BlockSpec((1, MEM_LEN, groups, LANE), lambda i: (i // tiles, 0, 0, 0))
    kv_shape = jax.ShapeDtypeStruct((batch, MEM_LEN, groups, LANE), F32)
    return pl.pallas_call(
        functools.partial(_xattn_prompt_kernel, tiles=tiles),
        out_shape=(jax.ShapeDtypeStruct((m, d), F32), kv_shape, kv_shape),
        grid=(m // tm,),
        in_specs=[row_spec, vec_spec, w_spec,
                  pl.BlockSpec((1, MEM_LEN, d), lambda i: (i // tiles, 0, 0)), vec_spec,
                  w_spec, w_spec, w_spec],
        out_specs=(row_spec, kv_spec, kv_spec),
        scratch_shapes=[pltpu.VMEM((MEM_LEN, d), BF16), pltpu.VMEM((MEM_LEN, d), BF16)],
        compiler_params=_params(("arbitrary",)),
        name="xattn_prompt",
    )(h, g.reshape(1, d), wq, mem, gm.reshape(1, d), wk, wv, wo)


def _sample_attention_probs(q_ref, k_ref):
    nseq, krows, _ = k_ref.shape
    half = XQ_ROWS // 2
    lane = lax.broadcasted_iota(jnp.int32, (half, krows), 1)
    row = lax.broadcasted_iota(jnp.int32, (half, krows), 0)
    valid = (lane & 7) == (row >> 2)
    out = []
    for j in range(nseq):
        q = (q_ref[j * XQ_ROWS:(j + 1) * XQ_ROWS, :] * (X_HEAD_DIM ** -0.5)).astype(BF16)
        g = _dot_nt(q, k_ref[j].astype(BF16))
        s = g[:half] + pltpu.roll(g[half:], krows - 4, axis=1)
        s = jnp.where(valid, s, -1e30)
        p = jnp.exp(s - jnp.max(s, axis=-1, keepdims=True))
        inv = 1.0 / jnp.sum(p, axis=-1, keepdims=True)
        pe = jnp.concatenate([p, pltpu.roll(p, 4, axis=1)], axis=0).astype(BF16)
        out.append((pe, jnp.concatenate([inv, inv], axis=0)))
    return out


def _sample_attention_values(probs, v_ref, o_ref):
    for j, (pe, inv) in enumerate(probs):
        o = _dot(pe, v_ref[j].astype(BF16)) * inv
        o_ref[j * XQ_ROWS:(j + 1) * XQ_ROWS, :] = o.astype(o_ref.dtype)


def _mlp_kernel(*refs, with_attn, with_pre):
    (h_ref, g_ref, wu_ref, wd_ref, gf_ref) = refs[:5]
    rest = refs[5:]
    if with_attn:
        (q_ref, k_ref, v_ref), rest = rest[:3], rest[3:]
    if with_pre:
        (pre_ref, wpre_ref), rest = rest[:2], rest[2:]
    if with_attn:
        y_ref, a_ref, hn_ref, acc_ref = rest
    else:
        y_ref, hn_ref, acc_ref = rest
    f = pl.program_id(1)

    @pl.when(f == 0)
    def _():
        h = h_ref[...]
        if with_pre:
            h = h + _dot(pre_ref[...], wpre_ref[...])
        hn_ref[...] = _rmsnorm(h, g_ref[...]).astype(BF16)
        acc_ref[...] = h

    if with_attn:
        probs = _sample_attention_probs(q_ref, k_ref)
    a = jnp.maximum(_dot(hn_ref[...], wu_ref[...]), 0.0)
    if with_attn:
        _sample_attention_values(probs, v_ref, a_ref)
    acc_ref[...] += _dot((a * a).astype(BF16), wd_ref[...])

    @pl.when(f == pl.num_programs(1) - 1)
    def _():
        y_ref[...] = _rmsnorm(acc_ref[...], gf_ref[...])


def mlp_final(h, g, wu, wd, gf, *, tm, tf, attn=None, pre=None):
    m, d = h.shape
    ff = wu.shape[1]
    nf = ff // tf
    row_spec = pl.BlockSpec((tm, d), lambda i, f: (i, 0))
    vec_spec = pl.BlockSpec((1, d), lambda i, f: (0, 0))
    in_specs = [row_spec, vec_spec,
                pl.BlockSpec((d, tf), lambda i, f: (0, f)),
                pl.BlockSpec((tf, d), lambda i, f: (f, 0)),
                vec_spec]
    args = [h, g.reshape(1, d), wu, wd, gf.reshape(1, d)]
    out_shape = jax.ShapeDtypeStruct((m, d), F32)
    out_specs = row_spec
    if attn is not None:
        qhat, ck, cv = attn
        steps = (m // tm) * nf
        nseq = ck.shape[0] // steps
        assert nseq * steps == ck.shape[0]
        q_spec = pl.BlockSpec((nseq * XQ_ROWS, LANE), lambda i, f: (i * nf + f, 0))
        kv_spec = pl.BlockSpec((nseq,) + ck.shape[1:], lambda i, f: (i * nf + f, 0, 0))
        in_specs += [q_spec, kv_spec, kv_spec]
        args += [qhat, ck, cv]
        out_shape = (out_shape, jax.ShapeDtypeStruct(qhat.shape, BF16))
        out_specs = (row_spec, q_spec)
    if pre is not None:
        a, w_pre = pre
        in_specs += [pl.BlockSpec((tm, a.shape[1]), lambda i, f: (i, 0)),
                     pl.BlockSpec(w_pre.shape, lambda i, f: (0, 0))]
        args += [a, w_pre]
    return pl.pallas_call(
        functools.partial(_mlp_kernel, with_attn=attn is not None, with_pre=pre is not None),
        out_shape=out_shape,
        grid=(m // tm, nf),
        in_specs=in_specs,
        out_specs=out_specs,
        scratch_shapes=[pltpu.VMEM((tm, d), BF16), pltpu.VMEM((tm, d), F32)],
        compiler_params=_params(("parallel", "arbitrary")),
        name="mlp_final",
    )(*args)


def _prep_weights_kernel(win_ref, *refs):
    n = (len(refs) - 1) // 2
    in_refs, win_out, out_refs = refs[:n], refs[n], refs[n + 1:]
    gz_lo = 2 * GLA_KEY_W + GLA_VAL_W
    gz_hi = gz_lo + GLA_GATE_RANK
    gz_out = IN_COLS["gz"][0]
    win_out[:gz_lo, :] = win_ref[:gz_lo, :].astype(BF16)
    win_out[gz_lo:gz_out, :] = win_ref[gz_hi:, :].astype(BF16)
    win_out[gz_out:gz_out + GLA_GATE_RANK, :] = win_ref[gz_lo:gz_hi, :].astype(BF16)
    win_out[gz_out + GLA_GATE_RANK:, :] = jnp.zeros(
        (PROJ_W - gz_out - GLA_GATE_RANK, win_out.shape[1]), BF16)
    halves = X_HEAD_DIM // LANE
    for i_ref, o_ref in zip(in_refs, out_refs):
        for hd in range(X_HEADS):
            for c in range(halves):
                src = (hd * halves + c) * LANE
                dst = (c * X_HEADS + hd) * LANE
                o_ref[:, dst:dst + LANE] = i_ref[:, src:src + LANE].astype(BF16)


def prep_weights(w_in_t, others, nblk=8):
    def spec(shape):
        return pl.BlockSpec((None, shape[1] // nblk, shape[2]), lambda i: (0, i, 0))

    def out_spec(rows, cols):
        return pl.BlockSpec((rows // nblk, cols), lambda i: (i, 0))

    width, d = w_in_t.shape
    out_shapes = [jax.ShapeDtypeStruct((PROJ_W, d), BF16)]
    out_specs = [pl.BlockSpec((PROJ_W, d // nblk), lambda i: (0, i))]
    for w in others:
        out_shapes.append(jax.ShapeDtypeStruct(w.shape[1:], BF16))
        out_specs.append(out_spec(*w.shape[1:]))
    return pl.pallas_call(
        _prep_weights_kernel,
        out_shape=tuple(out_shapes),
        grid=(nblk,),
        in_specs=[pl.BlockSpec((width, d // nblk), lambda i: (0, i))] + [spec(w.shape) for w in others],
        out_specs=tuple(out_specs),
        compiler_params=_params(("parallel",)),
        name="prep_weights",
    )(w_in_t, *others)


def _cache_rows(c):
    b, m, h, dh = c.shape
    return c.reshape(b, m, h, dh // LANE, LANE).transpose(0, 1, 3, 2, 4).reshape(b, m * h * (dh // LANE), LANE)


def _mem_kv_output(kv):
    b, m, _, _ = kv.shape
    kv = kv.reshape(b, m, X_HEAD_DIM // LANE, X_HEADS, LANE).transpose(0, 1, 3, 2, 4)
    return kv.reshape(1, b, m, X_HEADS, X_HEAD_DIM)


def kernel(x_prompt, x_sample, mem_prompt, state_gla, state_pool, cache_mem_k, cache_mem_v,
           norm_mix_g, w_in, w_gk_up, b_gk, gla_norm_g, w_pool_mix, pool_scale,
           w_branch_a, w_branch_b, w_out, norm_x_g, norm_mem_g, w_xq, w_xk, w_xv, w_xo,
           norm_mlp_g, w_up, w_down, norm_final_g):
    depth = w_in.shape[0]
    assert depth == 1
    batch, seq, d = x_prompt.shape
    dec_batch, dec_seq, _ = x_sample.shape
    mp = batch * seq
    ms = dec_batch * dec_seq

    w_in_r, wxk, wxv = prep_weights(w_in[0].T, (w_xk, w_xv))
    wgk = jnp.concatenate(
        [w_gk_up[0], jnp.zeros((LANE - GLA_GATE_RANK, GLA_KEY_W), F32)], axis=0).astype(BF16)
    wmix = w_pool_mix[0].astype(BF16)

    xp = x_prompt.reshape(mp, d)
    xs = x_sample.reshape(ms, d)

    (qk_p, v_p, gate_p, u_p, sa_p, sb_p, lah_p, lal_p, d_p, wa, wb, wo, wxq, wxo, wu, wd) = in_proj(
        xp, norm_mix_g[0], w_in_r, wgk, b_gk[0], TM_PROJ,
        cast=(w_branch_a, w_branch_b, w_out, w_xq, w_xo, w_up, w_down), pool_seq_len=seq)

    qk_s, v_s, gate_s, u_s, sa_s, sb_s, lah_s, lal_s = in_proj(xs, norm_mix_g[0], w_in_r, wgk, b_gk[0], ms)
    gla_rows = SAMPLE_SEQS_PER_GLA_STEP * dec_seq
    o_s, sg_s = gla(qk_s, v_s, lah_s, lal_s, gate_s, gla_norm_g[0], state_gla[0],
                    groups=ms // gla_rows, tt=gla_rows, rows=gla_rows, seg=dec_seq)
    d_tm, sp_tm = pool_sample(state_pool[0].transpose(1, 0, 2),
                              u_s.reshape(dec_batch, dec_seq, d).transpose(1, 0, 2),
                              PAST_LEN, SAMPLE_SEQS_PER_POOL_STEP)
    h_s, q_s = mix_out(o_s, d_tm.transpose(1, 0, 2).reshape(ms, d), sa_s, sb_s, xs, wmix,
                       pool_scale[0], wa, wb, wo, tm=ms, query=(norm_x_g[0], wxq))
    halves = X_HEAD_DIM // LANE
    qhat = q_s.reshape(dec_batch, dec_seq, X_HEADS, halves, LANE).transpose(0, 3, 2, 1, 4)
    qhat = qhat.reshape(dec_batch * XQ_ROWS, LANE)

    o_p, sg_p = gla(qk_p, v_p, lah_p, lal_p, gate_p, gla_norm_g[0], None,
                    groups=batch, tt=TT_GLA, rows=GLA_CHUNK, seg=GLA_CHUNK)
    h_p = mix_out(o_p, d_p, sa_p, sb_p, xp, wmix, pool_scale[0], wa, wb, wo, tm=TM_MIX)
    h_p, mk_p, mv_p = xattn_prompt(h_p, norm_x_g[0], wxq, mem_prompt, norm_mem_g[0], wxk, wxv, wxo,
                                   tm=TM_XATTN)
    y_p, a_s = mlp_final(h_p, norm_mlp_g[0], wu, wd, norm_final_g, tm=TM_MLP, tf=TF_MLP,
                         attn=(qhat, _cache_rows(cache_mem_k[0]), _cache_rows(cache_mem_v[0])))
    sp_p = u_p.reshape(batch, HALO, d)[:, HALO - POOL_BUF:]

    a_s = a_s.reshape(dec_batch, halves, X_HEADS, dec_seq, LANE).transpose(0, 3, 2, 1, 4).reshape(ms, d)
    y_s = mlp_final(h_s, norm_mlp_g[0], wu, wd, norm_final_g, tm=ms, tf=TF_MLP, pre=(a_s, wxo))
    sp_s = sp_tm.transpose(1, 0, 2)

    return (y_p.reshape(batch, seq, d),
            y_s.reshape(dec_batch, dec_seq, d),
            _mem_kv_output(mk_p),
            _mem_kv_output(mv_p),
            sg_p[None],
            sg_s[None],
            sp_p[None],
            sp_s[None])
```

```python
import functools

import jax
import jax.numpy as jnp
from jax import lax
from jax.experimental import pallas as pl
from jax.experimental.pallas import tpu as pltpu

F32 = jnp.float32
BF16 = jnp.bfloat16

D_MODEL = 1024
GLA_HEADS = 4
GLA_DK = 128
GLA_DV = 256
GLA_KEY_W = GLA_HEADS * GLA_DK
GLA_VAL_W = GLA_HEADS * GLA_DV
GLA_GATE_RANK = 16
GLA_GATE_NORM = 16.0
POOL_WINDOWS = (2, 4, 8, 16)
POOL_G = 256
POOL_BUF = 15
MEM_LEN = 256
X_HEADS = 4
X_HEAD_DIM = 256
EPS = 1e-6
PAST_LEN = 16384

LANE = 128
HALO = 16
VMEM_LIMIT = 52 * 1024 * 1024
XQ_ROWS = 32

IN_COLS = {
    "qk": (0, 2 * GLA_KEY_W),
    "v": (1024, GLA_VAL_W),
    "og": (2048, GLA_VAL_W),
    "u": (3072, D_MODEL),
    "ga": (4096, D_MODEL),
    "gb": (5120, D_MODEL),
    "gz": (6144, LANE),
}
PROJ_W = 6272

TM_PROJ = 512
TM_MIX = 512
TM_XATTN = 1024
TM_MLP = 1024
TF_MLP = 1024
TT_GLA = 1024
GLA_CHUNK = 128
GLA_FAST_RANGE = 60.0
SAMPLE_SEQS_PER_GLA_STEP = 16
SAMPLE_SEQS_PER_POOL_STEP = 32


def _params(sem):
    return pltpu.CompilerParams(dimension_semantics=sem, vmem_limit_bytes=VMEM_LIMIT)


def _rmsnorm(x, g):
    return x * lax.rsqrt(jnp.mean(x * x, axis=-1, keepdims=True) + EPS) * g


def _dot(a, b):
    return jnp.dot(a, b, preferred_element_type=F32)


def _dot_nt(a, b):
    return lax.dot_general(a, b, (((1,), (1,)), ((), ())), preferred_element_type=F32)


def _dot_tn(a, b):
    return lax.dot_general(a, b, (((0,), (0,)), ((), ())), preferred_element_type=F32)


def _log_decay_split(z):
    la = (jnp.minimum(z, 0.0) - jnp.log(1.0 + jnp.exp(-jnp.abs(z)))) * (1.0 / GLA_GATE_NORM)
    la_hi = la.astype(BF16)
    return la_hi, (la - la_hi.astype(F32)).astype(BF16)


def _pool_diff(u, halo, t_in_seq):
    tm = u.shape[0]
    halo = jnp.where(t_in_seq == 0, 0.0, halo)
    pos1 = (t_in_seq * tm + 1 + lax.broadcasted_iota(jnp.int32, (tm, 1), 0)).astype(F32)
    out = []
    for g, w in enumerate(POOL_WINDOWS):
        cs = slice(g * POOL_G, (g + 1) * POOL_G)
        s = jnp.concatenate([halo[:, cs], u[:, cs]], axis=0)
        shift = 1
        while shift < w:
            s = s + pltpu.roll(s, shift, axis=0)
            shift *= 2
        out.append(s[HALO:, :] * (1.0 / jnp.minimum(pos1, float(w))) - u[:, cs])
    return jnp.concatenate(out, axis=1)


def _cast_head_major_to_half_major(i_ref, o_ref):
    halves = X_HEAD_DIM // LANE
    for hd in range(X_HEADS):
        for c in range(halves):
            src = (hd * halves + c) * LANE
            dst = (c * X_HEADS + hd) * LANE
            o_ref[:, dst:dst + LANE] = i_ref[:, src:src + LANE].astype(BF16)


def _in_proj_kernel(*refs, ncast, nregroup, tiles_per_seq):
    pool = tiles_per_seq is not None
    nout = 9 if pool else 8
    (x_ref, g_ref, w_ref, wgk_ref, bgk_ref) = refs[:5]
    cast_in = refs[5:5 + ncast]
    outs = refs[5 + ncast:5 + ncast + nout]
    (qk_ref, v_ref, gate_ref, u_ref, ga_ref, gb_ref, lahi_ref, lalo_ref) = outs[:8]
    cast_out = refs[5 + ncast + nout:5 + 2 * ncast + nout]
    for n, (i_ref, o_ref) in enumerate(zip(cast_in, cast_out)):
        if n >= ncast - nregroup:
            _cast_head_major_to_half_major(i_ref, o_ref)
        else:
            o_ref[...] = i_ref[...].astype(BF16)
    xn = _rmsnorm(x_ref[...], g_ref[...]).astype(BF16)

    def piece(name):
        col, width = IN_COLS[name]
        return _dot_nt(xn, w_ref[col:col + width, :])

    u = piece("u")
    if pool:
        d_ref, tail_ref = outs[8], refs[-1]
        tm = u.shape[0]
        d_ref[...] = _pool_diff(u, tail_ref[...], pl.program_id(0) % tiles_per_seq).astype(BF16)
        tail_ref[...] = u[tm - HALO:, :]
        u_ref[...] = u[tm - HALO:, :]
    else:
        u_ref[...] = u
    og = piece("og")
    gate_ref[...] = (og * jax.nn.sigmoid(og)).astype(BF16)
    ga_ref[...] = piece("ga").astype(BF16)
    gz = piece("gz").astype(BF16)
    gb_ref[...] = piece("gb").astype(BF16)
    z = _dot(gz, wgk_ref[...]) + bgk_ref[...]
    lahi_ref[...], lalo_ref[...] = _log_decay_split(z)
    qk_ref[...] = piece("qk")
    v_ref[...] = piece("v").astype(BF16)


def in_proj(x, g, w, wgk, bgk, tm, cast=(), cast_kv=(), pool_seq_len=None):
    m, k = x.shape
    steps = m // tm
    outs = [(2 * GLA_KEY_W, F32), (GLA_VAL_W, BF16), (GLA_VAL_W, BF16), (D_MODEL, F32),
            (D_MODEL, BF16), (D_MODEL, BF16), (GLA_KEY_W, BF16), (GLA_KEY_W, BF16)]
    scratch = []
    tiles_per_seq = None
    if pool_seq_len is not None:
        outs.append((D_MODEL, BF16))
        scratch.append(pltpu.VMEM((HALO, D_MODEL), F32))
        tiles_per_seq = pool_seq_len // tm
    const = lambda shape: pl.BlockSpec(shape, lambda i: (0, 0))
    out_shape = [jax.ShapeDtypeStruct((m, width), dtype) for width, dtype in outs]
    out_specs = [pl.BlockSpec((tm, width), lambda i: (i, 0)) for width, _ in outs]
    if pool_seq_len is not None:
        u_index = 3
        out_shape[u_index] = jax.ShapeDtypeStruct((m // pool_seq_len * HALO, D_MODEL), F32)
        out_specs[u_index] = pl.BlockSpec((HALO, D_MODEL), lambda i: (i // tiles_per_seq, 0))
    in_specs = [pl.BlockSpec((tm, k), lambda i: (i, 0)), const((1, k)), const((PROJ_W, k)),
                const((LANE, GLA_KEY_W)), const((1, GLA_KEY_W))]
    cast = tuple(cast) + tuple(cast_kv)
    for wc in cast:
        _, rows, cols = wc.shape
        in_specs.append(pl.BlockSpec((None, rows // steps, cols), lambda i: (0, i, 0)))
        out_specs.append(pl.BlockSpec((rows // steps, cols), lambda i: (i, 0)))
        out_shape.append(jax.ShapeDtypeStruct((rows, cols), BF16))
    return pl.pallas_call(
        functools.partial(_in_proj_kernel, ncast=len(cast), nregroup=len(cast_kv),
                          tiles_per_seq=tiles_per_seq),
        out_shape=tuple(out_shape),
        grid=(steps,),
        in_specs=in_specs,
        out_specs=tuple(out_specs),
        scratch_shapes=scratch,
        compiler_params=_params(("arbitrary",)),
        name="in_proj",
    )(x, g.reshape(1, k), w, wgk, bgk.reshape(1, GLA_KEY_W), *cast)


def _gla_kernel(q_ref, k_ref, v_ref, lahi_ref, lalo_ref, gate_ref, gn_ref, s0_ref,
                o_ref, s_ref, b_ref, *maybe_s_in_ref, rows, seg, zero_init):
    tt = q_ref.shape[0]
    nchunks = tt // rows
    nseg = rows // seg
    seg_shift = seg.bit_length() - 1

    @pl.when(pl.program_id(1) == 0)
    def _():
        if zero_init:
            s_ref[...] = jnp.zeros(s_ref.shape, F32)
        else:
            s_ref[...] = s0_ref[...]

    if maybe_s_in_ref:
        s_in_ref, = maybe_s_in_ref
        s_in_ref[...] = s_ref[...]
    else:
        s_in_ref = s0_ref

    ri = lax.broadcasted_iota(jnp.int32, (rows, rows), 0)
    ci = lax.broadcasted_iota(jnp.int32, (rows, rows), 1)
    same_seg = (ri >> seg_shift) == (ci >> seg_shift)
    causal = jnp.logical_and(same_seg, ci <= ri)
    l_cum = jnp.where(causal, 1.0, 0.0).astype(BF16)
    l_seg = jnp.where(same_seg, 1.0, 0.0).astype(BF16)
    row_seg = lax.broadcasted_iota(jnp.int32, (rows, GLA_DK), 0) >> seg_shift
    row_seg_v = lax.broadcasted_iota(jnp.int32, (rows, GLA_DV), 0) >> seg_shift
    gn = gn_ref[...]
    qscale = GLA_DK ** -0.5

    def cumulative(c):
        rsl = slice(c * rows, (c + 1) * rows)
        lah = lahi_ref[rsl, :]
        lal = lalo_ref[rsl, :]
        b = _dot(l_cum, lah) + _dot(l_cum, lal)
        if nseg == 1:
            b_end = b[rows - 1:rows, :]
        else:
            b_end = _dot(l_seg, lah) + _dot(l_seg, lal)
        return lah, lal, b, b_end

    def pairwise_scores(c, ks, qs_h, b_h):
        sub = 8

        def columns(blk, acc):
            r0 = pl.multiple_of(blk * sub, sub)
            b_blk = b_ref[pl.ds(r0, sub), ks]
            k_blk = k_ref[pl.ds(c * rows + r0, sub), ks]
            for j in range(sub):
                e = jnp.exp(jnp.minimum(b_h - b_blk[j:j + 1, :], 0.0))
                col = jnp.sum(qs_h * e * k_blk[j:j + 1, :], axis=1, keepdims=True)
                acc = acc + jnp.where(ci == r0 + j, col, 0.0)
            return acc

        return lax.fori_loop(0, rows // sub, columns, jnp.zeros((rows, rows), F32))

    def run(exact):
        worst = jnp.float32(0.0)
        ahead = cumulative(0)
        for c in range(nchunks):
            rsl = slice(c * rows, (c + 1) * rows)
            lah, lal, b, b_end = ahead
            if c + 1 < nchunks:
                ahead = cumulative(c + 1)
            qs = q_ref[rsl, :] * qscale
            k = k_ref[rsl, :]
            qt_b = (qs * jnp.exp(b)).astype(BF16)
            kd = k * jnp.exp(b_end - b)
            if exact:
                b_ref[...] = b
            else:
                kt_b = (k * jnp.exp(-b)).astype(BF16)
                worst = jnp.maximum(worst, jnp.max(-b_end))
            heads = []
            for h in range(GLA_HEADS):
                ks = slice(h * GLA_DK, (h + 1) * GLA_DK)
                v_h = v_ref[rsl, h * GLA_DV:(h + 1) * GLA_DV]
                if exact:
                    scores = pairwise_scores(c, ks, qs[:, ks], b[:, ks])
                else:
                    scores = _dot_nt(qt_b[:, ks], kt_b[:, ks])
                inter = None
                new_states = []
                for j in range(nseg):
                    s_old = s_ref[j, h]
                    inter_j = _dot(qt_b[:, ks], s_old.astype(BF16))
                    if nseg == 1:
                        inter = inter_j
                        kd_j = kd[:, ks].astype(BF16)
                        dcol = jnp.broadcast_to(b_end[:, ks], (GLA_DK, GLA_DK)).T
                    else:
                        inter_j = jnp.where(row_seg_v == j, inter_j, 0.0)
                        inter = inter_j if inter is None else inter + inter_j
                        kd_j = jnp.where(row_seg == j, kd[:, ks], 0.0).astype(BF16)
                        ones_j = jnp.where(row_seg == j, 1.0, 0.0).astype(BF16)
                        dcol = _dot_tn(lah[:, ks], ones_j) + _dot_tn(lal[:, ks], ones_j)
                    new_states.append((s_old, dcol, _dot_tn(kd_j, v_h)))
                heads.append((v_h, scores, inter, new_states))
            outs = []
            for v_h, scores, inter, _ in heads:
                a = jnp.where(causal, scores, 0.0).astype(BF16)
                outs.append(_dot(a, v_h) + inter)
            for h, (_, _, _, new_states) in enumerate(heads):
                vs = slice(h * GLA_DV, (h + 1) * GLA_DV)
                for j, (s_old, dcol, upd) in enumerate(new_states):
                    e = jnp.exp(dcol)
                    s_ref[j, h] = s_old * jnp.concatenate([e, e], axis=1) + upd
                on = _rmsnorm(outs[h], gn)
                o_ref[rsl, vs] = (on * gate_ref[rsl, vs].astype(F32)).astype(o_ref.dtype)
        return worst

    worst = run(exact=False)

    @pl.when(worst > GLA_FAST_RANGE)
    def _():
        s_ref[...] = s_in_ref[...]
        run(exact=True)


def gla(qk, v, la_hi, la_lo, gate, gn, s0, *, groups, tt, rows, seg):
    m = qk.shape[0]
    steps = m // (groups * tt)
    nseg = rows // seg
    zero_init = s0 is None
    if zero_init:
        s0 = jnp.zeros((nseg, GLA_HEADS, 8, LANE), F32)
        s0_spec = pl.BlockSpec((nseg, GLA_HEADS, 8, LANE), lambda g, t: (0, 0, 0, 0))
    else:
        s0_spec = pl.BlockSpec((nseg, GLA_HEADS, GLA_DK, GLA_DV), lambda g, t: (g, 0, 0, 0))

    def row_spec(width, blk=0):
        return pl.BlockSpec((tt, width), lambda g, t: (g * steps + t, blk))

    scratch = [pltpu.VMEM((rows, GLA_KEY_W), F32)]
    if zero_init or steps > 1:
        scratch.append(pltpu.VMEM((nseg, GLA_HEADS, GLA_DK, GLA_DV), F32))
    kern = functools.partial(_gla_kernel, rows=rows, seg=seg, zero_init=zero_init)
    return pl.pallas_call(
        kern,
        out_shape=(jax.ShapeDtypeStruct((m, GLA_VAL_W), BF16),
                   jax.ShapeDtypeStruct((groups * nseg, GLA_HEADS, GLA_DK, GLA_DV), F32)),
        grid=(groups, steps),
        in_specs=[
            row_spec(GLA_KEY_W, 0),
            row_spec(GLA_KEY_W, 1),
            row_spec(GLA_VAL_W),
            row_spec(GLA_KEY_W),
            row_spec(GLA_KEY_W),
            row_spec(GLA_VAL_W),
            pl.BlockSpec((1, GLA_DV), lambda g, t: (0, 0)),
            s0_spec,
        ],
        out_specs=(row_spec(GLA_VAL_W),
                   pl.BlockSpec((nseg, GLA_HEADS, GLA_DK, GLA_DV), lambda g, t: (g, 0, 0, 0))),
        scratch_shapes=scratch,
        compiler_params=_params(("parallel", "arbitrary")),
        name="gla",
    )(qk, qk, v, la_hi, la_lo, gate, gn.reshape(1, GLA_DV), s0)


def _pool_sample_kernel(buf_ref, u_ref, d_ref, new_ref, *, pos0):
    t_new = u_ref.shape[0]

    def ext(i, cs=slice(None)):
        return buf_ref[i, :, cs] if i < POOL_BUF else u_ref[i - POOL_BUF, :, cs]

    for r in range(POOL_BUF):
        new_ref[r] = ext(r + t_new)
    for t in range(t_new):
        cur = POOL_BUF + t
        for g, w in enumerate(POOL_WINDOWS):
            cs = slice(g * POOL_G, (g + 1) * POOL_G)
            win = ext(cur, cs)
            for j in range(1, w):
                win = win + ext(cur - j, cs)
            cnt = float(min(pos0 + t + 1, w))
            d_ref[t, :, cs] = win * (1.0 / cnt) - ext(cur, cs)


def pool_sample(buf_tm, u_tm, pos0, nb):
    nbuf, nseq, width = buf_tm.shape
    t_new = u_tm.shape[0]
    buf_spec = pl.BlockSpec((nbuf, nb, width), lambda i: (0, i, 0))
    new_spec = pl.BlockSpec((t_new, nb, width), lambda i: (0, i, 0))
    return pl.pallas_call(
        functools.partial(_pool_sample_kernel, pos0=pos0),
        out_shape=(jax.ShapeDtypeStruct(u_tm.shape, F32), jax.ShapeDtypeStruct(buf_tm.shape, F32)),
        grid=(nseq // nb,),
        in_specs=[buf_spec, new_spec],
        out_specs=(new_spec, buf_spec),
        compiler_params=_params(("parallel",)),
        name="pool_sample",
    )(buf_tm, u_tm)


def _mix_out_kernel(*refs, with_query):
    (o_ref, d_ref, ga_ref, gb_ref, x_ref, wmix_ref, ps_ref, wa_ref, wb_ref, wo_ref) = refs[:10]
    h_ref = refs[12] if with_query else refs[10]
    branch_a = _dot(o_ref[...], wa_ref[...])
    pooled = []
    for g in range(len(POOL_WINDOWS)):
        cs = slice(g * POOL_G, (g + 1) * POOL_G)
        y = _dot(d_ref[:, cs].astype(BF16), wmix_ref[g]) * ps_ref[:, cs]
        pooled.append(y.astype(BF16))
    pooled = jnp.concatenate(pooled, axis=1)
    merged = (jax.nn.sigmoid(ga_ref[...].astype(F32)) * branch_a
              + jax.nn.sigmoid(gb_ref[...].astype(F32)) * _dot(pooled, wb_ref[...]))
    h = x_ref[...] + _dot(merged.astype(BF16), wo_ref[...])
    h_ref[...] = h
    if with_query:
        gq_ref, wq_ref, q_ref = refs[10], refs[11], refs[13]
        q_ref[...] = _dot(_rmsnorm(h, gq_ref[...]).astype(BF16), wq_ref[...])


def mix_out(o, d, ga, gb, x, wmix, pscale, wa, wb, wo, *, tm, query=None):
    m, wide = x.shape
    row_spec = pl.BlockSpec((tm, wide), lambda i: (i, 0))
    const2 = lambda shape: pl.BlockSpec(shape, lambda i: (0, 0))
    in_specs = [row_spec] * 5 + [
        pl.BlockSpec((len(POOL_WINDOWS), POOL_G, POOL_G), lambda i: (0, 0, 0)),
        const2((1, wide)),
        const2((GLA_VAL_W, wide)),
        const2((wide, wide)),
        const2((wide, wide)),
    ]
    args = [o, d, ga, gb, x, wmix, pscale.reshape(1, wide), wa, wb, wo]
    out_shape = jax.ShapeDtypeStruct((m, wide), F32)
    out_specs = row_spec
    if query is not None:
        gq, wq = query
        in_specs += [const2((1, wide)), const2((wide, wide))]
        args += [gq.reshape(1, wide), wq]
        out_shape = (out_shape, out_shape)
        out_specs = (row_spec, row_spec)
    return pl.pallas_call(
        functools.partial(_mix_out_kernel, with_query=query is not None),
        out_shape=out_shape,
        grid=(m // tm,),
        in_specs=in_specs,
        out_specs=out_specs,
        compiler_params=_params(("parallel",)),
        name="mix_out",
    )(*args)


def _softmax_rows(s):
    p = jnp.exp(s - jnp.max(s, axis=-1, keepdims=True))
    return p, 1.0 / jnp.sum(p, axis=-1, keepdims=True)


def _xattn_prompt_kernel(h_ref, g_ref, wq_ref, mem_ref, gm_ref, wk_ref, wv_ref, wo_ref,
                         o_ref, mk_ref, mv_ref, kb_ref, vb_ref, *, tiles):
    groups = 2 * X_HEADS

    @pl.when(pl.program_id(0) % tiles == 0)
    def _():
        mn = _rmsnorm(mem_ref[0], gm_ref[...]).astype(BF16)
        k = _dot(mn, wk_ref[...])
        v = _dot(mn, wv_ref[...])
        kb_ref[...] = k.astype(BF16)
        vb_ref[...] = v.astype(BF16)
        mk_ref[0] = k.reshape(MEM_LEN, groups, LANE)
        mv_ref[0] = v.reshape(MEM_LEN, groups, LANE)

    def head(ref, hd):
        lo = ref[:, hd * LANE:(hd + 1) * LANE]
        hi = ref[:, (X_HEADS + hd) * LANE:(X_HEADS + hd + 1) * LANE]
        return jnp.concatenate([lo, hi], axis=1)

    h = h_ref[...]
    hn = _rmsnorm(h, g_ref[...]).astype(BF16)
    q = (_dot(hn, wq_ref[...]) * (X_HEAD_DIM ** -0.5)).astype(BF16)

    def scores(hd):
        return _dot_nt(q[:, hd * X_HEAD_DIM:(hd + 1) * X_HEAD_DIM], head(kb_ref, hd))

    outs = []
    s_next = scores(0)
    for hd in range(X_HEADS):
        s_cur = s_next
        if hd + 1 < X_HEADS:
            s_next = scores(hd + 1)
        p, inv = _softmax_rows(s_cur)
        outs.append((_dot(p.astype(BF16), head(vb_ref, hd)) * inv).astype(BF16))
    o = jnp.concatenate(outs, axis=1)
    o_ref[...] = h + _dot(o, wo_ref[...])


def xattn_prompt(h, g, wq, mem, gm, wk, wv, wo, *, tm):
    m, d = h.shape
    batch = mem.shape[0]
    tiles = m // batch // tm
    groups = 2 * X_HEADS
    row_spec = pl.BlockSpec((tm, d), lambda i: (i, 0))
    vec_spec = pl.BlockSpec((1, d), lambda i: (0, 0))
    w_spec = pl.BlockSpec((d, d), lambda i: (0, 0))
    kv_spec = pl.BlockSpec((1, MEM_LEN, groups, LANE), lambda i: (i // tiles, 0, 0, 0))
    kv_shape = jax.ShapeDtypeStruct((batch, MEM_LEN, groups, LANE), F32)
    return pl.pallas_call(
        functools.partial(_xattn_prompt_kernel, tiles=tiles),
        out_shape=(jax.ShapeDtypeStruct((m, d), F32), kv_shape, kv_shape),
        grid=(m // tm,),
        in_specs=[row_spec, vec_spec, w_spec,
                  pl.BlockSpec((1, MEM_LEN, d), lambda i: (i // tiles, 0, 0)), vec_spec,
                  w_spec, w_spec, w_spec],
        out_specs=(row_spec, kv_spec, kv_spec),
        scratch_shapes=[pltpu.VMEM((MEM_LEN, d), BF16), pltpu.VMEM((MEM_LEN, d), BF16)],
        compiler_params=_params(("arbitrary",)),
        name="xattn_prompt",
    )(h, g.reshape(1, d), wq, mem, gm.reshape(1, d), wk, wv, wo)


def _sample_attention_probs(q_ref, k_ref):
    nseq, krows, _ = k_ref.shape
    half = XQ_ROWS // 2
    lane = lax.broadcasted_iota(jnp.int32, (half, krows), 1)
    row = lax.broadcasted_iota(jnp.int32, (half, krows), 0)
    valid = (lane & 7) == (row >> 2)
    out = []
    for j in range(nseq):
        q = (q_ref[j * XQ_ROWS:(j + 1) * XQ_ROWS, :] * (X_HEAD_DIM ** -0.5)).astype(BF16)
        g = _dot_nt(q, k_ref[j].astype(BF16))
        s = g[:half] + pltpu.roll(g[half:], krows - 4, axis=1)
        s = jnp.where(valid, s, -1e30)
        p = jnp.exp(s - jnp.max(s, axis=-1, keepdims=True))
        inv = 1.0 / jnp.sum(p, axis=-1, keepdims=True)
        pe = jnp.concatenate([p, pltpu.roll(p, 4, axis=1)], axis=0).astype(BF16)
        out.append((pe, jnp.concatenate([inv, inv], axis=0)))
    return out


def _sample_attention_values(probs, v_ref, o_ref):
    for j, (pe, inv) in enumerate(probs):
        o = _dot(pe, v_ref[j].astype(BF16)) * inv
        o_ref[j * XQ_ROWS:(j + 1) * XQ_ROWS, :] = o.astype(o_ref.dtype)


def _mlp_kernel(*refs, with_attn, with_pre):
    (h_ref, g_ref, wu_ref, wd_ref, gf_ref) = refs[:5]
    rest = refs[5:]
    if with_attn:
        (q_ref, k_ref, v_ref), rest = rest[:3], rest[3:]
    if with_pre:
        (pre_ref, wpre_ref), rest = rest[:2], rest[2:]
    if with_attn:
        y_ref, a_ref, hn_ref, acc_ref = rest
    else:
        y_ref, hn_ref, acc_ref = rest
    f = pl.program_id(1)

    @pl.when(f == 0)
    def _():
        h = h_ref[...]
        if with_pre:
            h = h + _dot(pre_ref[...], wpre_ref[...])
        hn_ref[...] = _rmsnorm(h, g_ref[...]).astype(BF16)
        acc_ref[...] = h

    if with_attn:
        probs = _sample_attention_probs(q_ref, k_ref)
    a = jnp.maximum(_dot(hn_ref[...], wu_ref[...]), 0.0)
    if with_attn:
        _sample_attention_values(probs, v_ref, a_ref)
    acc_ref[...] += _dot((a * a).astype(BF16), wd_ref[...])

    @pl.when(f == pl.num_programs(1) - 1)
    def _():
        y_ref[...] = _rmsnorm(acc_ref[...], gf_ref[...])


def mlp_final(h, g, wu, wd, gf, *, tm, tf, attn=None, pre=None):
    m, d = h.shape
    ff = wu.shape[1]
    nf = ff // tf
    row_spec = pl.BlockSpec((tm, d), lambda i, f: (i, 0))
    vec_spec = pl.BlockSpec((1, d), lambda i, f: (0, 0))
    in_specs = [row_spec, vec_spec,
                pl.BlockSpec((d, tf), lambda i, f: (0, f)),
                pl.BlockSpec((tf, d), lambda i, f: (f, 0)),
                vec_spec]
    args = [h, g.reshape(1, d), wu, wd, gf.reshape(1, d)]
    out_shape = jax.ShapeDtypeStruct((m, d), F32)
    out_specs = row_spec
    if attn is not None:
        qhat, ck, cv = attn
        steps = (m // tm) * nf
        nseq = ck.shape[0] // steps
        assert nseq * steps == ck.shape[0]
        q_spec = pl.BlockSpec((nseq * XQ_ROWS, LANE), lambda i, f: (i * nf + f, 0))
        kv_spec = pl.BlockSpec((nseq,) + ck.shape[1:], lambda i, f: (i * nf + f, 0, 0))
        in_specs += [q_spec, kv_spec, kv_spec]
        args += [qhat, ck, cv]
        out_shape = (out_shape, jax.ShapeDtypeStruct(qhat.shape, BF16))
        out_specs = (row_spec, q_spec)
    if pre is not None:
        a, w_pre = pre
        in_specs += [pl.BlockSpec((tm, a.shape[1]), lambda i, f: (i, 0)),
                     pl.BlockSpec(w_pre.shape, lambda i, f: (0, 0))]
        args += [a, w_pre]
    return pl.pallas_call(
        functools.partial(_mlp_kernel, with_attn=attn is not None, with_pre=pre is not None),
        out_shape=out_shape,
        grid=(m // tm, nf),
        in_specs=in_specs,
        out_specs=out_specs,
        scratch_shapes=[pltpu.VMEM((tm, d), BF16), pltpu.VMEM((tm, d), F32)],
        compiler_params=_params(("parallel", "arbitrary")),
        name="mlp_final",
    )(*args)


def _prep_w_in_kernel(win_ref, win_out):
    gz_lo = 2 * GLA_KEY_W + GLA_VAL_W
    gz_hi = gz_lo + GLA_GATE_RANK
    gz_out = IN_COLS["gz"][0]
    win_out[:gz_lo, :] = win_ref[:gz_lo, :].astype(BF16)
    win_out[gz_lo:gz_out, :] = win_ref[gz_hi:, :].astype(BF16)
    win_out[gz_out:gz_out + GLA_GATE_RANK, :] = win_ref[gz_lo:gz_hi, :].astype(BF16)
    win_out[gz_out + GLA_GATE_RANK:, :] = jnp.zeros(
        (PROJ_W - gz_out - GLA_GATE_RANK, win_out.shape[1]), BF16)


def prep_w_in(w_in_t, nblk=8):
    width, d = w_in_t.shape
    return pl.pallas_call(
        _prep_w_in_kernel,
        out_shape=jax.ShapeDtypeStruct((PROJ_W, d), BF16),
        grid=(nblk,),
        in_specs=[pl.BlockSpec((width, d // nblk), lambda i: (0, i))],
        out_specs=pl.BlockSpec((PROJ_W, d // nblk), lambda i: (0, i)),
        compiler_params=_params(("parallel",)),
        name="prep_w_in",
    )(w_in_t)


def _cache_rows(c):
    b, m, h, dh = c.shape
    return c.reshape(b, m, h, dh // LANE, LANE).transpose(0, 1, 3, 2, 4).reshape(b, m * h * (dh // LANE), LANE)


def _mem_kv_output(kv):
    b, m, _, _ = kv.shape
    kv = kv.reshape(b, m, X_HEAD_DIM // LANE, X_HEADS, LANE).transpose(0, 1, 3, 2, 4)
    return kv.reshape(1, b, m, X_HEADS, X_HEAD_DIM)


def kernel(x_prompt, x_sample, mem_prompt, state_gla, state_pool, cache_mem_k, cache_mem_v,
           norm_mix_g, w_in, w_gk_up, b_gk, gla_norm_g, w_pool_mix, pool_scale,
           w_branch_a, w_branch_b, w_out, norm_x_g, norm_mem_g, w_xq, w_xk, w_xv, w_xo,
           norm_mlp_g, w_up, w_down, norm_final_g):
    depth = w_in.shape[0]
    assert depth == 1
    batch, seq, d = x_prompt.shape
    dec_batch, dec_seq, _ = x_sample.shape
    mp = batch * seq
    ms = dec_batch * dec_seq

    w_in_r = prep_w_in(w_in[0].T)
    wgk = jnp.concatenate(
        [w_gk_up[0], jnp.zeros((LANE - GLA_GATE_RANK, GLA_KEY_W), F32)], axis=0).astype(BF16)
    wmix = w_pool_mix[0].astype(BF16)

    xp = x_prompt.reshape(mp, d)
    xs = x_sample.reshape(ms, d)

    (qk_p, v_p, gate_p, u_p, ga_p, gb_p, lah_p, lal_p, d_p,
     wa, wb, wo, wxq, wxo, wu, wd, wxk, wxv) = in_proj(
        xp, norm_mix_g[0], w_in_r, wgk, b_gk[0], TM_PROJ,
        cast=(w_branch_a, w_branch_b, w_out, w_xq, w_xo, w_up, w_down), cast_kv=(w_xk, w_xv),
        pool_seq_len=seq)

    qk_s, v_s, gate_s, u_s, ga_s, gb_s, lah_s, lal_s = in_proj(xs, norm_mix_g[0], w_in_r, wgk, b_gk[0], ms)
    gla_rows = SAMPLE_SEQS_PER_GLA_STEP * dec_seq
    o_s, sg_s = gla(qk_s, v_s, lah_s, lal_s, gate_s, gla_norm_g[0], state_gla[0],
                    groups=ms // gla_rows, tt=gla_rows, rows=gla_rows, seg=dec_seq)
    d_tm, sp_tm = pool_sample(state_pool[0].transpose(1, 0, 2),
                              u_s.reshape(dec_batch, dec_seq, d).transpose(1, 0, 2),
                              PAST_LEN, SAMPLE_SEQS_PER_POOL_STEP)
    h_s, q_s = mix_out(o_s, d_tm.transpose(1, 0, 2).reshape(ms, d), ga_s, gb_s, xs, wmix,
                       pool_scale[0], wa, wb, wo, tm=ms, query=(norm_x_g[0], wxq))
    halves = X_HEAD_DIM // LANE
    qhat = q_s.reshape(dec_batch, dec_seq, X_HEADS, halves, LANE).transpose(0, 3, 2, 1, 4)
    qhat = qhat.reshape(dec_batch * XQ_ROWS, LANE)

    o_p, sg_p = gla(qk_p, v_p, lah_p, lal_p, gate_p, gla_norm_g[0], None,
                    groups=batch, tt=TT_GLA, rows=GLA_CHUNK, seg=GLA_CHUNK)
    h_p = mix_out(o_p, d_p, ga_p, gb_p, xp, wmix, pool_scale[0], wa, wb, wo, tm=TM_MIX)
    h_p, mk_p, mv_p = xattn_prompt(h_p, norm_x_g[0], wxq, mem_prompt, norm_mem_g[0], wxk, wxv, wxo,
                                   tm=TM_XATTN)
    y_p, a_s = mlp_final(h_p, norm_mlp_g[0], wu, wd, norm_final_g, tm=TM_MLP, tf=TF_MLP,
                         attn=(qhat, _cache_rows(cache_mem_k[0]), _cache_rows(cache_mem_v[0])))
    sp_p = u_p.reshape(batch, HALO, d)[:, HALO - POOL_BUF:]

    a_s = a_s.reshape(dec_batch, halves, X_HEADS, dec_seq, LANE).transpose(0, 3, 2, 1, 4).reshape(ms, d)
    y_s = mlp_final(h_s, norm_mlp_g[0], wu, wd, norm_final_g, tm=ms, tf=TF_MLP, pre=(a_s, wxo))
    sp_s = sp_tm.transpose(1, 0, 2)

    return (y_p.reshape(batch, seq, d),
            y_s.reshape(dec_batch, dec_seq, d),
            _mem_kv_output(mk_p),
            _mem_kv_output(mv_p),
            sg_p[None],
            sg_s[None],
            sp_p[None],
            sp_s[None])
```

```python
import functools

import jax
import jax.numpy as jnp
from jax import lax
from jax.experimental import pallas as pl
from jax.experimental.pallas import tpu as pltpu

F32 = jnp.float32
BF16 = jnp.bfloat16

D_MODEL = 1024
GLA_HEADS = 4
GLA_DK = 128
GLA_DV = 256
GLA_KEY_W = GLA_HEADS * GLA_DK
GLA_VAL_W = GLA_HEADS * GLA_DV
GLA_GATE_RANK = 16
GLA_GATE_NORM = 16.0
POOL_WINDOWS = (2, 4, 8, 16)
POOL_G = 256
POOL_BUF = 15
MEM_LEN = 256
X_HEADS = 4
X_HEAD_DIM = 256
EPS = 1e-6
PAST_LEN = 16384

LANE = 128
HALO = 16
VMEM_LIMIT = 52 * 1024 * 1024
XQ_ROWS = 32

IN_COLS = {
    "qk": (0, 2 * GLA_KEY_W),
    "v": (1024, GLA_VAL_W),
    "og": (2048, GLA_VAL_W),
    "u": (3072, D_MODEL),
    "ga": (4096, D_MODEL),
    "gb": (5120, D_MODEL),
    "gz": (6144, LANE),
}
PROJ_W = 6272

TM_PROJ = 512
TM_MIX = 1024
TM_XATTN = 1024
TM_MLP = 1024
TF_MLP = 1024
TT_GLA = 1024
GLA_CHUNK = 128
GLA_FAST_RANGE = 60.0
SAMPLE_SEQS_PER_GLA_STEP = 16
SAMPLE_SEQS_PER_POOL_STEP = 32


def _params(sem):
    return pltpu.CompilerParams(dimension_semantics=sem, vmem_limit_bytes=VMEM_LIMIT)


def _rmsnorm(x, g):
    return x * lax.rsqrt(jnp.mean(x * x, axis=-1, keepdims=True) + EPS) * g


def _dot(a, b):
    return jnp.dot(a, b, preferred_element_type=F32)


def _dot_nt(a, b):
    return lax.dot_general(a, b, (((1,), (1,)), ((), ())), preferred_element_type=F32)


def _dot_tn(a, b):
    return lax.dot_general(a, b, (((0,), (0,)), ((), ())), preferred_element_type=F32)


def _log_decay_split(z):
    la = (jnp.minimum(z, 0.0) - jnp.log(1.0 + jnp.exp(-jnp.abs(z)))) * (1.0 / GLA_GATE_NORM)
    la_hi = la.astype(BF16)
    return la_hi, (la - la_hi.astype(F32)).astype(BF16)


def _pool_diff(u, halo, t_in_seq):
    tm = u.shape[0]
    halo = jnp.where(t_in_seq == 0, 0.0, halo)
    pos1 = (t_in_seq * tm + 1 + lax.broadcasted_iota(jnp.int32, (tm, 1), 0)).astype(F32)
    out = []
    for g, w in enumerate(POOL_WINDOWS):
        cs = slice(g * POOL_G, (g + 1) * POOL_G)
        s = jnp.concatenate([halo[:, cs], u[:, cs]], axis=0)
        shift = 1
        while shift < w:
            s = s + pltpu.roll(s, shift, axis=0)
            shift *= 2
        out.append(s[HALO:, :] * (1.0 / jnp.minimum(pos1, float(w))) - u[:, cs])
    return jnp.concatenate(out, axis=1)


def _cast_head_major_to_half_major(i_ref, o_ref):
    halves = X_HEAD_DIM // LANE
    for hd in range(X_HEADS):
        for c in range(halves):
            src = (hd * halves + c) * LANE
            dst = (c * X_HEADS + hd) * LANE
            o_ref[:, dst:dst + LANE] = i_ref[:, src:src + LANE].astype(BF16)


def _in_proj_kernel(*refs, ncast, nregroup, tiles_per_seq):
    pool = tiles_per_seq is not None
    nout = 9 if pool else 8
    (x_ref, g_ref, w_ref, wgk_ref, bgk_ref) = refs[:5]
    cast_in = refs[5:5 + ncast]
    outs = refs[5 + ncast:5 + ncast + nout]
    (qk_ref, v_ref, gate_ref, u_ref, ga_ref, gb_ref, lahi_ref, lalo_ref) = outs[:8]
    cast_out = refs[5 + ncast + nout:5 + 2 * ncast + nout]
    for n, (i_ref, o_ref) in enumerate(zip(cast_in, cast_out)):
        if n >= ncast - nregroup:
            _cast_head_major_to_half_major(i_ref, o_ref)
        else:
            o_ref[...] = i_ref[...].astype(BF16)
    xn = _rmsnorm(x_ref[...], g_ref[...]).astype(BF16)

    def piece(name):
        col, width = IN_COLS[name]
        return _dot_nt(xn, w_ref[col:col + width, :])

    u = piece("u")
    if pool:
        d_ref, tail_ref = outs[8], refs[-1]
        tm = u.shape[0]
        d_ref[...] = _pool_diff(u, tail_ref[...], pl.program_id(0) % tiles_per_seq).astype(BF16)
        tail_ref[...] = u[tm - HALO:, :]
        u_ref[...] = u[tm - HALO:, :]
    else:
        u_ref[...] = u
    og = piece("og")
    gate_ref[...] = (og * jax.nn.sigmoid(og)).astype(BF16)
    ga_ref[...] = piece("ga").astype(BF16)
    gz = piece("gz").astype(BF16)
    gb_ref[...] = piece("gb").astype(BF16)
    z = _dot(gz, wgk_ref[...]) + bgk_ref[...]
    lahi_ref[...], lalo_ref[...] = _log_decay_split(z)
    qk_ref[...] = piece("qk")
    v_ref[...] = piece("v").astype(BF16)


def in_proj(x, g, w, wgk, bgk, tm, cast=(), cast_kv=(), pool_seq_len=None):
    m, k = x.shape
    steps = m // tm
    outs = [(2 * GLA_KEY_W, F32), (GLA_VAL_W, BF16), (GLA_VAL_W, BF16), (D_MODEL, F32),
            (D_MODEL, BF16), (D_MODEL, BF16), (GLA_KEY_W, BF16), (GLA_KEY_W, BF16)]
    scratch = []
    tiles_per_seq = None
    if pool_seq_len is not None:
        outs.append((D_MODEL, BF16))
        scratch.append(pltpu.VMEM((HALO, D_MODEL), F32))
        tiles_per_seq = pool_seq_len // tm
    const = lambda shape: pl.BlockSpec(shape, lambda i: (0, 0))
    out_shape = [jax.ShapeDtypeStruct((m, width), dtype) for width, dtype in outs]
    out_specs = [pl.BlockSpec((tm, width), lambda i: (i, 0)) for width, _ in outs]
    if pool_seq_len is not None:
        u_index = 3
        out_shape[u_index] = jax.ShapeDtypeStruct((m // pool_seq_len * HALO, D_MODEL), F32)
        out_specs[u_index] = pl.BlockSpec((HALO, D_MODEL), lambda i: (i // tiles_per_seq, 0))
    in_specs = [pl.BlockSpec((tm, k), lambda i: (i, 0)), const((1, k)), const((PROJ_W, k)),
                const((LANE, GLA_KEY_W)), const((1, GLA_KEY_W))]
    cast = tuple(cast) + tuple(cast_kv)
    for wc in cast:
        _, rows, cols = wc.shape
        in_specs.append(pl.BlockSpec((None, rows // steps, cols), lambda i: (0, i, 0)))
        out_specs.append(pl.BlockSpec((rows // steps, cols), lambda i: (i, 0)))
        out_shape.append(jax.ShapeDtypeStruct((rows, cols), BF16))
    return pl.pallas_call(
        functools.partial(_in_proj_kernel, ncast=len(cast), nregroup=len(cast_kv),
                          tiles_per_seq=tiles_per_seq),
        out_shape=tuple(out_shape),
        grid=(steps,),
        in_specs=in_specs,
        out_specs=tuple(out_specs),
        scratch_shapes=scratch,
        compiler_params=_params(("arbitrary",)),
        name="in_proj",
    )(x, g.reshape(1, k), w, wgk, bgk.reshape(1, GLA_KEY_W), *cast)


def _gla_kernel(q_ref, k_ref, v_ref, lahi_ref, lalo_ref, gate_ref, gn_ref, s0_ref,
                o_ref, s_ref, b_ref, *maybe_s_in_ref, rows, seg, zero_init):
    tt = q_ref.shape[0]
    nchunks = tt // rows
    nseg = rows // seg
    seg_shift = seg.bit_length() - 1

    @pl.when(pl.program_id(1) == 0)
    def _():
        if zero_init:
            s_ref[...] = jnp.zeros(s_ref.shape, F32)
        else:
            s_ref[...] = s0_ref[...]

    if maybe_s_in_ref:
        s_in_ref, = maybe_s_in_ref
        s_in_ref[...] = s_ref[...]
    else:
        s_in_ref = s0_ref

    ri = lax.broadcasted_iota(jnp.int32, (rows, rows), 0)
    ci = lax.broadcasted_iota(jnp.int32, (rows, rows), 1)
    same_seg = (ri >> seg_shift) == (ci >> seg_shift)
    causal = jnp.logical_and(same_seg, ci <= ri)
    l_cum = jnp.where(causal, 1.0, 0.0).astype(BF16)
    l_seg = jnp.where(same_seg, 1.0, 0.0).astype(BF16)
    row_seg = lax.broadcasted_iota(jnp.int32, (rows, GLA_DK), 0) >> seg_shift
    row_seg_v = lax.broadcasted_iota(jnp.int32, (rows, GLA_DV), 0) >> seg_shift
    gn = gn_ref[...]
    qscale = GLA_DK ** -0.5

    def cumulative(c):
        rsl = slice(c * rows, (c + 1) * rows)
        lah = lahi_ref[rsl, :]
        lal = lalo_ref[rsl, :]
        b = _dot(l_cum, lah) + _dot(l_cum, lal)
        if nseg == 1:
            b_end = b[rows - 1:rows, :]
        else:
            b_end = _dot(l_seg, lah) + _dot(l_seg, lal)
        return lah, lal, b, b_end

    def pairwise_scores(c, ks, qs_h, b_h):
        sub = 8

        def columns(blk, acc):
            r0 = pl.multiple_of(blk * sub, sub)
            b_blk = b_ref[pl.ds(r0, sub), ks]
            k_blk = k_ref[pl.ds(c * rows + r0, sub), ks]
            for j in range(sub):
                e = jnp.exp(jnp.minimum(b_h - b_blk[j:j + 1, :], 0.0))
                col = jnp.sum(qs_h * e * k_blk[j:j + 1, :], axis=1, keepdims=True)
                acc = acc + jnp.where(ci == r0 + j, col, 0.0)
            return acc

        return lax.fori_loop(0, rows // sub, columns, jnp.zeros((rows, rows), F32))

    def run(exact):
        worst = jnp.float32(0.0)
        ahead = cumulative(0)
        for c in range(nchunks):
            rsl = slice(c * rows, (c + 1) * rows)
            lah, lal, b, b_end = ahead
            if c + 1 < nchunks:
                ahead = cumulative(c + 1)
            qs = q_ref[rsl, :] * qscale
            k = k_ref[rsl, :]
            qt_b = (qs * jnp.exp(b)).astype(BF16)
            kd = k * jnp.exp(b_end - b)
            if exact:
                b_ref[...] = b
            else:
                kt_b = (k * jnp.exp(-b)).astype(BF16)
                worst = jnp.maximum(worst, jnp.max(-b_end))
            heads = []
            for h in range(GLA_HEADS):
                ks = slice(h * GLA_DK, (h + 1) * GLA_DK)
                v_h = v_ref[rsl, h * GLA_DV:(h + 1) * GLA_DV]
                if exact:
                    scores = pairwise_scores(c, ks, qs[:, ks], b[:, ks])
                else:
                    scores = _dot_nt(qt_b[:, ks], kt_b[:, ks])
                inter = None
                new_states = []
                for j in range(nseg):
                    s_old = s_ref[j, h]
                    inter_j = _dot(qt_b[:, ks], s_old.astype(BF16))
                    if nseg == 1:
                        inter = inter_j
                        kd_j = kd[:, ks].astype(BF16)
                        dcol = jnp.broadcast_to(b_end[:, ks], (GLA_DK, GLA_DK)).T
                    else:
                        inter_j = jnp.where(row_seg_v == j, inter_j, 0.0)
                        inter = inter_j if inter is None else inter + inter_j
                        kd_j = jnp.where(row_seg == j, kd[:, ks], 0.0).astype(BF16)
                        ones_j = jnp.where(row_seg == j, 1.0, 0.0).astype(BF16)
                        dcol = _dot_tn(lah[:, ks], ones_j) + _dot_tn(lal[:, ks], ones_j)
                    new_states.append((s_old, dcol, _dot_tn(kd_j, v_h)))
                heads.append((v_h, scores, inter, new_states))
            outs = []
            for v_h, scores, inter, _ in heads:
                a = jnp.where(causal, scores, 0.0).astype(BF16)
                outs.append(_dot(a, v_h) + inter)
            for h, (_, _, _, new_states) in enumerate(heads):
                vs = slice(h * GLA_DV, (h + 1) * GLA_DV)
                for j, (s_old, dcol, upd) in enumerate(new_states):
                    e = jnp.exp(dcol)
                    s_ref[j, h] = s_old * jnp.concatenate([e, e], axis=1) + upd
                on = _rmsnorm(outs[h], gn)
                o_ref[rsl, vs] = (on * gate_ref[rsl, vs].astype(F32)).astype(o_ref.dtype)
        return worst

    worst = run(exact=False)

    @pl.when(worst > GLA_FAST_RANGE)
    def _():
        s_ref[...] = s_in_ref[...]
        run(exact=True)


def gla(qk, v, la_hi, la_lo, gate, gn, s0, *, groups, tt, rows, seg):
    m = qk.shape[0]
    steps = m // (groups * tt)
    nseg = rows // seg
    zero_init = s0 is None
    if zero_init:
        s0 = jnp.zeros((nseg, GLA_HEADS, 8, LANE), F32)
        s0_spec = pl.BlockSpec((nseg, GLA_HEADS, 8, LANE), lambda g, t: (0, 0, 0, 0))
    else:
        s0_spec = pl.BlockSpec((nseg, GLA_HEADS, GLA_DK, GLA_DV), lambda g, t: (g, 0, 0, 0))

    def row_spec(width, blk=0):
        return pl.BlockSpec((tt, width), lambda g, t: (g * steps + t, blk))

    scratch = [pltpu.VMEM((rows, GLA_KEY_W), F32)]
    if zero_init or steps > 1:
        scratch.append(pltpu.VMEM((nseg, GLA_HEADS, GLA_DK, GLA_DV), F32))
    kern = functools.partial(_gla_kernel, rows=rows, seg=seg, zero_init=zero_init)
    return pl.pallas_call(
        kern,
        out_shape=(jax.ShapeDtypeStruct((m, GLA_VAL_W), BF16),
                   jax.ShapeDtypeStruct((groups * nseg, GLA_HEADS, GLA_DK, GLA_DV), F32)),
        grid=(groups, steps),
        in_specs=[
            row_spec(GLA_KEY_W, 0),
            row_spec(GLA_KEY_W, 1),
            row_spec(GLA_VAL_W),
            row_spec(GLA_KEY_W),
            row_spec(GLA_KEY_W),
            row_spec(GLA_VAL_W),
            pl.BlockSpec((1, GLA_DV), lambda g, t: (0, 0)),
            s0_spec,
        ],
        out_specs=(row_spec(GLA_VAL_W),
                   pl.BlockSpec((nseg, GLA_HEADS, GLA_DK, GLA_DV), lambda g, t: (g, 0, 0, 0))),
        scratch_shapes=scratch,
        compiler_params=_params(("parallel", "arbitrary")),
        name="gla",
    )(qk, qk, v, la_hi, la_lo, gate, gn.reshape(1, GLA_DV), s0)


def _pool_sample_kernel(buf_ref, u_ref, d_ref, new_ref, *, pos0):
    t_new = u_ref.shape[0]

    def ext(i, cs=slice(None)):
        return buf_ref[i, :, cs] if i < POOL_BUF else u_ref[i - POOL_BUF, :, cs]

    for r in range(POOL_BUF):
        new_ref[r] = ext(r + t_new)
    for t in range(t_new):
        cur = POOL_BUF + t
        for g, w in enumerate(POOL_WINDOWS):
            cs = slice(g * POOL_G, (g + 1) * POOL_G)
            win = ext(cur, cs)
            for j in range(1, w):
                win = win + ext(cur - j, cs)
            cnt = float(min(pos0 + t + 1, w))
            d_ref[t, :, cs] = win * (1.0 / cnt) - ext(cur, cs)


def pool_sample(buf_tm, u_tm, pos0, nb):
    nbuf, nseq, width = buf_tm.shape
    t_new = u_tm.shape[0]
    buf_spec = pl.BlockSpec((nbuf, nb, width), lambda i: (0, i, 0))
    new_spec = pl.BlockSpec((t_new, nb, width), lambda i: (0, i, 0))
    return pl.pallas_call(
        functools.partial(_pool_sample_kernel, pos0=pos0),
        out_shape=(jax.ShapeDtypeStruct(u_tm.shape, F32), jax.ShapeDtypeStruct(buf_tm.shape, F32)),
        grid=(nseq // nb,),
        in_specs=[buf_spec, new_spec],
        out_specs=(new_spec, buf_spec),
        compiler_params=_params(("parallel",)),
        name="pool_sample",
    )(buf_tm, u_tm)


def _mix_out_kernel(*refs, with_query):
    (o_ref, d_ref, ga_ref, gb_ref, x_ref, wmix_ref, ps_ref, wa_ref, wb_ref, wo_ref) = refs[:10]
    h_ref = refs[12] if with_query else refs[10]
    branch_a = _dot(o_ref[...], wa_ref[...])
    pooled = []
    for g in range(len(POOL_WINDOWS)):
        cs = slice(g * POOL_G, (g + 1) * POOL_G)
        y = _dot(d_ref[:, cs].astype(BF16), wmix_ref[g]) * ps_ref[:, cs]
        pooled.append(y.astype(BF16))
    pooled = jnp.concatenate(pooled, axis=1)
    merged = (jax.nn.sigmoid(ga_ref[...].astype(F32)) * branch_a
              + jax.nn.sigmoid(gb_ref[...].astype(F32)) * _dot(pooled, wb_ref[...]))
    h = x_ref[...] + _dot(merged.astype(BF16), wo_ref[...])
    h_ref[...] = h
    if with_query:
        gq_ref, wq_ref, q_ref = refs[10], refs[11], refs[13]
        q_ref[...] = _dot(_rmsnorm(h, gq_ref[...]).astype(BF16), wq_ref[...])


def mix_out(o, d, ga, gb, x, wmix, pscale, wa, wb, wo, *, tm, query=None):
    m, wide = x.shape
    row_spec = pl.BlockSpec((tm, wide), lambda i: (i, 0))
    const2 = lambda shape: pl.BlockSpec(shape, lambda i: (0, 0))
    in_specs = [row_spec] * 5 + [
        pl.BlockSpec((len(POOL_WINDOWS), POOL_G, POOL_G), lambda i: (0, 0, 0)),
        const2((1, wide)),
        const2((GLA_VAL_W, wide)),
        const2((wide, wide)),
        const2((wide, wide)),
    ]
    args = [o, d, ga, gb, x, wmix, pscale.reshape(1, wide), wa, wb, wo]
    out_shape = jax.ShapeDtypeStruct((m, wide), F32)
    out_specs = row_spec
    if query is not None:
        gq, wq = query
        in_specs += [const2((1, wide)), const2((wide, wide))]
        args += [gq.reshape(1, wide), wq]
        out_shape = (out_shape, out_shape)
        out_specs = (row_spec, row_spec)
    return pl.pallas_call(
        functools.partial(_mix_out_kernel, with_query=query is not None),
        out_shape=out_shape,
        grid=(m // tm,),
        in_specs=in_specs,
        out_specs=out_specs,
        compiler_params=_params(("parallel",)),
        name="mix_out",
    )(*args)


def _softmax_rows(s):
    p = jnp.exp(s - jnp.max(s, axis=-1, keepdims=True))
    return p, 1.0 / jnp.sum(p, axis=-1, keepdims=True)


def _xattn_prompt_kernel(h_ref, g_ref, wq_ref, mem_ref, gm_ref, wk_ref, wv_ref, wo_ref,
                         o_ref, mk_ref, mv_ref, kb_ref, vb_ref, *, tiles):
    groups = 2 * X_HEADS

    @pl.when(pl.program_id(0) % tiles == 0)
    def _():
        mn = _rmsnorm(mem_ref[0], gm_ref[...]).astype(BF16)
        k = _dot(mn, wk_ref[...])
        v = _dot(mn, wv_ref[...])
        kb_ref[...] = k.astype(BF16)
        vb_ref[...] = v.astype(BF16)
        mk_ref[0] = k.reshape(MEM_LEN, groups, LANE)
        mv_ref[0] = v.reshape(MEM_LEN, groups, LANE)

    def head(ref, hd):
        lo = ref[:, hd * LANE:(hd + 1) * LANE]
        hi = ref[:, (X_HEADS + hd) * LANE:(X_HEADS + hd + 1) * LANE]
        return jnp.concatenate([lo, hi], axis=1)

    h = h_ref[...]
    hn = _rmsnorm(h, g_ref[...]).astype(BF16)
    q = (_dot(hn, wq_ref[...]) * (X_HEAD_DIM ** -0.5)).astype(BF16)

    def scores(hd):
        return _dot_nt(q[:, hd * X_HEAD_DIM:(hd + 1) * X_HEAD_DIM], head(kb_ref, hd))

    outs = []
    s_next = scores(0)
    for hd in range(X_HEADS):
        s_cur = s_next
        if hd + 1 < X_HEADS:
            s_next = scores(hd + 1)
        p, inv = _softmax_rows(s_cur)
        outs.append((_dot(p.astype(BF16), head(vb_ref, hd)) * inv).astype(BF16))
    o = jnp.concatenate(outs, axis=1)
    o_ref[...] = h + _dot(o, wo_ref[...])


def xattn_prompt(h, g, wq, mem, gm, wk, wv, wo, *, tm):
    m, d = h.shape
    batch = mem.shape[0]
    tiles = m // batch // tm
    groups = 2 * X_HEADS
    row_spec = pl.BlockSpec((tm, d), lambda i: (i, 0))
    vec_spec = pl.BlockSpec((1, d), lambda i: (0, 0))
    w_spec = pl.BlockSpec((d, d), lambda i: (0, 0))
    kv_spec = pl.BlockSpec((1, MEM_LEN, groups, LANE), lambda i: (i // tiles, 0, 0, 0))
    kv_shape = jax.ShapeDtypeStruct((batch, MEM_LEN, groups, LANE), F32)
    return pl.pallas_call(
        functools.partial(_xattn_prompt_kernel, tiles=tiles),
        out_shape=(jax.ShapeDtypeStruct((m, d), F32), kv_shape, kv_shape),
        grid=(m // tm,),
        in_specs=[row_spec, vec_spec, w_spec,
                  pl.BlockSpec((1, MEM_LEN, d), lambda i: (i // tiles, 0, 0)), vec_spec,
                  w_spec, w_spec, w_spec],
        out_specs=(row_spec, kv_spec, kv_spec),
        scratch_shapes=[pltpu.VMEM((MEM_LEN, d), BF16), pltpu.VMEM((MEM_LEN, d), BF16)],
        compiler_params=_params(("arbitrary",)),
        name="xattn_prompt",
    )(h, g.reshape(1, d), wq, mem, gm.reshape(1, d), wk, wv, wo)


def _sample_attention_probs(q_ref, k_ref):
    nseq, krows, _ = k_ref.shape
    half = XQ_ROWS // 2
    lane = lax.broadcasted_iota(jnp.int32, (half, krows), 1)
    row = lax.broadcasted_iota(jnp.int32, (half, krows), 0)
    valid = (lane & 7) == (row >> 2)
    out = []
    for j in range(nseq):
        q = (q_ref[j * XQ_ROWS:(j + 1) * XQ_ROWS, :] * (X_HEAD_DIM ** -0.5)).astype(BF16)
        g = _dot_nt(q, k_ref[j].astype(BF16))
        s = g[:half] + pltpu.roll(g[half:], krows - 4, axis=1)
        s = jnp.where(valid, s, -1e30)
        p = jnp.exp(s - jnp.max(s, axis=-1, keepdims=True))
        inv = 1.0 / jnp.sum(p, axis=-1, keepdims=True)
        pe = jnp.concatenate([p, pltpu.roll(p, 4, axis=1)], axis=0).astype(BF16)
        out.append((pe, jnp.concatenate([inv, inv], axis=0)))
    return out


def _sample_attention_values(probs, v_ref, o_ref):
    for j, (pe, inv) in enumerate(probs):
        o = _dot(pe, v_ref[j].astype(BF16)) * inv
        o_ref[j * XQ_ROWS:(j + 1) * XQ_ROWS, :] = o.astype(o_ref.dtype)


def _mlp_kernel(*refs, with_attn, with_pre):
    (h_ref, g_ref, wu_ref, wd_ref, gf_ref) = refs[:5]
    rest = refs[5:]
    if with_attn:
        (q_ref, k_ref, v_ref), rest = rest[:3], rest[3:]
    if with_pre:
        (pre_ref, wpre_ref), rest = rest[:2], rest[2:]
    if with_attn:
        y_ref, a_ref, hn_ref, acc_ref = rest
    else:
        y_ref, hn_ref, acc_ref = rest
    f = pl.program_id(1)

    @pl.when(f == 0)
    def _():
        h = h_ref[...]
        if with_pre:
            h = h + _dot(pre_ref[...], wpre_ref[...])
        hn_ref[...] = _rmsnorm(h, g_ref[...]).astype(BF16)
        acc_ref[...] = h

    if with_attn:
        probs = _sample_attention_probs(q_ref, k_ref)
    a = jnp.maximum(_dot(hn_ref[...], wu_ref[...]), 0.0)
    if with_attn:
        _sample_attention_values(probs, v_ref, a_ref)
    acc_ref[...] += _dot((a * a).astype(BF16), wd_ref[...])

    @pl.when(f == pl.num_programs(1) - 1)
    def _():
        y_ref[...] = _rmsnorm(acc_ref[...], gf_ref[...])


def mlp_final(h, g, wu, wd, gf, *, tm, tf, attn=None, pre=None):
    m, d = h.shape
    ff = wu.shape[1]
    nf = ff // tf
    row_spec = pl.BlockSpec((tm, d), lambda i, f: (i, 0))
    vec_spec = pl.BlockSpec((1, d), lambda i, f: (0, 0))
    in_specs = [row_spec, vec_spec,
                pl.BlockSpec((d, tf), lambda i, f: (0, f)),
                pl.BlockSpec((tf, d), lambda i, f: (f, 0)),
                vec_spec]
    args = [h, g.reshape(1, d), wu, wd, gf.reshape(1, d)]
    out_shape = jax.ShapeDtypeStruct((m, d), F32)
    out_specs = row_spec
    if attn is not None:
        qhat, ck, cv = attn
        steps = (m // tm) * nf
        nseq = ck.shape[0] // steps
        assert nseq * steps == ck.shape[0]
        q_spec = pl.BlockSpec((nseq * XQ_ROWS, LANE), lambda i, f: (i * nf + f, 0))
        kv_spec = pl.BlockSpec((nseq,) + ck.shape[1:], lambda i, f: (i * nf + f, 0, 0))
        in_specs += [q_spec, kv_spec, kv_spec]
        args += [qhat, ck, cv]
        out_shape = (out_shape, jax.ShapeDtypeStruct(qhat.shape, BF16))
        out_specs = (row_spec, q_spec)
    if pre is not None:
        a, w_pre = pre
        in_specs += [pl.BlockSpec((tm, a.shape[1]), lambda i, f: (i, 0)),
                     pl.BlockSpec(w_pre.shape, lambda i, f: (0, 0))]
        args += [a, w_pre]
    return pl.pallas_call(
        functools.partial(_mlp_kernel, with_attn=attn is not None, with_pre=pre is not None),
        out_shape=out_shape,
        grid=(m // tm, nf),
        in_specs=in_specs,
        out_specs=out_specs,
        scratch_shapes=[pltpu.VMEM((tm, d), BF16), pltpu.VMEM((tm, d), F32)],
        compiler_params=_params(("parallel", "arbitrary")),
        name="mlp_final",
    )(*args)


def _prep_w_in_kernel(win_ref, win_out):
    gz_lo = 2 * GLA_KEY_W + GLA_VAL_W
    gz_hi = gz_lo + GLA_GATE_RANK
    gz_out = IN_COLS["gz"][0]
    win_out[:gz_lo, :] = win_ref[:gz_lo, :].astype(BF16)
    win_out[gz_lo:gz_out, :] = win_ref[gz_hi:, :].astype(BF16)
    win_out[gz_out:gz_out + GLA_GATE_RANK, :] = win_ref[gz_lo:gz_hi, :].astype(BF16)
    win_out[gz_out + GLA_GATE_RANK:, :] = jnp.zeros(
        (PROJ_W - gz_out - GLA_GATE_RANK, win_out.shape[1]), BF16)


def prep_w_in(w_in_t, nblk=4):
    width, d = w_in_t.shape
    return pl.pallas_call(
        _prep_w_in_kernel,
        out_shape=jax.ShapeDtypeStruct((PROJ_W, d), BF16),
        grid=(nblk,),
        in_specs=[pl.BlockSpec((width, d // nblk), lambda i: (0, i))],
        out_specs=pl.BlockSpec((PROJ_W, d // nblk), lambda i: (0, i)),
        compiler_params=_params(("parallel",)),
        name="prep_w_in",
    )(w_in_t)


def _cache_rows(c):
    b, m, h, dh = c.shape
    return c.reshape(b, m, h, dh // LANE, LANE).transpose(0, 1, 3, 2, 4).reshape(b, m * h * (dh // LANE), LANE)


def _mem_kv_output(kv):
    b, m, _, _ = kv.shape
    kv = kv.reshape(b, m, X_HEAD_DIM // LANE, X_HEADS, LANE).transpose(0, 1, 3, 2, 4)
    return kv.reshape(1, b, m, X_HEADS, X_HEAD_DIM)


def kernel(x_prompt, x_sample, mem_prompt, state_gla, state_pool, cache_mem_k, cache_mem_v,
           norm_mix_g, w_in, w_gk_up, b_gk, gla_norm_g, w_pool_mix, pool_scale,
           w_branch_a, w_branch_b, w_out, norm_x_g, norm_mem_g, w_xq, w_xk, w_xv, w_xo,
           norm_mlp_g, w_up, w_down, norm_final_g):
    depth = w_in.shape[0]
    assert depth == 1
    batch, seq, d = x_prompt.shape
    dec_batch, dec_seq, _ = x_sample.shape
    mp = batch * seq
    ms = dec_batch * dec_seq

    w_in_r = prep_w_in(w_in[0].T)
    wgk = jnp.concatenate(
        [w_gk_up[0], jnp.zeros((LANE - GLA_GATE_RANK, GLA_KEY_W), F32)], axis=0).astype(BF16)
    wmix = w_pool_mix[0].astype(BF16)

    xp = x_prompt.reshape(mp, d)
    xs = x_sample.reshape(ms, d)

    (qk_p, v_p, gate_p, u_p, ga_p, gb_p, lah_p, lal_p, d_p,
     wa, wb, wo, wxq, wxo, wu, wd, wxk, wxv) = in_proj(
        xp, norm_mix_g[0], w_in_r, wgk, b_gk[0], TM_PROJ,
        cast=(w_branch_a, w_branch_b, w_out, w_xq, w_xo, w_up, w_down), cast_kv=(w_xk, w_xv),
        pool_seq_len=seq)

    qk_s, v_s, gate_s, u_s, ga_s, gb_s, lah_s, lal_s = in_proj(xs, norm_mix_g[0], w_in_r, wgk, b_gk[0], ms)
    gla_rows = SAMPLE_SEQS_PER_GLA_STEP * dec_seq
    o_s, sg_s = gla(qk_s, v_s, lah_s, lal_s, gate_s, gla_norm_g[0], state_gla[0],
                    groups=ms // gla_rows, tt=gla_rows, rows=gla_rows, seg=dec_seq)
    d_tm, sp_tm = pool_sample(state_pool[0].transpose(1, 0, 2),
                              u_s.reshape(dec_batch, dec_seq, d).transpose(1, 0, 2),
                              PAST_LEN, SAMPLE_SEQS_PER_POOL_STEP)
    h_s, q_s = mix_out(o_s, d_tm.transpose(1, 0, 2).reshape(ms, d), ga_s, gb_s, xs, wmix,
                       pool_scale[0], wa, wb, wo, tm=ms, query=(norm_x_g[0], wxq))
    halves = X_HEAD_DIM // LANE
    qhat = q_s.reshape(dec_batch, dec_seq, X_HEADS, halves, LANE).transpose(0, 3, 2, 1, 4)
    qhat = qhat.reshape(dec_batch * XQ_ROWS, LANE)

    o_p, sg_p = gla(qk_p, v_p, lah_p, lal_p, gate_p, gla_norm_g[0], None,
                    groups=batch, tt=TT_GLA, rows=GLA_CHUNK, seg=GLA_CHUNK)
    h_p = mix_out(o_p, d_p, ga_p, gb_p, xp, wmix, pool_scale[0], wa, wb, wo, tm=TM_MIX)
    h_p, mk_p, mv_p = xattn_prompt(h_p, norm_x_g[0], wxq, mem_prompt, norm_mem_g[0], wxk, wxv, wxo,
                                   tm=TM_XATTN)
    y_p, a_s = mlp_final(h_p, norm_mlp_g[0], wu, wd, norm_final_g, tm=TM_MLP, tf=TF_MLP,
                         attn=(qhat, _cache_rows(cache_mem_k[0]), _cache_rows(cache_mem_v[0])))
    sp_p = u_p.reshape(batch, HALO, d)[:, HALO - POOL_BUF:]

    a_s = a_s.reshape(dec_batch, halves, X_HEADS, dec_seq, LANE).transpose(0, 3, 2, 1, 4).reshape(ms, d)
    y_s = mlp_final(h_s, norm_mlp_g[0], wu, wd, norm_final_g, tm=ms, tf=TF_MLP, pre=(a_s, wxo))
    sp_s = sp_tm.transpose(1, 0, 2)

    return (y_p.reshape(batch, seq, d),
            y_s.reshape(dec_batch, dec_seq, d),
            _mem_kv_output(mk_p),
            _mem_kv_output(mv_p),
            sg_p[None],
            sg_s[None],
            sp_p[None],
            sp_s[None])
```

```python
import functools

import jax
import jax.numpy as jnp
from jax import lax
from jax.experimental import pallas as pl
from jax.experimental.pallas import tpu as pltpu

F32 = jnp.float32
BF16 = jnp.bfloat16

D_MODEL = 1024
GLA_HEADS = 4
GLA_DK = 128
GLA_DV = 256
GLA_KEY_W = GLA_HEADS * GLA_DK
GLA_VAL_W = GLA_HEADS * GLA_DV
GLA_GATE_RANK = 16
GLA_GATE_NORM = 16.0
POOL_WINDOWS = (2, 4, 8, 16)
POOL_G = 256
POOL_BUF = 15
MEM_LEN = 256
X_HEADS = 4
X_HEAD_DIM = 256
EPS = 1e-6
PAST_LEN = 16384

LANE = 128
HALO = 16
VMEM_LIMIT = 52 * 1024 * 1024
XQ_ROWS = 32

IN_COLS = {
    "qk": (0, 2 * GLA_KEY_W),
    "v": (1024, GLA_VAL_W),
    "og": (2048, GLA_VAL_W),
    "u": (3072, D_MODEL),
    "ga": (4096, D_MODEL),
    "gb": (5120, D_MODEL),
    "gz": (6144, LANE),
}
PROJ_W = 6272

TM_PROJ = 512
TM_MIX = 512
TM_XATTN = 1024
TM_MLP = 1024
TF_MLP = 1024
TT_GLA = 1024
GLA_CHUNK = 128
GLA_FAST_RANGE = 60.0
SAMPLE_SEQS_PER_GLA_STEP = 16
SAMPLE_SEQS_PER_POOL_STEP = 32


def _params(sem):
    return pltpu.CompilerParams(dimension_semantics=sem, vmem_limit_bytes=VMEM_LIMIT)


def _rmsnorm(x, g):
    return x * lax.rsqrt(jnp.mean(x * x, axis=-1, keepdims=True) + EPS) * g


def _dot(a, b):
    return jnp.dot(a, b, preferred_element_type=F32)


def _dot_nt(a, b):
    return lax.dot_general(a, b, (((1,), (1,)), ((), ())), preferred_element_type=F32)


def _dot_tn(a, b):
    return lax.dot_general(a, b, (((0,), (0,)), ((), ())), preferred_element_type=F32)


def _log_decay_split(z):
    la = (jnp.minimum(z, 0.0) - jnp.log(1.0 + jnp.exp(-jnp.abs(z)))) * (1.0 / GLA_GATE_NORM)
    la_hi = la.astype(BF16)
    return la_hi, (la - la_hi.astype(F32)).astype(BF16)


def _pool_diff(u, halo, t_in_seq):
    tm = u.shape[0]
    halo = jnp.where(t_in_seq == 0, 0.0, halo)
    pos1 = (t_in_seq * tm + 1 + lax.broadcasted_iota(jnp.int32, (tm, 1), 0)).astype(F32)
    out = []
    for g, w in enumerate(POOL_WINDOWS):
        cs = slice(g * POOL_G, (g + 1) * POOL_G)
        s = jnp.concatenate([halo[:, cs], u[:, cs]], axis=0)
        shift = 1
        while shift < w:
            s = s + pltpu.roll(s, shift, axis=0)
            shift *= 2
        out.append(s[HALO:, :] * (1.0 / jnp.minimum(pos1, float(w))) - u[:, cs])
    return jnp.concatenate(out, axis=1)


def _cast_head_major_to_half_major(i_ref, o_ref):
    halves = X_HEAD_DIM // LANE
    for hd in range(X_HEADS):
        for c in range(halves):
            src = (hd * halves + c) * LANE
            dst = (c * X_HEADS + hd) * LANE
            o_ref[:, dst:dst + LANE] = i_ref[:, src:src + LANE].astype(BF16)


def _in_proj_kernel(*refs, ncast, nregroup, tiles_per_seq):
    pool = tiles_per_seq is not None
    nout = 9 if pool else 8
    (x_ref, g_ref, w_ref, wgk_ref, bgk_ref) = refs[:5]
    cast_in = refs[5:5 + ncast]
    outs = refs[5 + ncast:5 + ncast + nout]
    (qk_ref, v_ref, gate_ref, u_ref, ga_ref, gb_ref, lahi_ref, lalo_ref) = outs[:8]
    cast_out = refs[5 + ncast + nout:5 + 2 * ncast + nout]
    for n, (i_ref, o_ref) in enumerate(zip(cast_in, cast_out)):
        if n >= ncast - nregroup:
            _cast_head_major_to_half_major(i_ref, o_ref)
        else:
            o_ref[...] = i_ref[...].astype(BF16)
    xn = _rmsnorm(x_ref[...], g_ref[...]).astype(BF16)

    def piece(name):
        col, width = IN_COLS[name]
        return _dot_nt(xn, w_ref[col:col + width, :])

    u = piece("u")
    if pool:
        d_ref, tail_ref = outs[8], refs[-1]
        tm = u.shape[0]
        d_ref[...] = _pool_diff(u, tail_ref[...], pl.program_id(0) % tiles_per_seq).astype(BF16)
        tail_ref[...] = u[tm - HALO:, :]
        u_ref[...] = u[tm - HALO:, :]
    else:
        u_ref[...] = u
    og = piece("og")
    gate_ref[...] = (og * jax.nn.sigmoid(og)).astype(BF16)
    ga_ref[...] = piece("ga").astype(BF16)
    gz = piece("gz").astype(BF16)
    gb_ref[...] = piece("gb").astype(BF16)
    z = _dot(gz, wgk_ref[...]) + bgk_ref[...]
    lahi_ref[...], lalo_ref[...] = _log_decay_split(z)
    qk_ref[...] = piece("qk")
    v_ref[...] = piece("v").astype(BF16)


def in_proj(x, g, w, wgk, bgk, tm, cast=(), cast_kv=(), pool_seq_len=None):
    m, k = x.shape
    steps = m // tm
    outs = [(2 * GLA_KEY_W, F32), (GLA_VAL_W, BF16), (GLA_VAL_W, BF16), (D_MODEL, F32),
            (D_MODEL, BF16), (D_MODEL, BF16), (GLA_KEY_W, BF16), (GLA_KEY_W, BF16)]
    scratch = []
    tiles_per_seq = None
    if pool_seq_len is not None:
        outs.append((D_MODEL, BF16))
        scratch.append(pltpu.VMEM((HALO, D_MODEL), F32))
        tiles_per_seq = pool_seq_len // tm
    const = lambda shape: pl.BlockSpec(shape, lambda i: (0, 0))
    out_shape = [jax.ShapeDtypeStruct((m, width), dtype) for width, dtype in outs]
    out_specs = [pl.BlockSpec((tm, width), lambda i: (i, 0)) for width, _ in outs]
    if pool_seq_len is not None:
        u_index = 3
        out_shape[u_index] = jax.ShapeDtypeStruct((m // pool_seq_len * HALO, D_MODEL), F32)
        out_specs[u_index] = pl.BlockSpec((HALO, D_MODEL), lambda i: (i // tiles_per_seq, 0))
    in_specs = [pl.BlockSpec((tm, k), lambda i: (i, 0)), const((1, k)), const((PROJ_W, k)),
                const((LANE, GLA_KEY_W)), const((1, GLA_KEY_W))]
    cast = tuple(cast) + tuple(cast_kv)
    for wc in cast:
        _, rows, cols = wc.shape
        in_specs.append(pl.BlockSpec((None, rows // steps, cols), lambda i: (0, i, 0)))
        out_specs.append(pl.BlockSpec((rows // steps, cols), lambda i: (i, 0)))
        out_shape.append(jax.ShapeDtypeStruct((rows, cols), BF16))
    return pl.pallas_call(
        functools.partial(_in_proj_kernel, ncast=len(cast), nregroup=len(cast_kv),
                          tiles_per_seq=tiles_per_seq),
        out_shape=tuple(out_shape),
        grid=(steps,),
        in_specs=in_specs,
        out_specs=tuple(out_specs),
        scratch_shapes=scratch,
        compiler_params=_params(("arbitrary",)),
        name="in_proj",
    )(x, g.reshape(1, k), w, wgk, bgk.reshape(1, GLA_KEY_W), *cast)


def _gla_kernel(q_ref, k_ref, v_ref, lahi_ref, lalo_ref, gate_ref, gn_ref, s0_ref,
                o_ref, s_ref, b_ref, *maybe_s_in_ref, rows, seg, zero_init):
    tt = q_ref.shape[0]
    nchunks = tt // rows
    nseg = rows // seg
    seg_shift = seg.bit_length() - 1

    @pl.when(pl.program_id(1) == 0)
    def _():
        if zero_init:
            s_ref[...] = jnp.zeros(s_ref.shape, F32)
        else:
            s_ref[...] = s0_ref[...]

    if maybe_s_in_ref:
        s_in_ref, = maybe_s_in_ref
        s_in_ref[...] = s_ref[...]
    else:
        s_in_ref = s0_ref

    ri = lax.broadcasted_iota(jnp.int32, (rows, rows), 0)
    ci = lax.broadcasted_iota(jnp.int32, (rows, rows), 1)
    same_seg = (ri >> seg_shift) == (ci >> seg_shift)
    causal = jnp.logical_and(same_seg, ci <= ri)
    l_cum = jnp.where(causal, 1.0, 0.0).astype(BF16)
    l_seg = jnp.where(same_seg, 1.0, 0.0).astype(BF16)
    row_seg = lax.broadcasted_iota(jnp.int32, (rows, GLA_DK), 0) >> seg_shift
    row_seg_v = lax.broadcasted_iota(jnp.int32, (rows, GLA_DV), 0) >> seg_shift
    gn = gn_ref[...]
    qscale = GLA_DK ** -0.5

    def cumulative(c):
        rsl = slice(c * rows, (c + 1) * rows)
        lah = lahi_ref[rsl, :]
        lal = lalo_ref[rsl, :]
        b = _dot(l_cum, lah) + _dot(l_cum, lal)
        if nseg == 1:
            b_end = b[rows - 1:rows, :]
        else:
            b_end = _dot(l_seg, lah) + _dot(l_seg, lal)
        return lah, lal, b, b_end

    def pairwise_scores(c, ks, qs_h, b_h):
        sub = 8

        def columns(blk, acc):
            r0 = pl.multiple_of(blk * sub, sub)
            b_blk = b_ref[pl.ds(r0, sub), ks]
            k_blk = k_ref[pl.ds(c * rows + r0, sub), ks]
            for j in range(sub):
                e = jnp.exp(jnp.minimum(b_h - b_blk[j:j + 1, :], 0.0))
                col = jnp.sum(qs_h * e * k_blk[j:j + 1, :], axis=1, keepdims=True)
                acc = acc + jnp.where(ci == r0 + j, col, 0.0)
            return acc

        return lax.fori_loop(0, rows // sub, columns, jnp.zeros((rows, rows), F32))

    def run(exact):
        worst = jnp.float32(0.0)
        ahead = cumulative(0)
        for c in range(nchunks):
            rsl = slice(c * rows, (c + 1) * rows)
            lah, lal, b, b_end = ahead
            if c + 1 < nchunks:
                ahead = cumulative(c + 1)
            qs = q_ref[rsl, :] * qscale
            k = k_ref[rsl, :]
            qt_b = (qs * jnp.exp(b)).astype(BF16)
            kd = k * jnp.exp(b_end - b)
            if exact:
                b_ref[...] = b
            else:
                kt_b = (k * jnp.exp(-b)).astype(BF16)
                worst = jnp.maximum(worst, jnp.max(-b_end))
            heads = []
            for h in range(GLA_HEADS):
                ks = slice(h * GLA_DK, (h + 1) * GLA_DK)
                v_h = v_ref[rsl, h * GLA_DV:(h + 1) * GLA_DV]
                if exact:
                    scores = pairwise_scores(c, ks, qs[:, ks], b[:, ks])
                else:
                    scores = _dot_nt(qt_b[:, ks], kt_b[:, ks])
                inter = None
                new_states = []
                for j in range(nseg):
                    s_old = s_ref[j, h]
                    inter_j = _dot(qt_b[:, ks], s_old.astype(BF16))
                    if nseg == 1:
                        inter = inter_j
                        kd_j = kd[:, ks].astype(BF16)
                        dcol = jnp.broadcast_to(b_end[:, ks], (GLA_DK, GLA_DK)).T
                    else:
                        inter_j = jnp.where(row_seg_v == j, inter_j, 0.0)
                        inter = inter_j if inter is None else inter + inter_j
                        kd_j = jnp.where(row_seg == j, kd[:, ks], 0.0).astype(BF16)
                        ones_j = jnp.where(row_seg == j, 1.0, 0.0).astype(BF16)
                        dcol = _dot_tn(lah[:, ks], ones_j) + _dot_tn(lal[:, ks], ones_j)
                    new_states.append((s_old, dcol, _dot_tn(kd_j, v_h)))
                heads.append((v_h, scores, inter, new_states))
            outs = []
            for v_h, scores, inter, _ in heads:
                a = jnp.where(causal, scores, 0.0).astype(BF16)
                outs.append(_dot(a, v_h) + inter)
            for h, (_, _, _, new_states) in enumerate(heads):
                vs = slice(h * GLA_DV, (h + 1) * GLA_DV)
                for j, (s_old, dcol, upd) in enumerate(new_states):
                    e = jnp.exp(dcol)
                    s_ref[j, h] = s_old * jnp.concatenate([e, e], axis=1) + upd
                on = _rmsnorm(outs[h], gn)
                o_ref[rsl, vs] = (on * gate_ref[rsl, vs].astype(F32)).astype(o_ref.dtype)
        return worst

    worst = run(exact=False)

    @pl.when(worst > GLA_FAST_RANGE)
    def _():
        s_ref[...] = s_in_ref[...]
        run(exact=True)


def gla(qk, v, la_hi, la_lo, gate, gn, s0, *, groups, tt, rows, seg):
    m = qk.shape[0]
    steps = m // (groups * tt)
    nseg = rows // seg
    zero_init = s0 is None
    if zero_init:
        s0 = jnp.zeros((nseg, GLA_HEADS, 8, LANE), F32)
        s0_spec = pl.BlockSpec((nseg, GLA_HEADS, 8, LANE), lambda g, t: (0, 0, 0, 0))
    else:
        s0_spec = pl.BlockSpec((nseg, GLA_HEADS, GLA_DK, GLA_DV), lambda g, t: (g, 0, 0, 0))

    def row_spec(width, blk=0):
        return pl.BlockSpec((tt, width), lambda g, t: (g * steps + t, blk))

    scratch = [pltpu.VMEM((rows, GLA_KEY_W), F32)]
    if zero_init or steps > 1:
        scratch.append(pltpu.VMEM((nseg, GLA_HEADS, GLA_DK, GLA_DV), F32))
    kern = functools.partial(_gla_kernel, rows=rows, seg=seg, zero_init=zero_init)
    return pl.pallas_call(
        kern,
        out_shape=(jax.ShapeDtypeStruct((m, GLA_VAL_W), BF16),
                   jax.ShapeDtypeStruct((groups * nseg, GLA_HEADS, GLA_DK, GLA_DV), F32)),
        grid=(groups, steps),
        in_specs=[
            row_spec(GLA_KEY_W, 0),
            row_spec(GLA_KEY_W, 1),
            row_spec(GLA_VAL_W),
            row_spec(GLA_KEY_W),
            row_spec(GLA_KEY_W),
            row_spec(GLA_VAL_W),
            pl.BlockSpec((1, GLA_DV), lambda g, t: (0, 0)),
            s0_spec,
        ],
        out_specs=(row_spec(GLA_VAL_W),
                   pl.BlockSpec((nseg, GLA_HEADS, GLA_DK, GLA_DV), lambda g, t: (g, 0, 0, 0))),
        scratch_shapes=scratch,
        compiler_params=_params(("parallel", "arbitrary")),
        name="gla",
    )(qk, qk, v, la_hi, la_lo, gate, gn.reshape(1, GLA_DV), s0)


def _pool_sample_kernel(buf_ref, u_ref, d_ref, new_ref, *, pos0):
    t_new = u_ref.shape[0]

    def ext(i, cs=slice(None)):
        return buf_ref[i, :, cs] if i < POOL_BUF else u_ref[i - POOL_BUF, :, cs]

    for r in range(POOL_BUF):
        new_ref[r] = ext(r + t_new)
    for t in range(t_new):
        cur = POOL_BUF + t
        for g, w in enumerate(POOL_WINDOWS):
            cs = slice(g * POOL_G, (g + 1) * POOL_G)
            win = ext(cur, cs)
            for j in range(1, w):
                win = win + ext(cur - j, cs)
            cnt = float(min(pos0 + t + 1, w))
            d_ref[t, :, cs] = win * (1.0 / cnt) - ext(cur, cs)


def pool_sample(buf_tm, u_tm, pos0, nb):
    nbuf, nseq, width = buf_tm.shape
    t_new = u_tm.shape[0]
    buf_spec = pl.BlockSpec((nbuf, nb, width), lambda i: (0, i, 0))
    new_spec = pl.BlockSpec((t_new, nb, width), lambda i: (0, i, 0))
    return pl.pallas_call(
        functools.partial(_pool_sample_kernel, pos0=pos0),
        out_shape=(jax.ShapeDtypeStruct(u_tm.shape, F32), jax.ShapeDtypeStruct(buf_tm.shape, F32)),
        grid=(nseq // nb,),
        in_specs=[buf_spec, new_spec],
        out_specs=(new_spec, buf_spec),
        compiler_params=_params(("parallel",)),
        name="pool_sample",
    )(buf_tm, u_tm)


def _mix_out_kernel(*refs, with_query):
    (o_ref, d_ref, ga_ref, gb_ref, x_ref, wmix_ref, ps_ref, wa_ref, wb_ref, wo_ref) = refs[:10]
    h_ref = refs[12] if with_query else refs[10]
    branch_a = _dot(o_ref[...], wa_ref[...])
    pooled = []
    for g in range(len(POOL_WINDOWS)):
        cs = slice(g * POOL_G, (g + 1) * POOL_G)
        y = _dot(d_ref[:, cs].astype(BF16), wmix_ref[g]) * ps_ref[:, cs]
        pooled.append(y.astype(BF16))
    pooled = jnp.concatenate(pooled, axis=1)
    merged = (jax.nn.sigmoid(ga_ref[...].astype(F32)) * branch_a
              + jax.nn.sigmoid(gb_ref[...].astype(F32)) * _dot(pooled, wb_ref[...]))
    h = x_ref[...] + _dot(merged.astype(BF16), wo_ref[...])
    h_ref[...] = h
    if with_query:
        gq_ref, wq_ref, q_ref = refs[10], refs[11], refs[13]
        q_ref[...] = _dot(_rmsnorm(h, gq_ref[...]).astype(BF16), wq_ref[...])


def mix_out(o, d, ga, gb, x, wmix, pscale, wa, wb, wo, *, tm, query=None):
    m, wide = x.shape
    row_spec = pl.BlockSpec((tm, wide), lambda i: (i, 0))
    const2 = lambda shape: pl.BlockSpec(shape, lambda i: (0, 0))
    in_specs = [row_spec] * 5 + [
        pl.BlockSpec((len(POOL_WINDOWS), POOL_G, POOL_G), lambda i: (0, 0, 0)),
        const2((1, wide)),
        const2((GLA_VAL_W, wide)),
        const2((wide, wide)),
        const2((wide, wide)),
    ]
    args = [o, d, ga, gb, x, wmix, pscale.reshape(1, wide), wa, wb, wo]
    out_shape = jax.ShapeDtypeStruct((m, wide), F32)
    out_specs = row_spec
    if query is not None:
        gq, wq = query
        in_specs += [const2((1, wide)), const2((wide, wide))]
        args += [gq.reshape(1, wide), wq]
        out_shape = (out_shape, out_shape)
        out_specs = (row_spec, row_spec)
    return pl.pallas_call(
        functools.partial(_mix_out_kernel, with_query=query is not None),
        out_shape=out_shape,
        grid=(m // tm,),
        in_specs=in_specs,
        out_specs=out_specs,
        compiler_params=_params(("parallel",)),
        name="mix_out",
    )(*args)


def _softmax_rows(s):
    p = jnp.exp(s - jnp.max(s, axis=-1, keepdims=True))
    return p, 1.0 / jnp.sum(p, axis=-1, keepdims=True)


def _xattn_prompt_kernel(h_ref, g_ref, wq_ref, mem_ref, gm_ref, wk_ref, wv_ref, wo_ref,
                         o_ref, mk_ref, mv_ref, kb_ref, vb_ref, *, tiles):
    groups = 2 * X_HEADS

    @pl.when(pl.program_id(0) % tiles == 0)
    def _():
        mn = _rmsnorm(mem_ref[0], gm_ref[...]).astype(BF16)
        k = _dot(mn, wk_ref[...])
        v = _dot(mn, wv_ref[...])
        kb_ref[...] = k.astype(BF16)
        vb_ref[...] = v.astype(BF16)
        mk_ref[0] = k.reshape(MEM_LEN, groups, LANE)
        mv_ref[0] = v.reshape(MEM_LEN, groups, LANE)

    def head(ref, hd):
        lo = ref[:, hd * LANE:(hd + 1) * LANE]
        hi = ref[:, (X_HEADS + hd) * LANE:(X_HEADS + hd + 1) * LANE]
        return jnp.concatenate([lo, hi], axis=1)

    h = h_ref[...]
    hn = _rmsnorm(h, g_ref[...]).astype(BF16)
    q = (_dot(hn, wq_ref[...]) * (X_HEAD_DIM ** -0.5)).astype(BF16)

    def scores(hd):
        return _dot_nt(q[:, hd * X_HEAD_DIM:(hd + 1) * X_HEAD_DIM], head(kb_ref, hd))

    outs = []
    s_next = scores(0)
    for hd in range(X_HEADS):
        s_cur = s_next
        if hd + 1 < X_HEADS:
            s_next = scores(hd + 1)
        p, inv = _softmax_rows(s_cur)
        outs.append((_dot(p.astype(BF16), head(vb_ref, hd)) * inv).astype(BF16))
    o = jnp.concatenate(outs, axis=1)
    o_ref[...] = h + _dot(o, wo_ref[...])


def xattn_prompt(h, g, wq, mem, gm, wk, wv, wo, *, tm):
    m, d = h.shape
    batch = mem.shape[0]
    tiles = m // batch // tm
    groups = 2 * X_HEADS
    row_spec = pl.BlockSpec((tm, d), lambda i: (i, 0))
    vec_spec = pl.BlockSpec((1, d), lambda i: (0, 0))
    w_spec = pl.BlockSpec((d, d), lambda i: (0, 0))
    kv_spec = pl.BlockSpec((1, MEM_LEN, groups, LANE), lambda i: (i // tiles, 0, 0, 0))
    kv_shape = jax.ShapeDtypeStruct((batch, MEM_LEN, groups, LANE), F32)
    return pl.pallas_call(
        functools.partial(_xattn_prompt_kernel, tiles=tiles),
        out_shape=(jax.ShapeDtypeStruct((m, d), F32), kv_shape, kv_shape),
        grid=(m // tm,),
        in_specs=[row_spec, vec_spec, w_spec,
                  pl.BlockSpec((1, MEM_LEN, d), lambda i: (i // tiles, 0, 0)), vec_spec,
                  w_spec, w_spec, w_spec],
        out_specs=(row_spec, kv_spec, kv_spec),
        scratch_shapes=[pltpu.VMEM((MEM_LEN, d), BF16), pltpu.VMEM((MEM_LEN, d), BF16)],
        compiler_params=_params(("arbitrary",)),
        name="xattn_prompt",
    )(h, g.reshape(1, d), wq, mem, gm.reshape(1, d), wk, wv, wo)


def _sample_attention_probs(q_ref, k_ref):
    nseq, krows, _ = k_ref.shape
    half = XQ_ROWS // 2
    lane = lax.broadcasted_iota(jnp.int32, (half, krows), 1)
    row = lax.broadcasted_iota(jnp.int32, (half, krows), 0)
    valid = (lane & 7) == (row >> 2)
    out = []
    for j in range(nseq):
        q = (q_ref[j * XQ_ROWS:(j + 1) * XQ_ROWS, :] * (X_HEAD_DIM ** -0.5)).astype(BF16)
        g = _dot_nt(q, k_ref[j].astype(BF16))
        s = g[:half] + pltpu.roll(g[half:], krows - 4, axis=1)
        s = jnp.where(valid, s, -1e30)
        p = jnp.exp(s - jnp.max(s, axis=-1, keepdims=True))
        inv = 1.0 / jnp.sum(p, axis=-1, keepdims=True)
        pe = jnp.concatenate([p, pltpu.roll(p, 4, axis=1)], axis=0).astype(BF16)
        out.append((pe, jnp.concatenate([inv, inv], axis=0)))
    return out


def _sample_attention_values(probs, v_ref, o_ref):
    for j, (pe, inv) in enumerate(probs):
        o = _dot(pe, v_ref[j].astype(BF16)) * inv
        o_ref[j * XQ_ROWS:(j + 1) * XQ_ROWS, :] = o.astype(o_ref.dtype)


def _mlp_kernel(*refs, with_attn, with_pre):
    (h_ref, g_ref, wu_ref, wd_ref, gf_ref) = refs[:5]
    rest = refs[5:]
    if with_attn:
        (q_ref, k_ref, v_ref), rest = rest[:3], rest[3:]
    if with_pre:
        (pre_ref, wpre_ref), rest = rest[:2], rest[2:]
    if with_attn:
        y_ref, a_ref, hn_ref, acc_ref = rest
    else:
        y_ref, hn_ref, acc_ref = rest
    f = pl.program_id(1)

    @pl.when(f == 0)
    def _():
        h = h_ref[...]
        if with_pre:
            h = h + _dot(pre_ref[...], wpre_ref[...])
        hn_ref[...] = _rmsnorm(h, g_ref[...]).astype(BF16)
        acc_ref[...] = h

    if with_attn:
        probs = _sample_attention_probs(q_ref, k_ref)
    a = jnp.maximum(_dot(hn_ref[...], wu_ref[...]), 0.0)
    if with_attn:
        _sample_attention_values(probs, v_ref, a_ref)
    acc_ref[...] += _dot((a * a).astype(BF16), wd_ref[...])

    @pl.when(f == pl.num_programs(1) - 1)
    def _():
        y_ref[...] = _rmsnorm(acc_ref[...], gf_ref[...])


def mlp_final(h, g, wu, wd, gf, *, tm, tf, attn=None, pre=None):
    m, d = h.shape
    nf = wu.shape[0]
    row_spec = pl.BlockSpec((tm, d), lambda i, f: (i, 0))
    vec_spec = pl.BlockSpec((1, d), lambda i, f: (0, 0))
    in_specs = [row_spec, vec_spec,
                pl.BlockSpec((None, d, tf), lambda i, f: (f, 0, 0)),
                pl.BlockSpec((tf, d), lambda i, f: (f, 0)),
                vec_spec]
    args = [h, g.reshape(1, d), wu, wd, gf.reshape(1, d)]
    out_shape = jax.ShapeDtypeStruct((m, d), F32)
    out_specs = row_spec
    if attn is not None:
        qhat, ck, cv = attn
        steps = (m // tm) * nf
        nseq = ck.shape[0] // steps
        assert nseq * steps == ck.shape[0]
        q_spec = pl.BlockSpec((nseq * XQ_ROWS, LANE), lambda i, f: (i * nf + f, 0))
        kv_spec = pl.BlockSpec((nseq,) + ck.shape[1:], lambda i, f: (i * nf + f, 0, 0))
        in_specs += [q_spec, kv_spec, kv_spec]
        args += [qhat, ck, cv]
        out_shape = (out_shape, jax.ShapeDtypeStruct(qhat.shape, BF16))
        out_specs = (row_spec, q_spec)
    if pre is not None:
        a, w_pre = pre
        in_specs += [pl.BlockSpec((tm, a.shape[1]), lambda i, f: (i, 0)),
                     pl.BlockSpec(w_pre.shape, lambda i, f: (0, 0))]
        args += [a, w_pre]
    return pl.pallas_call(
        functools.partial(_mlp_kernel, with_attn=attn is not None, with_pre=pre is not None),
        out_shape=out_shape,
        grid=(m // tm, nf),
        in_specs=in_specs,
        out_specs=out_specs,
        scratch_shapes=[pltpu.VMEM((tm, d), BF16), pltpu.VMEM((tm, d), F32)],
        compiler_params=_params(("parallel", "arbitrary")),
        name="mlp_final",
    )(*args)


def _prep_w_in_kernel(win_ref, win_out):
    gz_lo = 2 * GLA_KEY_W + GLA_VAL_W
    gz_hi = gz_lo + GLA_GATE_RANK
    gz_out = IN_COLS["gz"][0]
    win_out[:gz_lo, :] = win_ref[:gz_lo, :].astype(BF16)
    win_out[gz_lo:gz_out, :] = win_ref[gz_hi:, :].astype(BF16)
    win_out[gz_out:gz_out + GLA_GATE_RANK, :] = win_ref[gz_lo:gz_hi, :].astype(BF16)
    win_out[gz_out + GLA_GATE_RANK:, :] = jnp.zeros(
        (PROJ_W - gz_out - GLA_GATE_RANK, win_out.shape[1]), BF16)


def prep_w_in(w_in_t, nblk=8):
    width, d = w_in_t.shape
    return pl.pallas_call(
        _prep_w_in_kernel,
        out_shape=jax.ShapeDtypeStruct((PROJ_W, d), BF16),
        grid=(nblk,),
        in_specs=[pl.BlockSpec((width, d // nblk), lambda i: (0, i))],
        out_specs=pl.BlockSpec((PROJ_W, d // nblk), lambda i: (0, i)),
        compiler_params=_params(("parallel",)),
        name="prep_w_in",
    )(w_in_t)


def _cache_rows(c):
    b, m, h, dh = c.shape
    return c.reshape(b, m, h, dh // LANE, LANE).transpose(0, 1, 3, 2, 4).reshape(b, m * h * (dh // LANE), LANE)


def _mem_kv_output(kv):
    b, m, _, _ = kv.shape
    kv = kv.reshape(b, m, X_HEAD_DIM // LANE, X_HEADS, LANE).transpose(0, 1, 3, 2, 4)
    return kv.reshape(1, b, m, X_HEADS, X_HEAD_DIM)


def kernel(x_prompt, x_sample, mem_prompt, state_gla, state_pool, cache_mem_k, cache_mem_v,
           norm_mix_g, w_in, w_gk_up, b_gk, gla_norm_g, w_pool_mix, pool_scale,
           w_branch_a, w_branch_b, w_out, norm_x_g, norm_mem_g, w_xq, w_xk, w_xv, w_xo,
           norm_mlp_g, w_up, w_down, norm_final_g):
    depth = w_in.shape[0]
    assert depth == 1
    batch, seq, d = x_prompt.shape
    dec_batch, dec_seq, _ = x_sample.shape
    mp = batch * seq
    ms = dec_batch * dec_seq

    w_in_r = prep_w_in(w_in[0].T)
    wgk = jnp.concatenate(
        [w_gk_up[0], jnp.zeros((LANE - GLA_GATE_RANK, GLA_KEY_W), F32)], axis=0).astype(BF16)
    wmix = w_pool_mix[0].astype(BF16)

    xp = x_prompt.reshape(mp, d)
    xs = x_sample.reshape(ms, d)

    (qk_p, v_p, gate_p, u_p, ga_p, gb_p, lah_p, lal_p, d_p,
     wa, wb, wo, wxq, wxo, wu, wd, wxk, wxv) = in_proj(
        xp, norm_mix_g[0], w_in_r, wgk, b_gk[0], TM_PROJ,
        cast=(w_branch_a, w_branch_b, w_out, w_xq, w_xo, w_up, w_down), cast_kv=(w_xk, w_xv),
        pool_seq_len=seq)

    qk_s, v_s, gate_s, u_s, ga_s, gb_s, lah_s, lal_s = in_proj(xs, norm_mix_g[0], w_in_r, wgk, b_gk[0], ms)
    gla_rows = SAMPLE_SEQS_PER_GLA_STEP * dec_seq
    o_s, sg_s = gla(qk_s, v_s, lah_s, lal_s, gate_s, gla_norm_g[0], state_gla[0],
                    groups=ms // gla_rows, tt=gla_rows, rows=gla_rows, seg=dec_seq)
    d_tm, sp_tm = pool_sample(state_pool[0].transpose(1, 0, 2),
                              u_s.reshape(dec_batch, dec_seq, d).transpose(1, 0, 2),
                              PAST_LEN, SAMPLE_SEQS_PER_POOL_STEP)
    h_s, q_s = mix_out(o_s, d_tm.transpose(1, 0, 2).reshape(ms, d), ga_s, gb_s, xs, wmix,
                       pool_scale[0], wa, wb, wo, tm=ms, query=(norm_x_g[0], wxq))
    halves = X_HEAD_DIM // LANE
    qhat = q_s.reshape(dec_batch, dec_seq, X_HEADS, halves, LANE).transpose(0, 3, 2, 1, 4)
    qhat = qhat.reshape(dec_batch * XQ_ROWS, LANE)

    o_p, sg_p = gla(qk_p, v_p, lah_p, lal_p, gate_p, gla_norm_g[0], None,
                    groups=batch, tt=TT_GLA, rows=GLA_CHUNK, seg=GLA_CHUNK)
    h_p = mix_out(o_p, d_p, ga_p, gb_p, xp, wmix, pool_scale[0], wa, wb, wo, tm=TM_MIX)
    h_p, mk_p, mv_p = xattn_prompt(h_p, norm_x_g[0], wxq, mem_prompt, norm_mem_g[0], wxk, wxv, wxo,
                                   tm=TM_XATTN)
    wu = wu.reshape(d, wu.shape[1] // TF_MLP, TF_MLP).transpose(1, 0, 2)
    y_p, a_s = mlp_final(h_p, norm_mlp_g[0], wu, wd, norm_final_g, tm=TM_MLP, tf=TF_MLP,
                         attn=(qhat, _cache_rows(cache_mem_k[0]), _cache_rows(cache_mem_v[0])))
    sp_p = u_p.reshape(batch, HALO, d)[:, HALO - POOL_BUF:]

    a_s = a_s.reshape(dec_batch, halves, X_HEADS, dec_seq, LANE).transpose(0, 3, 2, 1, 4).reshape(ms, d)
    y_s = mlp_final(h_s, norm_mlp_g[0], wu, wd, norm_final_g, tm=ms, tf=TF_MLP, pre=(a_s, wxo))
    sp_s = sp_tm.transpose(1, 0, 2)

    return (y_p.reshape(batch, seq, d),
            y_s.reshape(dec_batch, dec_seq, d),
            _mem_kv_output(mk_p),
            _mem_kv_output(mv_p),
            sg_p[None],
            sg_s[None],
            sp_p[None],
            sp_s[None])
```
